```python
import math, functools
import jax, jax.numpy as jnp
from jax import lax
import numpy as np

D_MODEL = 1024
BATCH = 32
SEQ = 256
DEPTH = 2
DEC_BATCH = 2
DEC_SEQ = 2048
PAST_LEN = 512

GRID_W = 64
N_MIXERS = 2
N_HEADS = 16
HEAD_DIM = D_MODEL // N_HEADS
N_FOURIER_GROUPS = 4
FOURIER_GROUP = D_MODEL // N_FOURIER_GROUPS
WIN_ROWS_MAX = 8
WIN_COLS = 16
D_FF = -(-8 * D_MODEL // (3 * 256)) * 256
N_FOURIER_LAYERS = (DEPTH + N_MIXERS - 1) // N_MIXERS
N_NA_LAYERS = DEPTH // N_MIXERS
N_NORMS = 4
EPS = 1e-6
Q_BLOCK = 128
NEG_INF = -1e30

kernel_name = "hybrid_fourier_natten_prefix_dit_step"


def _rmsnorm(x, g):
    xf = x.astype(jnp.float32)
    y = xf * lax.rsqrt(jnp.mean(xf * xf, axis=-1, keepdims=True) + EPS)
    return (y * g.astype(jnp.float32)).astype(x.dtype)


def _modulation(cond, w, b):
    m = jax.nn.silu(cond) @ w + b
    return jnp.split(m[:, None, :], 6, axis=-1)


def _fourier_mix(h, w_out):
    b, s, d = h.shape
    hg = h.astype(jnp.float32).reshape(b, s, N_FOURIER_GROUPS, FOURIER_GROUP)
    f = jnp.fft.fft2(hg, axes=(1, 3)).real
    return f.reshape(b, s, d).astype(h.dtype) @ w_out


def _qkv(h, w_qkv):
    b, s, _ = h.shape
    q, k, v = jnp.split(h @ w_qkv, 3, axis=-1)
    sh = (b, s, N_HEADS, HEAD_DIM)
    return q.reshape(sh), k.reshape(sh), v.reshape(sh)


def _dense_attn(q, k, v):
    b, s, h, dh = q.shape
    scale = dh ** -0.5
    qb = q.reshape(b, s // Q_BLOCK, Q_BLOCK, h, dh).transpose(1, 0, 2, 3, 4)

    def one_block(qi):
        sc = jnp.einsum('bqhd,bkhd->bhqk', qi, k).astype(jnp.float32) * scale
        p = jax.nn.softmax(sc, axis=-1)
        return jnp.einsum('bhqk,bkhd->bqhd', p.astype(v.dtype), v)

    out = lax.map(one_block, qb)
    return out.transpose(1, 0, 2, 3, 4).reshape(b, s, h, dh)


def _na_latent(q, k, v, ck, cv, rpb):
    b, s, h, dh = q.shape
    rows = s // GRID_W
    kr = min(WIN_ROWS_MAX, rows)
    scale = dh ** -0.5
    qg = q.reshape(b, rows, GRID_W, h, dh)
    kg = k.reshape(b, rows, GRID_W, h, dh)
    vg = v.reshape(b, rows, GRID_W, h, dh)
    r = jnp.arange(rows)
    rs = jnp.clip(r - kr // 2, 0, rows - kr)
    row_idx = rs[:, None] + jnp.arange(kr)[None, :]
    kb = kg[:, row_idx]
    vb = vg[:, row_idx]
    col = jnp.arange(GRID_W)
    cs = jnp.clip(col - WIN_COLS // 2, 0, GRID_W - WIN_COLS)
    col_mask = (col[None, :] >= cs[:, None]) & (col[None, :] < cs[:, None] + WIN_COLS)
    dr = row_idx - r[:, None]
    dc = jnp.clip(col[None, :] - col[:, None], -(WIN_COLS - 1), WIN_COLS - 1)
    bias = rpb[:, dr[:, None, :, None] + (WIN_ROWS_MAX - 1), dc[None, :, None, :] + (WIN_COLS - 1)]
    bias = jnp.where(col_mask[None, None, :, None, :], bias.astype(jnp.float32), NEG_INF)
    s_loc = jnp.einsum('brqhd,brikhd->bhrqik', qg, kb).astype(jnp.float32) * scale + bias[None]
    s_loc = s_loc.reshape(b, h, rows, GRID_W, kr * GRID_W)
    s_ctx = jnp.einsum('brqhd,bnhd->bhrqn', qg, ck).astype(jnp.float32) * scale
    p = jax.nn.softmax(jnp.concatenate([s_loc, s_ctx], axis=-1), axis=-1)
    p_loc = p[..., :kr * GRID_W].reshape(b, h, rows, GRID_W, kr, GRID_W).astype(v.dtype)
    p_ctx = p[..., kr * GRID_W:].astype(v.dtype)
    out = (jnp.einsum('bhrqik,brikhd->brqhd', p_loc, vb)
           + jnp.einsum('bhrqn,bnhd->brqhd', p_ctx, cv))
    return out.reshape(b, s, h, dh)


def _swiglu(h, w_gate, w_up, w_down):
    return (jax.nn.silu(h @ w_gate) * (h @ w_up)) @ w_down


def setup_inputs(seed: int = 0) -> dict:
    key = jax.random.key(seed)
    ks = jax.random.split(key, 16)
    f32 = jnp.float32
    n = lambda k, shape, s: (jax.random.normal(k, shape, f32) * s)
    return {
        "x_prompt": n(ks[0], (BATCH, SEQ, D_MODEL), 1.0),
        "x_sample": n(ks[1], (DEC_BATCH, DEC_SEQ, D_MODEL), 1.0),
        "c": n(ks[2], (DEC_BATCH, D_MODEL), 1.0),
        "cache_k": n(ks[3], (DEC_BATCH, N_NA_LAYERS, PAST_LEN, N_HEADS, HEAD_DIM), 1.0),
        "cache_v": n(ks[4], (DEC_BATCH, N_NA_LAYERS, PAST_LEN, N_HEADS, HEAD_DIM), 1.0),
        "c_ctx": n(ks[5], (D_MODEL,), 1.0),
        "ada_w": n(ks[6], (DEPTH, D_MODEL, 6 * D_MODEL), 0.5 * D_MODEL ** -0.5),
        "ada_b": n(ks[7], (DEPTH, 6 * D_MODEL), 0.02),
        "norm_g": 1.0 + n(ks[8], (DEPTH, N_NORMS, D_MODEL), 0.01),
        "fourier_w_out": n(ks[9], (N_FOURIER_LAYERS, D_MODEL, D_MODEL), D_MODEL ** -0.5),
        "na_w_qkv": n(ks[10], (N_NA_LAYERS, D_MODEL, 3 * D_MODEL), D_MODEL ** -0.5),
        "na_rpb": n(ks[11], (N_NA_LAYERS, N_HEADS, 2 * WIN_ROWS_MAX - 1, 2 * WIN_COLS - 1), 0.1),
        "na_w_out": n(ks[12], (N_NA_LAYERS, D_MODEL, D_MODEL), D_MODEL ** -0.5),
        "ffn_w_gate": n(ks[13], (DEPTH, D_MODEL, D_FF), D_MODEL ** -0.5),
        "ffn_w_up": n(ks[14], (DEPTH, D_MODEL, D_FF), D_MODEL ** -0.5),
        "ffn_w_down": n(ks[15], (DEPTH, D_FF, D_MODEL), D_FF ** -0.5),
    }


def reference(x_prompt, x_sample, c, cache_k, cache_v, c_ctx, ada_w, ada_b, norm_g,
              fourier_w_out, na_w_qkv, na_rpb, na_w_out, ffn_w_gate, ffn_w_up, ffn_w_down):
    xp, xs = x_prompt, x_sample
    new_k, new_v = [], []
    for i in range(DEPTH):
        g = norm_g[i]
        sh_mp, sc_mp, gt_mp, sh_fp, sc_fp, gt_fp = _modulation(c_ctx[None, :], ada_w[i], ada_b[i])
        sh_ms, sc_ms, gt_ms, sh_fs, sc_fs, gt_fs = _modulation(c, ada_w[i], ada_b[i])
        hp = _rmsnorm(xp, g[0]) * (1.0 + sc_mp) + sh_mp
        hs = _rmsnorm(xs, g[0]) * (1.0 + sc_ms) + sh_ms
        j = i // N_MIXERS
        if i % N_MIXERS == 0:
            mp = _fourier_mix(hp, fourier_w_out[j])
            ms = _fourier_mix(hs, fourier_w_out[j])
        else:
            qp, kp, vp = _qkv(hp, na_w_qkv[j])
            new_k.append(kp)
            new_v.append(vp)
            op = _dense_attn(qp, kp, vp)
            mp = op.reshape(xp.shape) @ na_w_out[j]
            qs, ks_, vs = _qkv(hs, na_w_qkv[j])
            os_ = _na_latent(qs, ks_, vs, cache_k[:, j], cache_v[:, j], na_rpb[j])
            ms = os_.reshape(xs.shape) @ na_w_out[j]
        xp = xp + gt_mp * _rmsnorm(mp, g[1])
        xs = xs + gt_ms * _rmsnorm(ms, g[1])
        fp = _rmsnorm(xp, g[2]) * (1.0 + sc_fp) + sh_fp
        fs = _rmsnorm(xs, g[2]) * (1.0 + sc_fs) + sh_fs
        xp = xp + gt_fp * _rmsnorm(_swiglu(fp, ffn_w_gate[i], ffn_w_up[i], ffn_w_down[i]), g[3])
        xs = xs + gt_fs * _rmsnorm(_swiglu(fs, ffn_w_gate[i], ffn_w_up[i], ffn_w_down[i]), g[3])
    new_cache_k = jnp.stack(new_k, axis=1)
    new_cache_v = jnp.stack(new_v, axis=1)
    return (xp, xs, new_cache_k, new_cache_v)
```

```python
import functools

import numpy as np
import jax
import jax.numpy as jnp
from jax import lax
from jax.experimental import pallas as pl
from jax.experimental.pallas import tpu as pltpu

D_MODEL = 1024
BATCH = 32
SEQ = 256
DEPTH = 2
DEC_BATCH = 2
DEC_SEQ = 2048
PAST_LEN = 512
GRID_W = 64
GRID_ROWS = DEC_SEQ // GRID_W
N_HEADS = 16
HEAD_DIM = D_MODEL // N_HEADS
N_FOURIER_GROUPS = 4
FOURIER_GROUP = D_MODEL // N_FOURIER_GROUPS
WIN_ROWS = 8
WIN_COLS = 16
D_FF = 2816
EPS = 1e-6
NEG_INF = -1e30

N_PAIRS = N_HEADS // 2
PAIR_W = 2 * HEAD_DIM
N_DR = 2 * WIN_ROWS - 1
N_DR_PAIRS = N_DR - 1
COND_ROWS = 8
Q_SCALE = HEAD_DIM ** -0.5

VMEM_LIMIT = 56 * 1024 * 1024

F32 = jnp.float32
BF16 = jnp.bfloat16


def _dot(a, b):
    return jnp.dot(a, b, preferred_element_type=F32)


def _dot_nt(a, b):
    return lax.dot_general(a, b, (((1,), (1,)), ((), ())), preferred_element_type=F32)


def _rms(x, g):
    ms = jnp.mean(x * x, axis=-1, keepdims=True)
    return x * lax.rsqrt(ms + EPS) * g


def _silu(x):
    return x / (1.0 + jnp.exp(-x))


def _const_spec(shape):
    return pl.BlockSpec(shape, lambda *_: (0,) * len(shape), pipeline_mode=pl.Buffered(1))


def _mod_spec(row_of_step):
    return pl.BlockSpec((None, 1, 6 * D_MODEL), lambda *idx: (row_of_step(*idx), 0, 0))


def _params(n_axes):
    return pltpu.CompilerParams(
        dimension_semantics=("arbitrary",) * n_axes, vmem_limit_bytes=VMEM_LIMIT)


def _mod_kernel(cond_ref, w_ref, b_ref, o_ref):
    a = _silu(cond_ref[...])
    w = w_ref[...]
    a_hi = a.astype(BF16)
    a_lo = (a - a_hi.astype(F32)).astype(BF16)
    w_hi = w.astype(BF16)
    w_lo = (w - w_hi.astype(F32)).astype(BF16)
    o_ref[...] = _dot(a_hi, w_hi) + _dot(a_lo, w_hi) + _dot(a_hi, w_lo) + b_ref[...]


def _modulation(cond, ada_w, ada_b):
    tn = 1536
    n_out = 6 * D_MODEL
    return pl.pallas_call(
        _mod_kernel,
        grid=(DEPTH, n_out // tn),
        in_specs=[
            pl.BlockSpec((COND_ROWS, D_MODEL), lambda l, n: (0, 0)),
            pl.BlockSpec((None, D_MODEL, tn), lambda l, n: (l, 0, n)),
            pl.BlockSpec((None, 1, tn), lambda l, n: (l, 0, n)),
        ],
        out_specs=pl.BlockSpec((None, COND_ROWS, tn), lambda l, n: (l, 0, n)),
        out_shape=jax.ShapeDtypeStruct((DEPTH, COND_ROWS, n_out), F32),
        compiler_params=_params(2),
        name="modulation",
    )(cond, ada_w, ada_b.reshape(DEPTH, 1, n_out))


def _ffn_kernel(x_ref, mod_ref, g_ref, wg_ref, wu_ref, wd_ref, o_ref):
    x = x_ref[...]
    m = mod_ref[...]
    sh = m[:, 3 * D_MODEL:4 * D_MODEL]
    sc = m[:, 4 * D_MODEL:5 * D_MODEL]
    gt = m[:, 5 * D_MODEL:6 * D_MODEL]
    f = (_rms(x, g_ref[2:3, :]) * (1.0 + sc) + sh).astype(BF16)
    gate = _dot(f, wg_ref[...])
    up = _dot(f, wu_ref[...])
    act = (_silu(gate) * up).astype(BF16)
    y = _dot(act, wd_ref[...])
    o_ref[...] = x + gt * _rms(y, g_ref[3:4, :])


def _ffn(x, mod, row_of_step, g, wg, wu, wd, tm=512):
    t = x.shape[0]
    return pl.pallas_call(
        _ffn_kernel,
        grid=(t // tm,),
        in_specs=[
            pl.BlockSpec((tm, D_MODEL), lambda i: (i, 0)),
            _mod_spec(row_of_step),
            _const_spec((4, D_MODEL)),
            _const_spec((D_MODEL, D_FF)),
            _const_spec((D_MODEL, D_FF)),
            _const_spec((D_FF, D_MODEL)),
        ],
        out_specs=pl.BlockSpec((tm, D_MODEL), lambda i: (i, 0)),
        out_shape=jax.ShapeDtypeStruct((t, D_MODEL), F32),
        compiler_params=_params(1),
        name="ffn",
    )(x, mod, g, wg, wu, wd)


def _dft_cos_sin(n):
    j = np.arange(n)
    ang = 2.0 * np.pi * ((j[:, None] * j[None, :]) % n) / n
    return np.cos(ang), np.sin(ang)


def _premix_cdft_kernel(x_ref, mod_ref, g_ref, cs_ref, xc_ref, xs_ref):
    x = x_ref[...]
    m = mod_ref[...]
    sh = m[:, 0:D_MODEL]
    sc = m[:, D_MODEL:2 * D_MODEL]
    h = (_rms(x, g_ref[0:1, :]) * (1.0 + sc) + sh).astype(BF16)
    cs = cs_ref[...]
    for j in range(N_FOURIER_GROUPS):
        lanes = slice(FOURIER_GROUP * j, FOURIER_GROUP * (j + 1))
        r = _dot(h[:, lanes], cs)
        xc_ref[:, lanes] = r[:, :FOURIER_GROUP].astype(BF16)
        xs_ref[:, lanes] = r[:, FOURIER_GROUP:].astype(BF16)


def _premix_cdft(x, mod, row_of_step, g, cs, tm=512):
    t = x.shape[0]
    out = jax.ShapeDtypeStruct((t, D_MODEL), BF16)
    return pl.pallas_call(
        _premix_cdft_kernel,
        grid=(t // tm,),
        in_specs=[
            pl.BlockSpec((tm, D_MODEL), lambda i: (i, 0)),
            _mod_spec(row_of_step),
            _const_spec((4, D_MODEL)),
            _const_spec((FOURIER_GROUP, 2 * FOURIER_GROUP)),
        ],
        out_specs=[pl.BlockSpec((tm, D_MODEL), lambda i: (i, 0))] * 2,
        out_shape=[out, out],
        compiler_params=_params(1),
        name="premix_cdft",
    )(x, mod, g, cs)


def _seqdft_out_kernel(c_ref, s_ref, xc_ref, xs_ref, x_ref, mod_ref, g_ref, w_ref, o_ref):
    f = _dot(c_ref[...], xc_ref[...]) - _dot(s_ref[...], xs_ref[...])
    mix = _dot(f.astype(BF16), w_ref[...])
    gt = mod_ref[...][:, 2 * D_MODEL:3 * D_MODEL]
    o_ref[...] = x_ref[...] + gt * _rms(mix, g_ref[1:2, :])


def _seqdft_out(cmat, smat, xc, xs, x, mod, row_of_batch, g, w, n_batch, seq, tr=256):
    n_t = seq // tr
    return pl.pallas_call(
        _seqdft_out_kernel,
        grid=(n_batch, n_t),
        in_specs=[
            pl.BlockSpec((tr, seq), lambda b, t: (t, 0)),
            pl.BlockSpec((tr, seq), lambda b, t: (t, 0)),
            pl.BlockSpec((seq, D_MODEL), lambda b, t: (b, 0)),
            pl.BlockSpec((seq, D_MODEL), lambda b, t: (b, 0)),
            pl.BlockSpec((tr, D_MODEL), lambda b, t: (b * n_t + t, 0)),
            _mod_spec(lambda b, t: row_of_batch(b)),
            _const_spec((4, D_MODEL)),
            _const_spec((D_MODEL, D_MODEL)),
        ],
        out_specs=pl.BlockSpec((tr, D_MODEL), lambda b, t: (b * n_t + t, 0)),
        out_shape=jax.ShapeDtypeStruct((n_batch * seq, D_MODEL), F32),
        compiler_params=_params(2),
        name="seqdft_out",
    )(cmat, smat, xc, xs, x, mod, g, w)


def _split_pair(q, lo):
    zero = jnp.zeros_like(q)
    return jnp.concatenate([jnp.where(lo, q, zero), jnp.where(lo, zero, q)], axis=0)


def _prompt_attn_kernel(x_ref, mod_ref, g_ref, wqkv_ref, wout_ref, o_ref, k_ref, v_ref,
                        q_scr, k_scr, v_scr, att_scr, *, n_seq):
    x = x_ref[...]
    m = mod_ref[...]
    sh = m[:, 0:D_MODEL]
    sc = m[:, D_MODEL:2 * D_MODEL]
    gt = m[:, 2 * D_MODEL:3 * D_MODEL]
    h = (_rms(x, g_ref[0:1, :]) * (1.0 + sc) + sh).astype(BF16)
    qkv = _dot(h, wqkv_ref[...])
    k = qkv[:, D_MODEL:2 * D_MODEL]
    v = qkv[:, 2 * D_MODEL:3 * D_MODEL]
    k_ref[...] = k
    v_ref[...] = v
    q_scr[...] = (qkv[:, 0:D_MODEL] * Q_SCALE).astype(BF16)
    k_scr[...] = k.astype(BF16)
    v_scr[...] = v.astype(BF16)

    lo = lax.broadcasted_iota(jnp.int32, (SEQ, PAIR_W), 1) < HEAD_DIM
    for b in range(n_seq):
        rows = slice(SEQ * b, SEQ * (b + 1))
        for j in range(N_PAIRS):
            lanes = slice(PAIR_W * j, PAIR_W * (j + 1))
            qs = _split_pair(q_scr[rows, lanes], lo)
            s = _dot_nt(qs, k_scr[rows, lanes])
            p = jnp.exp(s - jnp.max(s, axis=-1, keepdims=True))
            den = jnp.sum(p, axis=-1, keepdims=True)
            o = _dot(p.astype(BF16), v_scr[rows, lanes]) / den
            att_scr[rows, lanes] = jnp.where(lo, o[:SEQ], o[SEQ:]).astype(BF16)

    mix = _dot(att_scr[...], wout_ref[...])
    o_ref[...] = x + gt * _rms(mix, g_ref[1:2, :])


def _prompt_attn(x, mod, row, g, wqkv, wout, n_seq=2):
    t = x.shape[0]
    tm = n_seq * SEQ
    tok = pl.BlockSpec((tm, D_MODEL), lambda i: (i, 0))
    out = jax.ShapeDtypeStruct((t, D_MODEL), F32)
    return pl.pallas_call(
        functools.partial(_prompt_attn_kernel, n_seq=n_seq),
        grid=(t // tm,),
        in_specs=[
            tok,
            _mod_spec(lambda i: row),
            _const_spec((4, D_MODEL)),
            _const_spec((D_MODEL, 3 * D_MODEL)),
            _const_spec((D_MODEL, D_MODEL)),
        ],
        out_specs=[tok, tok, tok],
        out_shape=[out, out, out],
        scratch_shapes=[pltpu.VMEM((tm, D_MODEL), BF16)] * 4,
        compiler_params=_params(1),
        name="prompt_attn",
    )(x, mod, g, wqkv, wout)


def _premix_qkv_kernel(x_ref, mod_ref, g_ref, wqkv_ref, q_ref, k_ref, v_ref):
    x = x_ref[...]
    m = mod_ref[...]
    sh = m[:, 0:D_MODEL]
    sc = m[:, D_MODEL:2 * D_MODEL]
    h = (_rms(x, g_ref[0:1, :]) * (1.0 + sc) + sh).astype(BF16)
    qkv = _dot(h, wqkv_ref[...])
    q_ref[...] = (qkv[:, 0:D_MODEL] * Q_SCALE).astype(BF16)
    k_ref[...] = qkv[:, D_MODEL:2 * D_MODEL].astype(BF16)
    v_ref[...] = qkv[:, 2 * D_MODEL:3 * D_MODEL].astype(BF16)


def _premix_qkv(x, mod, row_of_step, g, wqkv, tm=512):
    t = x.shape[0]
    tok = pl.BlockSpec((tm, D_MODEL), lambda i: (i, 0))
    out = jax.ShapeDtypeStruct((t, D_MODEL), BF16)
    return pl.pallas_call(
        _premix_qkv_kernel,
        grid=(t // tm,),
        in_specs=[tok, _mod_spec(row_of_step), _const_spec((4, D_MODEL)),
                  _const_spec((D_MODEL, 3 * D_MODEL))],
        out_specs=[tok, tok, tok],
        out_shape=[out, out, out],
        compiler_params=_params(1),
        name="premix_qkv",
    )(x, mod, g, wqkv)


def _bias_table_kernel(l_ref, o_ref):
    lane = lax.broadcasted_iota(jnp.int32, (GRID_W, PAIR_W), 1)
    qcol = lax.broadcasted_iota(jnp.int32, (GRID_W, PAIR_W), 0)
    kcol = lane & (GRID_W - 1)
    start = jnp.clip(qcol - WIN_COLS // 2, 0, GRID_W - WIN_COLS)
    in_window = (kcol >= start) & (kcol < start + WIN_COLS)
    lo = lane < GRID_W

    def toeplitz(d, shift):
        row = jnp.broadcast_to(l_ref[d:d + 1, :], (GRID_W, PAIR_W))
        return pltpu.roll(row, shift, 1, stride=1, stride_axis=0)

    for d in range(N_DR_PAIRS):
        both = jnp.where(lo, toeplitz(d, GRID_W + 1), toeplitz(d + 1, 1))
        o_ref[d] = jnp.where(in_window, both, NEG_INF)


def _bias_table(rpb):
    padded = jnp.pad(rpb, ((0, 0), (0, 0), (48, 49)), mode="edge")
    return pl.pallas_call(
        _bias_table_kernel,
        grid=(N_HEADS,),
        in_specs=[pl.BlockSpec((None, N_DR, PAIR_W), lambda h: (h, 0, 0))],
        out_specs=pl.BlockSpec((None, N_DR_PAIRS, GRID_W, PAIR_W), lambda h: (h, 0, 0, 0)),
        out_shape=jax.ShapeDtypeStruct((N_HEADS, N_DR_PAIRS, GRID_W, PAIR_W), F32),
        compiler_params=_params(1),
        name="bias_table",
    )(padded)


def _na_attn_kernel(q_ref, k_ref, v_ref, ck_ref, cv_ref, bias_ref, x_ref, mod_ref, g_ref,
                    wout_ref, o_ref, att_scr, *, rows_per_step):
    blk = pl.program_id(1)
    lo = lax.broadcasted_iota(jnp.int32, (GRID_W, PAIR_W), 1) < HEAD_DIM
    n_local = WIN_ROWS * GRID_W

    def one_row(i, carry):
        r = blk * rows_per_step + i
        rs = jnp.clip(r - WIN_ROWS // 2, 0, GRID_ROWS - WIN_ROWS)
        d0 = rs - r + (WIN_ROWS - 1)
        q0 = pl.multiple_of(i * GRID_W, GRID_W)
        k0 = pl.multiple_of(rs * GRID_W, GRID_W)
        for j in range(N_PAIRS):
            lanes = slice(PAIR_W * j, PAIR_W * (j + 1))
            qs = _split_pair(q_ref[pl.ds(q0, GRID_W), lanes], lo)
            bias = jnp.concatenate(
                [jnp.concatenate(
                    [bias_ref[(2 * j) * N_DR_PAIRS + d0 + 2 * jj],
                     bias_ref[(2 * j + 1) * N_DR_PAIRS + d0 + 2 * jj]], axis=0)
                 for jj in range(WIN_ROWS // 2)], axis=1)
            s_loc = _dot_nt(qs, k_ref[pl.ds(k0, n_local), lanes]) + bias
            s_ctx = _dot_nt(qs, ck_ref[:, lanes])
            mx = jnp.maximum(jnp.max(s_loc, axis=-1, keepdims=True),
                             jnp.max(s_ctx, axis=-1, keepdims=True))
            p_loc = jnp.exp(s_loc - mx)
            p_ctx = jnp.exp(s_ctx - mx)
            den = (jnp.sum(p_loc, axis=-1, keepdims=True)
                   + jnp.sum(p_ctx, axis=-1, keepdims=True))
            o = (_dot(p_loc.astype(BF16), v_ref[pl.ds(k0, n_local), lanes])
                 + _dot(p_ctx.astype(BF16), cv_ref[:, lanes])) / den
            att_scr[pl.ds(q0, GRID_W), lanes] = (
                jnp.where(lo, o[:GRID_W], o[GRID_W:]).astype(BF16))
        return carry

    lax.fori_loop(0, rows_per_step, one_row, 0)

    gt = mod_ref[...][:, 2 * D_MODEL:3 * D_MODEL]
    mix = _dot(att_scr[...], wout_ref[...])
    o_ref[...] = x_ref[...] + gt * _rms(mix, g_ref[1:2, :])


def _na_attn(q, k, v, ck, cv, bias, x, mod, row_of_batch, g, wout, rows_per_step=8):
    tm = rows_per_step * GRID_W
    n_t = DEC_SEQ // tm
    tok = pl.BlockSpec((tm, D_MODEL), lambda b, t: (b * n_t + t, 0))
    per_batch = pl.BlockSpec((DEC_SEQ, D_MODEL), lambda b, t: (b, 0))
    ctx = pl.BlockSpec((PAST_LEN, D_MODEL), lambda b, t: (b, 0))
    return pl.pallas_call(
        functools.partial(_na_attn_kernel, rows_per_step=rows_per_step),
        grid=(DEC_BATCH, n_t),
        in_specs=[
            tok, per_batch, per_batch, ctx, ctx,
            _const_spec((N_HEADS * N_DR_PAIRS, GRID_W, PAIR_W)),
            tok,
            _mod_spec(lambda b, t: row_of_batch(b)),
            _const_spec((4, D_MODEL)),
            _const_spec((D_MODEL, D_MODEL)),
        ],
        out_specs=tok,
        out_shape=jax.ShapeDtypeStruct((DEC_BATCH * DEC_SEQ, D_MODEL), F32),
        scratch_shapes=[pltpu.VMEM((tm, D_MODEL), BF16)],
        compiler_params=_params(2),
        name="na_attn",
    )(q, k, v, ck, cv, bias, x, mod, g, wout)


def kernel(x_prompt, x_sample, c, cache_k, cache_v, c_ctx, ada_w, ada_b, norm_g, fourier_w_out,
           na_w_qkv, na_rpb, na_w_out, ffn_w_gate, ffn_w_up, ffn_w_down):
    n_p = BATCH * SEQ
    n_s = DEC_BATCH * DEC_SEQ
    xp = x_prompt.reshape(n_p, D_MODEL)
    xs = x_sample.reshape(n_s, D_MODEL)

    cond = jnp.concatenate(
        [c_ctx[None, :], c, jnp.zeros((COND_ROWS - 1 - DEC_BATCH, D_MODEL), F32)], axis=0)
    mod = _modulation(cond, ada_w, ada_b).reshape(DEPTH * COND_ROWS, 1, 6 * D_MODEL)

    cos_g, sin_g = _dft_cos_sin(FOURIER_GROUP)
    cs_chan = jnp.asarray(np.concatenate([cos_g, sin_g], axis=1), F32).astype(BF16)
    cos_p, sin_p = (jnp.asarray(a, F32).astype(BF16) for a in _dft_cos_sin(SEQ))
    cos_s, sin_s = (jnp.asarray(a, F32).astype(BF16) for a in _dft_cos_sin(DEC_SEQ))

    tm = 512
    new_k = new_v = None
    for layer in range(DEPTH):
        g = norm_g[layer]
        base = layer * COND_ROWS
        prompt_row = lambda *_, base=base: base
        sample_row_of_batch = lambda b, base=base: base + 1 + b
        sample_row_of_tile = lambda i, base=base: base + 1 + i // (DEC_SEQ // tm)

        if layer % 2 == 0:
            w_out = fourier_w_out[layer // 2].astype(BF16)
            xc, xsn = _premix_cdft(xp, mod, prompt_row, g, cs_chan, tm)
            xp = _seqdft_out(cos_p, sin_p, xc, xsn, xp, mod, prompt_row, g, w_out, BATCH, SEQ)
            xc, xsn = _premix_cdft(xs, mod, sample_row_of_tile, g, cs_chan, tm)
            xs = _seqdft_out(cos_s, sin_s, xc, xsn, xs, mod, sample_row_of_batch, g, w_out,
                             DEC_BATCH, DEC_SEQ)
        else:
            j = layer // 2
            w_qkv = na_w_qkv[j].astype(BF16)
            w_out = na_w_out[j].astype(BF16)
            xp, new_k, new_v = _prompt_attn(xp, mod, base, g, w_qkv, w_out)
            q, k, v = _premix_qkv(xs, mod, sample_row_of_tile, g, w_qkv, tm)
            ck = cache_k[:, j].reshape(n_s // DEC_SEQ * PAST_LEN, D_MODEL).astype(BF16)
            cv = cache_v[:, j].reshape(n_s // DEC_SEQ * PAST_LEN, D_MODEL).astype(BF16)
            bias = _bias_table(na_rpb[j]).reshape(N_HEADS * N_DR_PAIRS, GRID_W, PAIR_W)
            xs = _na_attn(q, k, v, ck, cv, bias, xs, mod, sample_row_of_batch, g, w_out)

        wg = ffn_w_gate[layer].astype(BF16)
        wu = ffn_w_up[layer].astype(BF16)
        wd = ffn_w_down[layer].astype(BF16)
        xp = _ffn(xp, mod, prompt_row, g, wg, wu, wd, tm)
        xs = _ffn(xs, mod, sample_row_of_tile, g, wg, wu, wd, tm)

    cache_shape = (BATCH, 1, SEQ, N_HEADS, HEAD_DIM)
    return (xp.reshape(BATCH, SEQ, D_MODEL), xs.reshape(DEC_BATCH, DEC_SEQ, D_MODEL),
            new_k.reshape(cache_shape), new_v.reshape(cache_shape))
```

```python
import functools

import numpy as np
import jax
import jax.numpy as jnp
from jax import lax
from jax.experimental import pallas as pl
from jax.experimental.pallas import tpu as pltpu

D_MODEL = 1024
BATCH = 32
SEQ = 256
DEPTH = 2
DEC_BATCH = 2
DEC_SEQ = 2048
PAST_LEN = 512
GRID_W = 64
GRID_ROWS = DEC_SEQ // GRID_W
N_HEADS = 16
HEAD_DIM = D_MODEL // N_HEADS
N_FOURIER_GROUPS = 4
FOURIER_GROUP = D_MODEL // N_FOURIER_GROUPS
WIN_ROWS = 8
WIN_COLS = 16
D_FF = 2816
EPS = 1e-6
NEG_INF = -1e30

N_PAIRS = N_HEADS // 2
PAIR_W = 2 * HEAD_DIM
N_DR = 2 * WIN_ROWS - 1
N_DR_PAIRS = N_DR - 1
N_LOCAL = WIN_ROWS * GRID_W
N_KEYS = N_LOCAL + PAST_LEN
COND_ROWS = 8
Q_SCALE = HEAD_DIM ** -0.5

VMEM_LIMIT = 56 * 1024 * 1024

F32 = jnp.float32
BF16 = jnp.bfloat16


def _dot(a, b):
    return jnp.dot(a, b, preferred_element_type=F32)


def _dot_nt(a, b):
    return lax.dot_general(a, b, (((1,), (1,)), ((), ())), preferred_element_type=F32)


def _rms(x, g):
    ms = jnp.mean(x * x, axis=-1, keepdims=True)
    return x * lax.rsqrt(ms + EPS) * g


def _silu(x):
    return x / (1.0 + jnp.exp(-x))


def _const_spec(shape):
    return pl.BlockSpec(shape, lambda *_: (0,) * len(shape), pipeline_mode=pl.Buffered(1))


def _mod_spec(row_of_step):
    return pl.BlockSpec((None, 1, 6 * D_MODEL), lambda *idx: (row_of_step(*idx), 0, 0))


def _params(n_axes):
    return pltpu.CompilerParams(
        dimension_semantics=("arbitrary",) * n_axes, vmem_limit_bytes=VMEM_LIMIT)


def _mod_kernel(cond_ref, w_ref, b_ref, o_ref):
    a = _silu(cond_ref[...])
    w = w_ref[...]
    a_hi = a.astype(BF16)
    a_lo = (a - a_hi.astype(F32)).astype(BF16)
    w_hi = w.astype(BF16)
    w_lo = (w - w_hi.astype(F32)).astype(BF16)
    o_ref[...] = _dot(a_hi, w_hi) + _dot(a_lo, w_hi) + _dot(a_hi, w_lo) + b_ref[...]


def _modulation(cond, ada_w, ada_b):
    tn = 1536
    n_out = 6 * D_MODEL
    return pl.pallas_call(
        _mod_kernel,
        grid=(DEPTH, n_out // tn),
        in_specs=[
            pl.BlockSpec((COND_ROWS, D_MODEL), lambda l, n: (0, 0)),
            pl.BlockSpec((None, D_MODEL, tn), lambda l, n: (l, 0, n)),
            pl.BlockSpec((None, 1, tn), lambda l, n: (l, 0, n)),
        ],
        out_specs=pl.BlockSpec((None, COND_ROWS, tn), lambda l, n: (l, 0, n)),
        out_shape=jax.ShapeDtypeStruct((DEPTH, COND_ROWS, n_out), F32),
        compiler_params=_params(2),
        name="modulation",
    )(cond, ada_w, ada_b.reshape(DEPTH, 1, n_out))


def _ffn_kernel(x_ref, mod_ref, g_ref, wg_ref, wu_ref, wd_ref, o_ref):
    x = x_ref[...]
    m = mod_ref[...]
    sh = m[:, 3 * D_MODEL:4 * D_MODEL]
    sc = m[:, 4 * D_MODEL:5 * D_MODEL]
    gt = m[:, 5 * D_MODEL:6 * D_MODEL]
    f = (_rms(x, g_ref[2:3, :]) * (1.0 + sc) + sh).astype(BF16)
    gate = _dot(f, wg_ref[...])
    up = _dot(f, wu_ref[...])
    act = (_silu(gate) * up).astype(BF16)
    y = _dot(act, wd_ref[...])
    o_ref[...] = x + gt * _rms(y, g_ref[3:4, :])


def _ffn(x, mod, row_of_step, g, wg, wu, wd, tm=512):
    t = x.shape[0]
    return pl.pallas_call(
        _ffn_kernel,
        grid=(t // tm,),
        in_specs=[
            pl.BlockSpec((tm, D_MODEL), lambda i: (i, 0)),
            _mod_spec(row_of_step),
            _const_spec((4, D_MODEL)),
            _const_spec((D_MODEL, D_FF)),
            _const_spec((D_MODEL, D_FF)),
            _const_spec((D_FF, D_MODEL)),
        ],
        out_specs=pl.BlockSpec((tm, D_MODEL), lambda i: (i, 0)),
        out_shape=jax.ShapeDtypeStruct((t, D_MODEL), F32),
        compiler_params=_params(1),
        name="ffn",
    )(x, mod, g, wg, wu, wd)


def _dft_cos_sin(n):
    j = np.arange(n)
    ang = 2.0 * np.pi * ((j[:, None] * j[None, :]) % n) / n
    return np.cos(ang), np.sin(ang)


def _premix_cdft_kernel(x_ref, mod_ref, g_ref, cs_ref, xc_ref, xs_ref):
    x = x_ref[...]
    m = mod_ref[...]
    sh = m[:, 0:D_MODEL]
    sc = m[:, D_MODEL:2 * D_MODEL]
    h = (_rms(x, g_ref[0:1, :]) * (1.0 + sc) + sh).astype(BF16)
    cs = cs_ref[...]
    for j in range(N_FOURIER_GROUPS):
        lanes = slice(FOURIER_GROUP * j, FOURIER_GROUP * (j + 1))
        r = _dot(h[:, lanes], cs)
        xc_ref[:, lanes] = r[:, :FOURIER_GROUP].astype(BF16)
        xs_ref[:, lanes] = r[:, FOURIER_GROUP:].astype(BF16)


def _premix_cdft(x, mod, row_of_step, g, cs, tm=512):
    t = x.shape[0]
    out = jax.ShapeDtypeStruct((t, D_MODEL), BF16)
    return pl.pallas_call(
        _premix_cdft_kernel,
        grid=(t // tm,),
        in_specs=[
            pl.BlockSpec((tm, D_MODEL), lambda i: (i, 0)),
            _mod_spec(row_of_step),
            _const_spec((4, D_MODEL)),
            _const_spec((FOURIER_GROUP, 2 * FOURIER_GROUP)),
        ],
        out_specs=[pl.BlockSpec((tm, D_MODEL), lambda i: (i, 0))] * 2,
        out_shape=[out, out],
        compiler_params=_params(1),
        name="premix_cdft",
    )(x, mod, g, cs)


def _seqdft_out_kernel(c_ref, s_ref, xc_ref, xs_ref, x_ref, mod_ref, g_ref, w_ref, o_ref, f_scr,
                       *, n_sub, seq, tr):
    for b in range(n_sub):
        src = slice(seq * b, seq * (b + 1))
        f = _dot(c_ref[...], xc_ref[src, :]) - _dot(s_ref[...], xs_ref[src, :])
        f_scr[tr * b:tr * (b + 1), :] = f.astype(BF16)
    mix = _dot(f_scr[...], w_ref[...])
    gt = mod_ref[...][:, 2 * D_MODEL:3 * D_MODEL]
    o_ref[...] = x_ref[...] + gt * _rms(mix, g_ref[1:2, :])


def _seqdft_out(cmat, smat, xc, xs, x, mod, row_of_group, g, w, n_seqs, seq, n_sub, tr):
    n_t = seq // tr
    tok = pl.BlockSpec((n_sub * tr, D_MODEL), lambda b, t: (b * n_t + t, 0))
    src = pl.BlockSpec((n_sub * seq, D_MODEL), lambda b, t: (b, 0))
    return pl.pallas_call(
        functools.partial(_seqdft_out_kernel, n_sub=n_sub, seq=seq, tr=tr),
        grid=(n_seqs // n_sub, n_t),
        in_specs=[
            pl.BlockSpec((tr, seq), lambda b, t: (t, 0)),
            pl.BlockSpec((tr, seq), lambda b, t: (t, 0)),
            src, src, tok,
            _mod_spec(lambda b, t: row_of_group(b)),
            _const_spec((4, D_MODEL)),
            _const_spec((D_MODEL, D_MODEL)),
        ],
        out_specs=tok,
        out_shape=jax.ShapeDtypeStruct((n_seqs * seq, D_MODEL), F32),
        scratch_shapes=[pltpu.VMEM((n_sub * tr, D_MODEL), BF16)],
        compiler_params=_params(2),
        name="seqdft_out",
    )(cmat, smat, xc, xs, x, mod, g, w)


def _split_pair(q, lo):
    zero = jnp.zeros_like(q)
    return jnp.concatenate([jnp.where(lo, q, zero), jnp.where(lo, zero, q)], axis=0)


def _prompt_attn_kernel(x_ref, mod_ref, g_ref, wqkv_ref, wout_ref, o_ref, k_ref, v_ref,
                        q_scr, k_scr, v_scr, att_scr, *, n_seq):
    x = x_ref[...]
    m = mod_ref[...]
    sh = m[:, 0:D_MODEL]
    sc = m[:, D_MODEL:2 * D_MODEL]
    gt = m[:, 2 * D_MODEL:3 * D_MODEL]
    h = (_rms(x, g_ref[0:1, :]) * (1.0 + sc) + sh).astype(BF16)
    qkv = _dot(h, wqkv_ref[...])
    k = qkv[:, D_MODEL:2 * D_MODEL]
    v = qkv[:, 2 * D_MODEL:3 * D_MODEL]
    for b in range(n_seq):
        rows = slice(SEQ * b, SEQ * (b + 1))
        k_ref[b] = k[rows].reshape(SEQ, N_HEADS, HEAD_DIM)
        v_ref[b] = v[rows].reshape(SEQ, N_HEADS, HEAD_DIM)
    q_scr[...] = (qkv[:, 0:D_MODEL] * Q_SCALE).astype(BF16)
    k_scr[...] = k.astype(BF16)
    v_scr[...] = v.astype(BF16)

    lo = lax.broadcasted_iota(jnp.int32, (SEQ, PAIR_W), 1) < HEAD_DIM
    for b in range(n_seq):
        rows = slice(SEQ * b, SEQ * (b + 1))
        for j in range(N_PAIRS):
            lanes = slice(PAIR_W * j, PAIR_W * (j + 1))
            qs = _split_pair(q_scr[rows, lanes], lo)
            s = _dot_nt(qs, k_scr[rows, lanes])
            p = jnp.exp(s - jnp.max(s, axis=-1, keepdims=True))
            den = jnp.sum(p, axis=-1, keepdims=True)
            o = _dot(p.astype(BF16), v_scr[rows, lanes]) / den
            att_scr[rows, lanes] = jnp.where(lo, o[:SEQ], o[SEQ:]).astype(BF16)

    mix = _dot(att_scr[...], wout_ref[...])
    o_ref[...] = x + gt * _rms(mix, g_ref[1:2, :])


def _prompt_attn(x, mod, row, g, wqkv, wout, n_seq=2):
    t = x.shape[0]
    tm = n_seq * SEQ
    tok = pl.BlockSpec((tm, D_MODEL), lambda i: (i, 0))
    out = jax.ShapeDtypeStruct((t, D_MODEL), F32)
    cache = pl.BlockSpec((n_seq, None, SEQ, N_HEADS, HEAD_DIM), lambda i: (i, 0, 0, 0, 0))
    cache_out = jax.ShapeDtypeStruct((t // SEQ, 1, SEQ, N_HEADS, HEAD_DIM), F32)
    return pl.pallas_call(
        functools.partial(_prompt_attn_kernel, n_seq=n_seq),
        grid=(t // tm,),
        in_specs=[
            tok,
            _mod_spec(lambda i: row),
            _const_spec((4, D_MODEL)),
            _const_spec((D_MODEL, 3 * D_MODEL)),
            _const_spec((D_MODEL, D_MODEL)),
        ],
        out_specs=[tok, cache, cache],
        out_shape=[out, cache_out, cache_out],
        scratch_shapes=[pltpu.VMEM((tm, D_MODEL), BF16)] * 4,
        compiler_params=_params(1),
        name="prompt_attn",
    )(x, mod, g, wqkv, wout)


def _premix_qkv_kernel(x_ref, mod_ref, g_ref, wqkv_ref, q_ref, k_ref, v_ref):
    x = x_ref[...]
    m = mod_ref[...]
    sh = m[:, 0:D_MODEL]
    sc = m[:, D_MODEL:2 * D_MODEL]
    h = (_rms(x, g_ref[0:1, :]) * (1.0 + sc) + sh).astype(BF16)
    qkv = _dot(h, wqkv_ref[...])
    q_ref[...] = (qkv[:, 0:D_MODEL] * Q_SCALE).astype(BF16)
    k_ref[...] = qkv[:, D_MODEL:2 * D_MODEL].astype(BF16)
    _store_values_with_ones(v_ref, qkv[:, 2 * D_MODEL:3 * D_MODEL].astype(BF16))


def _store_values_with_ones(v_ref, v):
    ones = jnp.ones((v.shape[0], PAIR_W), BF16)
    for j in range(N_PAIRS):
        v_ref[:, 2 * PAIR_W * j:2 * PAIR_W * j + PAIR_W] = v[:, PAIR_W * j:PAIR_W * (j + 1)]
        v_ref[:, 2 * PAIR_W * j + PAIR_W:2 * PAIR_W * (j + 1)] = ones


def _premix_qkv(x, mod, row_of_step, g, wqkv, tm=512):
    t = x.shape[0]
    tok = pl.BlockSpec((tm, D_MODEL), lambda i: (i, 0))
    tok2 = pl.BlockSpec((tm, 2 * D_MODEL), lambda i: (i, 0))
    out = jax.ShapeDtypeStruct((t, D_MODEL), BF16)
    out2 = jax.ShapeDtypeStruct((t, 2 * D_MODEL), BF16)
    return pl.pallas_call(
        _premix_qkv_kernel,
        grid=(t // tm,),
        in_specs=[tok, _mod_spec(row_of_step), _const_spec((4, D_MODEL)),
                  _const_spec((D_MODEL, 3 * D_MODEL))],
        out_specs=[tok, tok, tok2],
        out_shape=[out, out, out2],
        compiler_params=_params(1),
        name="premix_qkv",
    )(x, mod, g, wqkv)


def _ctx_prep_kernel(ck_ref, cv_ref, k_ref, v_ref):
    k_ref[...] = ck_ref[...].reshape(PAST_LEN, D_MODEL).astype(BF16)
    _store_values_with_ones(v_ref, cv_ref[...].reshape(PAST_LEN, D_MODEL).astype(BF16))


def _ctx_prep(cache_k, cache_v, layer_j):
    cache = pl.BlockSpec((None, None, PAST_LEN, N_HEADS, HEAD_DIM),
                         lambda b: (b, layer_j, 0, 0, 0))
    return pl.pallas_call(
        _ctx_prep_kernel,
        grid=(DEC_BATCH,),
        in_specs=[cache, cache],
        out_specs=[pl.BlockSpec((PAST_LEN, D_MODEL), lambda b: (b, 0)),
                   pl.BlockSpec((PAST_LEN, 2 * D_MODEL), lambda b: (b, 0))],
        out_shape=[jax.ShapeDtypeStruct((DEC_BATCH * PAST_LEN, D_MODEL), BF16),
                   jax.ShapeDtypeStruct((DEC_BATCH * PAST_LEN, 2 * D_MODEL), BF16)],
        compiler_params=_params(1),
        name="ctx_prep",
    )(cache_k, cache_v)


def _bias_table_kernel(l_ref, o_ref):
    lane = lax.broadcasted_iota(jnp.int32, (GRID_W, PAIR_W), 1)
    qcol = lax.broadcasted_iota(jnp.int32, (GRID_W, PAIR_W), 0)
    kcol = lane & (GRID_W - 1)
    start = jnp.clip(qcol - WIN_COLS // 2, 0, GRID_W - WIN_COLS)
    in_window = (kcol >= start) & (kcol < start + WIN_COLS)
    lo = lane < GRID_W

    def toeplitz(d, shift):
        row = jnp.broadcast_to(l_ref[d:d + 1, :], (GRID_W, PAIR_W))
        return pltpu.roll(row, shift, 1, stride=1, stride_axis=0)

    for d in range(N_DR_PAIRS):
        both = jnp.where(lo, toeplitz(d, GRID_W + 1), toeplitz(d + 1, 1))
        o_ref[d] = jnp.where(in_window, both, NEG_INF)


def _bias_table(rpb):
    padded = jnp.pad(rpb, ((0, 0), (0, 0), (48, 49)), mode="edge")
    return pl.pallas_call(
        _bias_table_kernel,
        grid=(N_HEADS,),
        in_specs=[pl.BlockSpec((None, N_DR, PAIR_W), lambda h: (h, 0, 0))],
        out_specs=pl.BlockSpec((None, N_DR_PAIRS, GRID_W, PAIR_W), lambda h: (h, 0, 0, 0)),
        out_shape=jax.ShapeDtypeStruct((N_HEADS, N_DR_PAIRS, GRID_W, PAIR_W), F32),
        compiler_params=_params(1),
        name="bias_table",
    )(padded)


def _na_attn_kernel(q_ref, k_ref, v_ref, ck_ref, cv_ref, bias_ref, x_ref, mod_ref, g_ref,
                    wout_ref, o_ref, s_scr, m_scr, p_scr, att_scr, *, rows_per_step):
    blk = pl.program_id(1)
    lo = lax.broadcasted_iota(jnp.int32, (GRID_W, PAIR_W), 1) < HEAD_DIM

    def row_geometry(i):
        r = blk * rows_per_step + i
        rs = jnp.clip(r - WIN_ROWS // 2, 0, GRID_ROWS - WIN_ROWS)
        d0 = rs - r + (WIN_ROWS - 1)
        q0 = pl.multiple_of(i * GRID_W, GRID_W)
        k0 = pl.multiple_of(rs * GRID_W, GRID_W)
        return d0, q0, k0

    def scores(i, slot):
        d0, q0, k0 = row_geometry(i)
        for j in range(N_PAIRS):
            lanes = slice(PAIR_W * j, PAIR_W * (j + 1))
            qs = _split_pair(q_ref[pl.ds(q0, GRID_W), lanes], lo)
            bias = jnp.concatenate(
                [jnp.concatenate(
                    [bias_ref[(2 * j) * N_DR_PAIRS + d0 + 2 * jj],
                     bias_ref[(2 * j + 1) * N_DR_PAIRS + d0 + 2 * jj]], axis=0)
                 for jj in range(WIN_ROWS // 2)], axis=1)
            s_loc = _dot_nt(qs, k_ref[pl.ds(k0, N_LOCAL), lanes]) + bias
            s_ctx = _dot_nt(qs, ck_ref[:, lanes])
            mx = jnp.maximum(jnp.max(s_loc, axis=-1, keepdims=True),
                             jnp.max(s_ctx, axis=-1, keepdims=True))
            s_scr[slot, j, :, 0:N_LOCAL] = s_loc
            s_scr[slot, j, :, N_LOCAL:N_KEYS] = s_ctx
            m_scr[slot, j] = jnp.broadcast_to(mx, (2 * GRID_W, PAIR_W))

    def probs(slot):
        for j in range(N_PAIRS):
            mx = m_scr[slot, j][:, 0:1]
            p_scr[slot, j] = jnp.exp(s_scr[slot, j] - mx).astype(BF16)

    def values(i, slot):
        _, q0, k0 = row_geometry(i)
        for j in range(N_PAIRS):
            lanes2 = slice(2 * PAIR_W * j, 2 * PAIR_W * (j + 1))
            p = p_scr[slot, j]
            o2 = (_dot(p[:, 0:N_LOCAL], v_ref[pl.ds(k0, N_LOCAL), lanes2])
                  + _dot(p[:, N_LOCAL:N_KEYS], cv_ref[:, lanes2]))
            o = o2[:, 0:PAIR_W] / o2[:, PAIR_W:2 * PAIR_W]
            att_scr[pl.ds(q0, GRID_W), PAIR_W * j:PAIR_W * (j + 1)] = (
                jnp.where(lo, o[:GRID_W], o[GRID_W:]).astype(BF16))

    scores(0, 0)
    probs(0)
    scores(1, 1)

    def two_rows(t, carry):
        i = 2 * t
        values(i - 2, 0)
        probs(1)
        scores(i, 0)
        values(i - 1, 1)
        probs(0)
        scores(i + 1, 1)
        return carry

    lax.fori_loop(1, rows_per_step // 2, two_rows, 0)
    values(rows_per_step - 2, 0)
    probs(1)
    values(rows_per_step - 1, 1)

    gt = mod_ref[...][:, 2 * D_MODEL:3 * D_MODEL]
    mix = _dot(att_scr[...], wout_ref[...])
    o_ref[...] = x_ref[...] + gt * _rms(mix, g_ref[1:2, :])


def _na_attn(q, k, v, ck, cv, bias, x, mod, row_of_batch, g, wout, rows_per_step=8):
    tm = rows_per_step * GRID_W
    n_t = DEC_SEQ // tm
    tok = pl.BlockSpec((tm, D_MODEL), lambda b, t: (b * n_t + t, 0))

    def per_batch(rows, width):
        return pl.BlockSpec((rows, width), lambda b, t: (b, 0), pipeline_mode=pl.Buffered(1))

    return pl.pallas_call(
        functools.partial(_na_attn_kernel, rows_per_step=rows_per_step),
        grid=(DEC_BATCH, n_t),
        in_specs=[
            tok,
            per_batch(DEC_SEQ, D_MODEL), per_batch(DEC_SEQ, 2 * D_MODEL),
            per_batch(PAST_LEN, D_MODEL), per_batch(PAST_LEN, 2 * D_MODEL),
            _const_spec((N_HEADS * N_DR_PAIRS, GRID_W, PAIR_W)),
            tok,
            _mod_spec(lambda b, t: row_of_batch(b)),
            _const_spec((4, D_MODEL)),
            _const_spec((D_MODEL, D_MODEL)),
        ],
        out_specs=tok,
        out_shape=jax.ShapeDtypeStruct((DEC_BATCH * DEC_SEQ, D_MODEL), F32),
        scratch_shapes=[
            pltpu.VMEM((2, N_PAIRS, 2 * GRID_W, N_KEYS), F32),
            pltpu.VMEM((2, N_PAIRS, 2 * GRID_W, PAIR_W), F32),
            pltpu.VMEM((2, N_PAIRS, 2 * GRID_W, N_KEYS), BF16),
            pltpu.VMEM((tm, D_MODEL), BF16),
        ],
        compiler_params=_params(2),
        name="na_attn",
    )(q, k, v, ck, cv, bias, x, mod, g, wout)


def kernel(x_prompt, x_sample, c, cache_k, cache_v, c_ctx, ada_w, ada_b, norm_g, fourier_w_out,
           na_w_qkv, na_rpb, na_w_out, ffn_w_gate, ffn_w_up, ffn_w_down):
    n_p = BATCH * SEQ
    n_s = DEC_BATCH * DEC_SEQ
    xp = x_prompt.reshape(n_p, D_MODEL)
    xs = x_sample.reshape(n_s, D_MODEL)

    cond = jnp.concatenate(
        [c_ctx[None, :], c, jnp.zeros((COND_ROWS - 1 - DEC_BATCH, D_MODEL), F32)], axis=0)
    mod = _modulation(cond, ada_w, ada_b).reshape(DEPTH * COND_ROWS, 1, 6 * D_MODEL)

    cos_g, sin_g = _dft_cos_sin(FOURIER_GROUP)
    cs_chan = jnp.asarray(np.concatenate([cos_g, sin_g], axis=1), F32).astype(BF16)
    cos_p, sin_p = (jnp.asarray(a, F32).astype(BF16) for a in _dft_cos_sin(SEQ))
    cos_s, sin_s = (jnp.asarray(a, F32).astype(BF16) for a in _dft_cos_sin(DEC_SEQ))

    tm = 512
    new_k = new_v = None
    for layer in range(DEPTH):
        g = norm_g[layer]
        base = layer * COND_ROWS
        prompt_row = lambda *_, base=base: base
        sample_row_of_batch = lambda b, base=base: base + 1 + b
        sample_row_of_tile = lambda i, base=base: base + 1 + i // (DEC_SEQ // tm)

        if layer % 2 == 0:
            w_out = fourier_w_out[layer // 2].astype(BF16)
            xc, xsn = _premix_cdft(xp, mod, prompt_row, g, cs_chan, tm)
            xp = _seqdft_out(cos_p, sin_p, xc, xsn, xp, mod, prompt_row, g, w_out, BATCH, SEQ,
                             n_sub=4, tr=SEQ)
            xc, xsn = _premix_cdft(xs, mod, sample_row_of_tile, g, cs_chan, tm)
            xs = _seqdft_out(cos_s, sin_s, xc, xsn, xs, mod, sample_row_of_batch, g, w_out,
                             DEC_BATCH, DEC_SEQ, n_sub=1, tr=512)
        else:
            j = layer // 2
            w_qkv = na_w_qkv[j].astype(BF16)
            w_out = na_w_out[j].astype(BF16)
            xp, new_k, new_v = _prompt_attn(xp, mod, base, g, w_qkv, w_out)
            q, k, v = _premix_qkv(xs, mod, sample_row_of_tile, g, w_qkv, tm)
            ck, cv = _ctx_prep(cache_k, cache_v, j)
            bias = _bias_table(na_rpb[j]).reshape(N_HEADS * N_DR_PAIRS, GRID_W, PAIR_W)
            xs = _na_attn(q, k, v, ck, cv, bias, xs, mod, sample_row_of_batch, g, w_out)

        wg = ffn_w_gate[layer].astype(BF16)
        wu = ffn_w_up[layer].astype(BF16)
        wd = ffn_w_down[layer].astype(BF16)
        xp = _ffn(xp, mod, prompt_row, g, wg, wu, wd, tm)
        xs = _ffn(xs, mod, sample_row_of_tile, g, wg, wu, wd, tm)

    return (xp.reshape(BATCH, SEQ, D_MODEL), xs.reshape(DEC_BATCH, DEC_SEQ, D_MODEL), new_k, new_v)
```

```python
import functools

import numpy as np
import jax
import jax.numpy as jnp
from jax import lax
from jax.experimental import pallas as pl
from jax.experimental.pallas import tpu as pltpu

D_MODEL = 1024
BATCH = 32
SEQ = 256
DEPTH = 2
DEC_BATCH = 2
DEC_SEQ = 2048
PAST_LEN = 512
GRID_W = 64
GRID_ROWS = DEC_SEQ // GRID_W
N_HEADS = 16
HEAD_DIM = D_MODEL // N_HEADS
N_FOURIER_GROUPS = 4
FOURIER_GROUP = D_MODEL // N_FOURIER_GROUPS
WIN_ROWS = 8
WIN_COLS = 16
D_FF = 2816
EPS = 1e-6
NEG_INF = -1e30

N_PAIRS = N_HEADS // 2
PAIR_W = 2 * HEAD_DIM
N_DR = 2 * WIN_ROWS - 1
N_DR_PAIRS = N_DR - 1
N_LOCAL = WIN_ROWS * GRID_W
N_KEYS = N_LOCAL + PAST_LEN
COND_ROWS = 8
Q_SCALE = HEAD_DIM ** -0.5

VMEM_LIMIT = 56 * 1024 * 1024

F32 = jnp.float32
BF16 = jnp.bfloat16


def _dot(a, b):
    return jnp.dot(a, b, preferred_element_type=F32)


def _dot_nt(a, b):
    return lax.dot_general(a, b, (((1,), (1,)), ((), ())), preferred_element_type=F32)


def _rms(x, g):
    ms = jnp.mean(x * x, axis=-1, keepdims=True)
    return x * lax.rsqrt(ms + EPS) * g


def _silu(x):
    return x / (1.0 + jnp.exp(-x))


def _const_spec(shape):
    return pl.BlockSpec(shape, lambda *_: (0,) * len(shape), pipeline_mode=pl.Buffered(1))


def _mod_spec(row_of_step):
    return pl.BlockSpec((None, 1, 6 * D_MODEL), lambda *idx: (row_of_step(*idx), 0, 0))


def _params(n_axes):
    return pltpu.CompilerParams(
        dimension_semantics=("arbitrary",) * n_axes, vmem_limit_bytes=VMEM_LIMIT)


def _mod_kernel(cond_ref, w_ref, b_ref, o_ref):
    a = _silu(cond_ref[...])
    w = w_ref[...]
    a_hi = a.astype(BF16)
    a_lo = (a - a_hi.astype(F32)).astype(BF16)
    w_hi = w.astype(BF16)
    w_lo = (w - w_hi.astype(F32)).astype(BF16)
    o_ref[...] = _dot(a_hi, w_hi) + _dot(a_lo, w_hi) + _dot(a_hi, w_lo) + b_ref[...]


def _modulation(cond, ada_w, ada_b):
    tn = 1536
    n_out = 6 * D_MODEL
    return pl.pallas_call(
        _mod_kernel,
        grid=(DEPTH, n_out // tn),
        in_specs=[
            pl.BlockSpec((COND_ROWS, D_MODEL), lambda l, n: (0, 0)),
            pl.BlockSpec((None, D_MODEL, tn), lambda l, n: (l, 0, n)),
            pl.BlockSpec((None, 1, tn), lambda l, n: (l, 0, n)),
        ],
        out_specs=pl.BlockSpec((None, COND_ROWS, tn), lambda l, n: (l, 0, n)),
        out_shape=jax.ShapeDtypeStruct((DEPTH, COND_ROWS, n_out), F32),
        compiler_params=_params(2),
        name="modulation",
    )(cond, ada_w, ada_b.reshape(DEPTH, 1, n_out))


def _ffn_kernel(x_ref, mod_ref, g_ref, wg_ref, wu_ref, wd_ref, o_ref):
    x = x_ref[...]
    m = mod_ref[...]
    sh = m[:, 3 * D_MODEL:4 * D_MODEL]
    sc = m[:, 4 * D_MODEL:5 * D_MODEL]
    gt = m[:, 5 * D_MODEL:6 * D_MODEL]
    f = (_rms(x, g_ref[2:3, :]) * (1.0 + sc) + sh).astype(BF16)
    gate = _dot(f, wg_ref[...])
    up = _dot(f, wu_ref[...])
    act = (_silu(gate) * up).astype(BF16)
    y = _dot(act, wd_ref[...])
    o_ref[...] = x + gt * _rms(y, g_ref[3:4, :])


def _layer_spec(shape, layer):
    return pl.BlockSpec((None,) + shape, lambda *_: (layer,) + (0,) * len(shape),
                        pipeline_mode=pl.Buffered(1))


def _ffn(x, mod, row_of_step, g, wg, wu, wd, layer, tm=512):
    t = x.shape[0]
    return pl.pallas_call(
        _ffn_kernel,
        grid=(t // tm,),
        in_specs=[
            pl.BlockSpec((tm, D_MODEL), lambda i: (i, 0)),
            _mod_spec(row_of_step),
            _const_spec((4, D_MODEL)),
            _layer_spec((D_MODEL, D_FF), layer),
            _layer_spec((D_MODEL, D_FF), layer),
            _layer_spec((D_FF, D_MODEL), layer),
        ],
        out_specs=pl.BlockSpec((tm, D_MODEL), lambda i: (i, 0)),
        out_shape=jax.ShapeDtypeStruct((t, D_MODEL), F32),
        compiler_params=_params(1),
        name="ffn",
    )(x, mod, g, wg, wu, wd)


def _dft_cos_sin(n):
    j = np.arange(n)
    ang = 2.0 * np.pi * ((j[:, None] * j[None, :]) % n) / n
    return np.cos(ang), np.sin(ang)


def _premix_cdft_kernel(x_ref, mod_ref, g_ref, cs_ref, xc_ref, xs_ref):
    x = x_ref[...]
    m = mod_ref[...]
    sh = m[:, 0:D_MODEL]
    sc = m[:, D_MODEL:2 * D_MODEL]
    h = (_rms(x, g_ref[0:1, :]) * (1.0 + sc) + sh).astype(BF16)
    cs = cs_ref[...]
    for j in range(N_FOURIER_GROUPS):
        lanes = slice(FOURIER_GROUP * j, FOURIER_GROUP * (j + 1))
        r = _dot(h[:, lanes], cs)
        xc_ref[:, lanes] = r[:, :FOURIER_GROUP].astype(BF16)
        xs_ref[:, lanes] = r[:, FOURIER_GROUP:].astype(BF16)


def _premix_cdft(x, mod, row_of_step, g, cs, tm=512):
    t = x.shape[0]
    out = jax.ShapeDtypeStruct((t, D_MODEL), BF16)
    return pl.pallas_call(
        _premix_cdft_kernel,
        grid=(t // tm,),
        in_specs=[
            pl.BlockSpec((tm, D_MODEL), lambda i: (i, 0)),
            _mod_spec(row_of_step),
            _const_spec((4, D_MODEL)),
            _const_spec((FOURIER_GROUP, 2 * FOURIER_GROUP)),
        ],
        out_specs=[pl.BlockSpec((tm, D_MODEL), lambda i: (i, 0))] * 2,
        out_shape=[out, out],
        compiler_params=_params(1),
        name="premix_cdft",
    )(x, mod, g, cs)


def _seqdft_out_kernel(c_ref, s_ref, xc_ref, xs_ref, x_ref, mod_ref, g_ref, w_ref, o_ref, f_scr,
                       *, n_sub, seq, tr):
    for b in range(n_sub):
        src = slice(seq * b, seq * (b + 1))
        f = _dot(c_ref[...], xc_ref[src, :]) - _dot(s_ref[...], xs_ref[src, :])
        f_scr[tr * b:tr * (b + 1), :] = f.astype(BF16)
    mix = _dot(f_scr[...], w_ref[...])
    gt = mod_ref[...][:, 2 * D_MODEL:3 * D_MODEL]
    o_ref[...] = x_ref[...] + gt * _rms(mix, g_ref[1:2, :])


def _seqdft_out(cmat, smat, xc, xs, x, mod, row_of_group, g, w, n_seqs, seq, n_sub, tr):
    n_t = seq // tr
    tok = pl.BlockSpec((n_sub * tr, D_MODEL), lambda b, t: (b * n_t + t, 0))
    src = pl.BlockSpec((n_sub * seq, D_MODEL), lambda b, t: (b, 0))
    return pl.pallas_call(
        functools.partial(_seqdft_out_kernel, n_sub=n_sub, seq=seq, tr=tr),
        grid=(n_seqs // n_sub, n_t),
        in_specs=[
            pl.BlockSpec((tr, seq), lambda b, t: (t, 0)),
            pl.BlockSpec((tr, seq), lambda b, t: (t, 0)),
            src, src, tok,
            _mod_spec(lambda b, t: row_of_group(b)),
            _const_spec((4, D_MODEL)),
            _const_spec((D_MODEL, D_MODEL)),
        ],
        out_specs=tok,
        out_shape=jax.ShapeDtypeStruct((n_seqs * seq, D_MODEL), F32),
        scratch_shapes=[pltpu.VMEM((n_sub * tr, D_MODEL), BF16)],
        compiler_params=_params(2),
        name="seqdft_out",
    )(cmat, smat, xc, xs, x, mod, g, w)


def _split_pair(q, lo):
    zero = jnp.zeros_like(q)
    return jnp.concatenate([jnp.where(lo, q, zero), jnp.where(lo, zero, q)], axis=0)


def _prompt_attn_kernel(x_ref, mod_ref, g_ref, wq_ref, wkt_ref, wvt_ref, wout_ref,
                        o_ref, kt_ref, vt_ref, h_scr, q_scr, kt_scr, vt_scr, att_scr, *, n_seq):
    x = x_ref[...]
    m = mod_ref[...]
    sh = m[:, 0:D_MODEL]
    sc = m[:, D_MODEL:2 * D_MODEL]
    gt = m[:, 2 * D_MODEL:3 * D_MODEL]
    h_scr[...] = (_rms(x, g_ref[0:1, :]) * (1.0 + sc) + sh).astype(BF16)
    q_scr[...] = (_dot(h_scr[...], wq_ref[...]) * Q_SCALE).astype(BF16)

    lo = lax.broadcasted_iota(jnp.int32, (SEQ, PAIR_W), 1) < HEAD_DIM
    ones = jnp.ones((PAIR_W, SEQ), BF16)
    for b in range(n_seq):
        rows = slice(SEQ * b, SEQ * (b + 1))
        kt = _dot_nt(wkt_ref[...], h_scr[rows, :])
        vt = _dot_nt(wvt_ref[...], h_scr[rows, :])
        kt_ref[b] = kt.reshape(N_HEADS, HEAD_DIM, SEQ)
        vt_ref[b] = vt.reshape(N_HEADS, HEAD_DIM, SEQ)
        kt_scr[b] = kt.astype(BF16)
        vt_scr[b] = vt.astype(BF16)
        for j in range(N_PAIRS):
            lanes = slice(PAIR_W * j, PAIR_W * (j + 1))
            qs = _split_pair(q_scr[rows, lanes], lo)
            s = _dot(qs, kt_scr[b, lanes, :])
            p = jnp.exp(s - jnp.max(s, axis=-1, keepdims=True)).astype(BF16)
            o2 = _dot_nt(p, jnp.concatenate([vt_scr[b, lanes, :], ones], axis=0))
            o = o2[:, 0:PAIR_W] / o2[:, PAIR_W:2 * PAIR_W]
            att_scr[rows, lanes] = jnp.where(lo, o[:SEQ], o[SEQ:]).astype(BF16)

    mix = _dot(att_scr[...], wout_ref[...])
    o_ref[...] = x + gt * _rms(mix, g_ref[1:2, :])


def _prompt_attn(x, mod, row, g, wq, wkt, wvt, wout, n_seq=2):
    t = x.shape[0]
    tm = n_seq * SEQ
    tok = pl.BlockSpec((tm, D_MODEL), lambda i: (i, 0))
    out = jax.ShapeDtypeStruct((t, D_MODEL), F32)
    cache = pl.BlockSpec((n_seq, None, N_HEADS, HEAD_DIM, SEQ), lambda i: (i, 0, 0, 0, 0))
    cache_out = jax.ShapeDtypeStruct((t // SEQ, 1, N_HEADS, HEAD_DIM, SEQ), F32)
    square = _const_spec((D_MODEL, D_MODEL))
    return pl.pallas_call(
        functools.partial(_prompt_attn_kernel, n_seq=n_seq),
        grid=(t // tm,),
        in_specs=[tok, _mod_spec(lambda i: row), _const_spec((4, D_MODEL)),
                  square, square, square, square],
        out_specs=[tok, cache, cache],
        out_shape=[out, cache_out, cache_out],
        scratch_shapes=[pltpu.VMEM((tm, D_MODEL), BF16), pltpu.VMEM((tm, D_MODEL), BF16),
                        pltpu.VMEM((n_seq, D_MODEL, SEQ), BF16),
                        pltpu.VMEM((n_seq, D_MODEL, SEQ), BF16),
                        pltpu.VMEM((tm, D_MODEL), BF16)],
        compiler_params=_params(1),
        name="prompt_attn",
    )(x, mod, g, wq, wkt, wvt, wout)


def _premix_qkv_kernel(x_ref, mod_ref, g_ref, wqkv_ref, q_ref, k_ref, v_ref):
    x = x_ref[...]
    m = mod_ref[...]
    sh = m[:, 0:D_MODEL]
    sc = m[:, D_MODEL:2 * D_MODEL]
    h = (_rms(x, g_ref[0:1, :]) * (1.0 + sc) + sh).astype(BF16)
    qkv = _dot(h, wqkv_ref[...])
    q_ref[...] = (qkv[:, 0:D_MODEL] * Q_SCALE).astype(BF16)
    k_ref[...] = qkv[:, D_MODEL:2 * D_MODEL].astype(BF16)
    _store_values_with_ones(v_ref, qkv[:, 2 * D_MODEL:3 * D_MODEL].astype(BF16))


def _store_values_with_ones(v_ref, v):
    ones = jnp.ones((v.shape[0], PAIR_W), BF16)
    for j in range(N_PAIRS):
        v_ref[:, 2 * PAIR_W * j:2 * PAIR_W * j + PAIR_W] = v[:, PAIR_W * j:PAIR_W * (j + 1)]
        v_ref[:, 2 * PAIR_W * j + PAIR_W:2 * PAIR_W * (j + 1)] = ones


def _premix_qkv(x, mod, row_of_step, g, wqkv, tm=512):
    t = x.shape[0]
    tok = pl.BlockSpec((tm, D_MODEL), lambda i: (i, 0))
    tok2 = pl.BlockSpec((tm, 2 * D_MODEL), lambda i: (i, 0))
    out = jax.ShapeDtypeStruct((t, D_MODEL), BF16)
    out2 = jax.ShapeDtypeStruct((t, 2 * D_MODEL), BF16)
    return pl.pallas_call(
        _premix_qkv_kernel,
        grid=(t // tm,),
        in_specs=[tok, _mod_spec(row_of_step), _const_spec((4, D_MODEL)),
                  _const_spec((D_MODEL, 3 * D_MODEL))],
        out_specs=[tok, tok, tok2],
        out_shape=[out, out, out2],
        compiler_params=_params(1),
        name="premix_qkv",
    )(x, mod, g, wqkv)


def _ctx_prep_kernel(ckt_ref, cvt_ref, k_ref, v_ref):
    k_ref[...] = ckt_ref[...].astype(BF16)
    vt = cvt_ref[...].astype(BF16)
    ones = jnp.ones((PAIR_W, PAST_LEN), BF16)
    for j in range(N_PAIRS):
        v_ref[2 * PAIR_W * j:2 * PAIR_W * j + PAIR_W, :] = vt[PAIR_W * j:PAIR_W * (j + 1), :]
        v_ref[2 * PAIR_W * j + PAIR_W:2 * PAIR_W * (j + 1), :] = ones


def _ctx_prep(cache_kt, cache_vt, layer_j):
    cache = pl.BlockSpec((None, None, D_MODEL, PAST_LEN), lambda b: (b, layer_j, 0, 0))
    return pl.pallas_call(
        _ctx_prep_kernel,
        grid=(DEC_BATCH,),
        in_specs=[cache, cache],
        out_specs=[pl.BlockSpec((D_MODEL, PAST_LEN), lambda b: (b, 0)),
                   pl.BlockSpec((2 * D_MODEL, PAST_LEN), lambda b: (b, 0))],
        out_shape=[jax.ShapeDtypeStruct((DEC_BATCH * D_MODEL, PAST_LEN), BF16),
                   jax.ShapeDtypeStruct((DEC_BATCH * 2 * D_MODEL, PAST_LEN), BF16)],
        compiler_params=_params(1),
        name="ctx_prep",
    )(cache_kt, cache_vt)


def _bias_table_kernel(l_ref, o_ref):
    lane = lax.broadcasted_iota(jnp.int32, (GRID_W, PAIR_W), 1)
    qcol = lax.broadcasted_iota(jnp.int32, (GRID_W, PAIR_W), 0)
    kcol = lane & (GRID_W - 1)
    start = jnp.clip(qcol - WIN_COLS // 2, 0, GRID_W - WIN_COLS)
    in_window = (kcol >= start) & (kcol < start + WIN_COLS)
    lo = lane < GRID_W

    def toeplitz(d, shift):
        row = jnp.broadcast_to(l_ref[d:d + 1, :], (GRID_W, PAIR_W))
        return pltpu.roll(row, shift, 1, stride=1, stride_axis=0)

    for d in range(N_DR_PAIRS):
        both = jnp.where(lo, toeplitz(d, GRID_W + 1), toeplitz(d + 1, 1))
        o_ref[d] = jnp.where(in_window, both, NEG_INF)


def _bias_table(rpb):
    padded = jnp.pad(rpb, ((0, 0), (0, 0), (48, 49)), mode="edge")
    return pl.pallas_call(
        _bias_table_kernel,
        grid=(N_HEADS,),
        in_specs=[pl.BlockSpec((None, N_DR, PAIR_W), lambda h: (h, 0, 0))],
        out_specs=pl.BlockSpec((None, N_DR_PAIRS, GRID_W, PAIR_W), lambda h: (h, 0, 0, 0)),
        out_shape=jax.ShapeDtypeStruct((N_HEADS, N_DR_PAIRS, GRID_W, PAIR_W), F32),
        compiler_params=_params(1),
        name="bias_table",
    )(padded)


def _na_attn_kernel(q_ref, k_ref, v_ref, ckt_ref, cvt_ref, bias_ref, x_ref, mod_ref, g_ref,
                    wout_ref, o_ref, s_scr, m_scr, p_scr, att_scr, *, rows_per_step):
    blk = pl.program_id(1)
    lo = lax.broadcasted_iota(jnp.int32, (GRID_W, PAIR_W), 1) < HEAD_DIM

    def row_geometry(i):
        r = blk * rows_per_step + i
        rs = jnp.clip(r - WIN_ROWS // 2, 0, GRID_ROWS - WIN_ROWS)
        d0 = rs - r + (WIN_ROWS - 1)
        q0 = pl.multiple_of(i * GRID_W, GRID_W)
        k0 = pl.multiple_of(rs * GRID_W, GRID_W)
        return d0, q0, k0

    def scores(i, slot):
        d0, q0, k0 = row_geometry(i)
        for j in range(N_PAIRS):
            lanes = slice(PAIR_W * j, PAIR_W * (j + 1))
            qs = _split_pair(q_ref[pl.ds(q0, GRID_W), lanes], lo)
            bias = jnp.concatenate(
                [jnp.concatenate(
                    [bias_ref[(2 * j) * N_DR_PAIRS + d0 + 2 * jj],
                     bias_ref[(2 * j + 1) * N_DR_PAIRS + d0 + 2 * jj]], axis=0)
                 for jj in range(WIN_ROWS // 2)], axis=1)
            s_loc = _dot_nt(qs, k_ref[pl.ds(k0, N_LOCAL), lanes]) + bias
            s_ctx = _dot(qs, ckt_ref[lanes, :])
            mx = jnp.maximum(jnp.max(s_loc, axis=-1, keepdims=True),
                             jnp.max(s_ctx, axis=-1, keepdims=True))
            s_scr[slot, j, :, 0:N_LOCAL] = s_loc
            s_scr[slot, j, :, N_LOCAL:N_KEYS] = s_ctx
            m_scr[slot, j] = jnp.broadcast_to(mx, (2 * GRID_W, PAIR_W))

    def probs(slot):
        for j in range(N_PAIRS):
            mx = m_scr[slot, j][:, 0:1]
            p_scr[slot, j] = jnp.exp(s_scr[slot, j] - mx).astype(BF16)

    def values(i, slot):
        _, q0, k0 = row_geometry(i)
        for j in range(N_PAIRS):
            lanes2 = slice(2 * PAIR_W * j, 2 * PAIR_W * (j + 1))
            p = p_scr[slot, j]
            o2 = (_dot(p[:, 0:N_LOCAL], v_ref[pl.ds(k0, N_LOCAL), lanes2])
                  + _dot_nt(p[:, N_LOCAL:N_KEYS], cvt_ref[lanes2, :]))
            o = o2[:, 0:PAIR_W] / o2[:, PAIR_W:2 * PAIR_W]
            att_scr[pl.ds(q0, GRID_W), PAIR_W * j:PAIR_W * (j + 1)] = (
                jnp.where(lo, o[:GRID_W], o[GRID_W:]).astype(BF16))

    scores(0, 0)
    probs(0)
    scores(1, 1)

    def two_rows(t, carry):
        i = 2 * t
        values(i - 2, 0)
        probs(1)
        scores(i, 0)
        values(i - 1, 1)
        probs(0)
        scores(i + 1, 1)
        return carry

    lax.fori_loop(1, rows_per_step // 2, two_rows, 0)
    values(rows_per_step - 2, 0)
    probs(1)
    values(rows_per_step - 1, 1)

    gt = mod_ref[...][:, 2 * D_MODEL:3 * D_MODEL]
    mix = _dot(att_scr[...], wout_ref[...])
    o_ref[...] = x_ref[...] + gt * _rms(mix, g_ref[1:2, :])


def _na_attn(q, k, v, ck, cv, bias, x, mod, row_of_batch, g, wout, rows_per_step=8):
    tm = rows_per_step * GRID_W
    n_t = DEC_SEQ // tm
    tok = pl.BlockSpec((tm, D_MODEL), lambda b, t: (b * n_t + t, 0))

    def per_batch(rows, width):
        return pl.BlockSpec((rows, width), lambda b, t: (b, 0), pipeline_mode=pl.Buffered(1))

    return pl.pallas_call(
        functools.partial(_na_attn_kernel, rows_per_step=rows_per_step),
        grid=(DEC_BATCH, n_t),
        in_specs=[
            tok,
            per_batch(DEC_SEQ, D_MODEL), per_batch(DEC_SEQ, 2 * D_MODEL),
            per_batch(D_MODEL, PAST_LEN), per_batch(2 * D_MODEL, PAST_LEN),
            _const_spec((N_HEADS * N_DR_PAIRS, GRID_W, PAIR_W)),
            tok,
            _mod_spec(lambda b, t: row_of_batch(b)),
            _const_spec((4, D_MODEL)),
            _const_spec((D_MODEL, D_MODEL)),
        ],
        out_specs=tok,
        out_shape=jax.ShapeDtypeStruct((DEC_BATCH * DEC_SEQ, D_MODEL), F32),
        scratch_shapes=[
            pltpu.VMEM((2, N_PAIRS, 2 * GRID_W, N_KEYS), F32),
            pltpu.VMEM((2, N_PAIRS, 2 * GRID_W, PAIR_W), F32),
            pltpu.VMEM((2, N_PAIRS, 2 * GRID_W, N_KEYS), BF16),
            pltpu.VMEM((tm, D_MODEL), BF16),
        ],
        compiler_params=_params(2),
        name="na_attn",
    )(q, k, v, ck, cv, bias, x, mod, g, wout)


def kernel(x_prompt, x_sample, c, cache_k, cache_v, c_ctx, ada_w, ada_b, norm_g, fourier_w_out,
           na_w_qkv, na_rpb, na_w_out, ffn_w_gate, ffn_w_up, ffn_w_down):
    n_p = BATCH * SEQ
    n_s = DEC_BATCH * DEC_SEQ
    xp = x_prompt.reshape(n_p, D_MODEL)
    xs = x_sample.reshape(n_s, D_MODEL)

    cond = jnp.concatenate(
        [c_ctx[None, :], c, jnp.zeros((COND_ROWS - 1 - DEC_BATCH, D_MODEL), F32)], axis=0)
    mod = _modulation(cond, ada_w, ada_b).reshape(DEPTH * COND_ROWS, 1, 6 * D_MODEL)

    cos_g, sin_g = _dft_cos_sin(FOURIER_GROUP)
    cs_chan = jnp.asarray(np.concatenate([cos_g, sin_g], axis=1), F32).astype(BF16)
    cos_p, sin_p = (jnp.asarray(a, F32).astype(BF16) for a in _dft_cos_sin(SEQ))
    cos_s, sin_s = (jnp.asarray(a, F32).astype(BF16) for a in _dft_cos_sin(DEC_SEQ))

    tm = 512
    wg = ffn_w_gate.astype(BF16)
    wu = ffn_w_up.astype(BF16)
    wd = ffn_w_down.astype(BF16)
    cache_kt = jnp.transpose(cache_k, (0, 1, 3, 4, 2)).reshape(DEC_BATCH, -1, D_MODEL, PAST_LEN)
    cache_vt = jnp.transpose(cache_v, (0, 1, 3, 4, 2)).reshape(DEC_BATCH, -1, D_MODEL, PAST_LEN)
    new_kt = new_vt = None
    for layer in range(DEPTH):
        g = norm_g[layer]
        base = layer * COND_ROWS
        prompt_row = lambda *_, base=base: base
        sample_row_of_batch = lambda b, base=base: base + 1 + b
        sample_row_of_tile = lambda i, base=base: base + 1 + i // (DEC_SEQ // tm)

        if layer % 2 == 0:
            w_out = fourier_w_out[layer // 2].astype(BF16)
            xc, xsn = _premix_cdft(xp, mod, prompt_row, g, cs_chan, tm)
            xp = _seqdft_out(cos_p, sin_p, xc, xsn, xp, mod, prompt_row, g, w_out, BATCH, SEQ,
                             n_sub=4, tr=SEQ)
            xc, xsn = _premix_cdft(xs, mod, sample_row_of_tile, g, cs_chan, tm)
            xs = _seqdft_out(cos_s, sin_s, xc, xsn, xs, mod, sample_row_of_batch, g, w_out,
                             DEC_BATCH, DEC_SEQ, n_sub=1, tr=512)
        else:
            j = layer // 2
            w_qkv = na_w_qkv[j].astype(BF16)
            w_out = na_w_out[j].astype(BF16)
            w_q = w_qkv[:, 0:D_MODEL]
            w_kt = w_qkv[:, D_MODEL:2 * D_MODEL].T
            w_vt = w_qkv[:, 2 * D_MODEL:3 * D_MODEL].T
            xp, new_kt, new_vt = _prompt_attn(xp, mod, base, g, w_q, w_kt, w_vt, w_out)
            q, k, v = _premix_qkv(xs, mod, sample_row_of_tile, g, w_qkv, tm)
            ckt, cvt = _ctx_prep(cache_kt, cache_vt, j)
            bias = _bias_table(na_rpb[j]).reshape(N_HEADS * N_DR_PAIRS, GRID_W, PAIR_W)
            xs = _na_attn(q, k, v, ckt, cvt, bias, xs, mod, sample_row_of_batch, g, w_out)

        xp = _ffn(xp, mod, prompt_row, g, wg, wu, wd, layer, tm)
        xs = _ffn(xs, mod, sample_row_of_tile, g, wg, wu, wd, layer, tm)

    new_k = jnp.transpose(new_kt, (0, 1, 4, 2, 3))
    new_v = jnp.transpose(new_vt, (0, 1, 4, 2, 3))
    return (xp.reshape(BATCH, SEQ, D_MODEL), xs.reshape(DEC_BATCH, DEC_SEQ, D_MODEL), new_k, new_v)
```

```python
import functools

import numpy as np
import jax
import jax.numpy as jnp
from jax import lax
from jax.experimental import pallas as pl
from jax.experimental.pallas import tpu as pltpu

D_MODEL = 1024
BATCH = 32
SEQ = 256
DEPTH = 2
DEC_BATCH = 2
DEC_SEQ = 2048
PAST_LEN = 512
GRID_W = 64
GRID_ROWS = DEC_SEQ // GRID_W
N_HEADS = 16
HEAD_DIM = D_MODEL // N_HEADS
N_FOURIER_GROUPS = 4
FOURIER_GROUP = D_MODEL // N_FOURIER_GROUPS
WIN_ROWS = 8
WIN_COLS = 16
D_FF = 2816
EPS = 1e-6
NEG_INF = -1e30

N_PAIRS = N_HEADS // 2
PAIR_W = 2 * HEAD_DIM
N_DR = 2 * WIN_ROWS - 1
N_DR_PAIRS = N_DR - 1
N_LOCAL = WIN_ROWS * GRID_W
N_KEYS = N_LOCAL + PAST_LEN
COND_ROWS = 8
Q_SCALE = HEAD_DIM ** -0.5

VMEM_LIMIT = 56 * 1024 * 1024

F32 = jnp.float32
BF16 = jnp.bfloat16


def _dot(a, b):
    return jnp.dot(a, b, preferred_element_type=F32)


def _dot_nt(a, b):
    return lax.dot_general(a, b, (((1,), (1,)), ((), ())), preferred_element_type=F32)


def _rms(x, g):
    ms = jnp.mean(x * x, axis=-1, keepdims=True)
    return x * lax.rsqrt(ms + EPS) * g


def _silu(x):
    return x / (1.0 + jnp.exp(-x))


def _const_spec(shape):
    return pl.BlockSpec(shape, lambda *_: (0,) * len(shape), pipeline_mode=pl.Buffered(1))


def _mod_spec(row_of_step):
    return pl.BlockSpec((None, 1, 6 * D_MODEL), lambda *idx: (row_of_step(*idx), 0, 0))


def _params(n_axes):
    return pltpu.CompilerParams(
        dimension_semantics=("arbitrary",) * n_axes, vmem_limit_bytes=VMEM_LIMIT)


def _mod_kernel(cond_ref, w_ref, b_ref, o_ref):
    a = _silu(cond_ref[...])
    w = w_ref[...]
    a_hi = a.astype(BF16)
    a_lo = (a - a_hi.astype(F32)).astype(BF16)
    w_hi = w.astype(BF16)
    w_lo = (w - w_hi.astype(F32)).astype(BF16)
    o_ref[...] = _dot(a_hi, w_hi) + _dot(a_lo, w_hi) + _dot(a_hi, w_lo) + b_ref[...]


def _modulation(cond, ada_w, ada_b):
    tn = 1536
    n_out = 6 * D_MODEL
    return pl.pallas_call(
        _mod_kernel,
        grid=(DEPTH, n_out // tn),
        in_specs=[
            pl.BlockSpec((COND_ROWS, D_MODEL), lambda l, n: (0, 0)),
            pl.BlockSpec((None, D_MODEL, tn), lambda l, n: (l, 0, n)),
            pl.BlockSpec((None, 1, tn), lambda l, n: (l, 0, n)),
        ],
        out_specs=pl.BlockSpec((None, COND_ROWS, tn), lambda l, n: (l, 0, n)),
        out_shape=jax.ShapeDtypeStruct((DEPTH, COND_ROWS, n_out), F32),
        compiler_params=_params(2),
        name="modulation",
    )(cond, ada_w, ada_b.reshape(DEPTH, 1, n_out))


FFN_SUB_ROWS = 256


def _ffn_kernel(x_ref, mod_ref, g_ref, wg_ref, wu_ref, wd_ref, o_ref):
    m = mod_ref[...]
    sh = m[:, 3 * D_MODEL:4 * D_MODEL]
    sc = m[:, 4 * D_MODEL:5 * D_MODEL]
    gt = m[:, 5 * D_MODEL:6 * D_MODEL]
    for r0 in range(0, x_ref.shape[0], FFN_SUB_ROWS):
        rows = slice(r0, r0 + FFN_SUB_ROWS)
        x = x_ref[rows, :]
        f = (_rms(x, g_ref[2:3, :]) * (1.0 + sc) + sh).astype(BF16)
        gate = _dot(f, wg_ref[...])
        up = _dot(f, wu_ref[...])
        act = (_silu(gate) * up).astype(BF16)
        y = _dot(act, wd_ref[...])
        o_ref[rows, :] = x + gt * _rms(y, g_ref[3:4, :])


def _layer_spec(shape, layer):
    return pl.BlockSpec((None,) + shape, lambda *_: (layer,) + (0,) * len(shape),
                        pipeline_mode=pl.Buffered(1))


def _ffn(x, mod, row_of_step, g, wg, wu, wd, layer, tm=512):
    t = x.shape[0]
    return pl.pallas_call(
        _ffn_kernel,
        grid=(t // tm,),
        in_specs=[
            pl.BlockSpec((tm, D_MODEL), lambda i: (i, 0)),
            _mod_spec(row_of_step),
            _const_spec((4, D_MODEL)),
            _layer_spec((D_MODEL, D_FF), layer),
            _layer_spec((D_MODEL, D_FF), layer),
            _layer_spec((D_FF, D_MODEL), layer),
        ],
        out_specs=pl.BlockSpec((tm, D_MODEL), lambda i: (i, 0)),
        out_shape=jax.ShapeDtypeStruct((t, D_MODEL), F32),
        compiler_params=_params(1),
        name="ffn",
    )(x, mod, g, wg, wu, wd)


def _dft_cos_sin(n):
    j = np.arange(n)
    ang = 2.0 * np.pi * ((j[:, None] * j[None, :]) % n) / n
    return np.cos(ang), np.sin(ang)


PREMIX_CHUNK = 512


def _fourier_kernel(c_ref, s_ref, cs_ref, x_ref, mod_ref, g_ref, w_ref, o_ref,
                    xc_scr, xs_scr, f_scr, *, n_sub, seq, tr):
    t = pl.program_id(1)
    m = mod_ref[...]
    sh = m[:, 0:D_MODEL]
    sc = m[:, D_MODEL:2 * D_MODEL]
    gt = m[:, 2 * D_MODEL:3 * D_MODEL]

    @pl.when(t == 0)
    def _():
        cs = cs_ref[...]
        for r0 in range(0, n_sub * seq, PREMIX_CHUNK):
            rows = slice(r0, r0 + PREMIX_CHUNK)
            h = (_rms(x_ref[rows, :], g_ref[0:1, :]) * (1.0 + sc) + sh).astype(BF16)
            for j in range(N_FOURIER_GROUPS):
                lanes = slice(FOURIER_GROUP * j, FOURIER_GROUP * (j + 1))
                r = _dot(h[:, lanes], cs)
                xc_scr[rows, lanes] = r[:, :FOURIER_GROUP].astype(BF16)
                xs_scr[rows, lanes] = r[:, FOURIER_GROUP:].astype(BF16)

    for b in range(n_sub):
        src = slice(seq * b, seq * (b + 1))
        f = _dot(c_ref[...], xc_scr[src, :]) - _dot(s_ref[...], xs_scr[src, :])
        f_scr[tr * b:tr * (b + 1), :] = f.astype(BF16)
    mix = _dot(f_scr[...], w_ref[...])
    if n_sub == 1:
        x = x_ref[pl.ds(pl.multiple_of(t * tr, tr), tr), :]
    else:
        x = x_ref[...]
    o_ref[...] = x + gt * _rms(mix, g_ref[1:2, :])


def _fourier_mixer(cmat, smat, cs, x, mod, row_of_group, g, w, n_seqs, seq, n_sub, tr):
    assert n_sub == 1 or tr == seq
    n_t = seq // tr
    rows = n_sub * seq
    return pl.pallas_call(
        functools.partial(_fourier_kernel, n_sub=n_sub, seq=seq, tr=tr),
        grid=(n_seqs // n_sub, n_t),
        in_specs=[
            pl.BlockSpec((tr, seq), lambda b, t: (t, 0)),
            pl.BlockSpec((tr, seq), lambda b, t: (t, 0)),
            _const_spec((FOURIER_GROUP, 2 * FOURIER_GROUP)),
            pl.BlockSpec((rows, D_MODEL), lambda b, t: (b, 0)),
            _mod_spec(lambda b, t: row_of_group(b)),
            _const_spec((4, D_MODEL)),
            _const_spec((D_MODEL, D_MODEL)),
        ],
        out_specs=pl.BlockSpec((n_sub * tr, D_MODEL), lambda b, t: (b * n_t + t, 0)),
        out_shape=jax.ShapeDtypeStruct((n_seqs * seq, D_MODEL), F32),
        scratch_shapes=[pltpu.VMEM((rows, D_MODEL), BF16), pltpu.VMEM((rows, D_MODEL), BF16),
                        pltpu.VMEM((n_sub * tr, D_MODEL), BF16)],
        compiler_params=_params(2),
        name="fourier_mixer",
    )(cmat, smat, cs, x, mod, g, w)


def _split_pair(q, lo):
    zero = jnp.zeros_like(q)
    return jnp.concatenate([jnp.where(lo, q, zero), jnp.where(lo, zero, q)], axis=0)


def _prompt_attn_kernel(x_ref, mod_ref, g_ref, wq_ref, wkt_ref, wvt_ref, wout_ref,
                        o_ref, kt_ref, vt_ref, h_scr, q_scr, kt_scr, vt_scr, att_scr, *, n_seq):
    x = x_ref[...]
    m = mod_ref[...]
    sh = m[:, 0:D_MODEL]
    sc = m[:, D_MODEL:2 * D_MODEL]
    gt = m[:, 2 * D_MODEL:3 * D_MODEL]
    h_scr[...] = (_rms(x, g_ref[0:1, :]) * (1.0 + sc) + sh).astype(BF16)
    q_scr[...] = (_dot(h_scr[...], wq_ref[...]) * Q_SCALE).astype(BF16)

    lo = lax.broadcasted_iota(jnp.int32, (SEQ, PAIR_W), 1) < HEAD_DIM
    ones = jnp.ones((PAIR_W, SEQ), BF16)
    for b in range(n_seq):
        rows = slice(SEQ * b, SEQ * (b + 1))
        kt = _dot_nt(wkt_ref[...], h_scr[rows, :])
        vt = _dot_nt(wvt_ref[...], h_scr[rows, :])
        kt_ref[b] = kt.reshape(N_HEADS, HEAD_DIM, SEQ)
        vt_ref[b] = vt.reshape(N_HEADS, HEAD_DIM, SEQ)
        kt_scr[b] = kt.astype(BF16)
        vt_scr[b] = vt.astype(BF16)
        for j in range(N_PAIRS):
            lanes = slice(PAIR_W * j, PAIR_W * (j + 1))
            qs = _split_pair(q_scr[rows, lanes], lo)
            s = _dot(qs, kt_scr[b, lanes, :])
            p = jnp.exp(s - jnp.max(s, axis=-1, keepdims=True)).astype(BF16)
            o2 = _dot_nt(p, jnp.concatenate([vt_scr[b, lanes, :], ones], axis=0))
            o = o2[:, 0:PAIR_W] / o2[:, PAIR_W:2 * PAIR_W]
            att_scr[rows, lanes] = jnp.where(lo, o[:SEQ], o[SEQ:]).astype(BF16)

    mix = _dot(att_scr[...], wout_ref[...])
    o_ref[...] = x + gt * _rms(mix, g_ref[1:2, :])


def _prompt_attn(x, mod, row, g, wq, wkt, wvt, wout, n_seq=2):
    t = x.shape[0]
    tm = n_seq * SEQ
    tok = pl.BlockSpec((tm, D_MODEL), lambda i: (i, 0))
    out = jax.ShapeDtypeStruct((t, D_MODEL), F32)
    cache = pl.BlockSpec((n_seq, None, N_HEADS, HEAD_DIM, SEQ), lambda i: (i, 0, 0, 0, 0))
    cache_out = jax.ShapeDtypeStruct((t // SEQ, 1, N_HEADS, HEAD_DIM, SEQ), F32)
    square = _const_spec((D_MODEL, D_MODEL))
    return pl.pallas_call(
        functools.partial(_prompt_attn_kernel, n_seq=n_seq),
        grid=(t // tm,),
        in_specs=[tok, _mod_spec(lambda i: row), _const_spec((4, D_MODEL)),
                  square, square, square, square],
        out_specs=[tok, cache, cache],
        out_shape=[out, cache_out, cache_out],
        scratch_shapes=[pltpu.VMEM((tm, D_MODEL), BF16), pltpu.VMEM((tm, D_MODEL), BF16),
                        pltpu.VMEM((n_seq, D_MODEL, SEQ), BF16),
                        pltpu.VMEM((n_seq, D_MODEL, SEQ), BF16),
                        pltpu.VMEM((tm, D_MODEL), BF16)],
        compiler_params=_params(1),
        name="prompt_attn",
    )(x, mod, g, wq, wkt, wvt, wout)


def _premix_qkv_kernel(x_ref, mod_ref, g_ref, wqkv_ref, q_ref, k_ref, v_ref):
    x = x_ref[...]
    m = mod_ref[...]
    sh = m[:, 0:D_MODEL]
    sc = m[:, D_MODEL:2 * D_MODEL]
    h = (_rms(x, g_ref[0:1, :]) * (1.0 + sc) + sh).astype(BF16)
    qkv = _dot(h, wqkv_ref[...])
    q_ref[...] = (qkv[:, 0:D_MODEL] * Q_SCALE).astype(BF16)
    k_ref[...] = qkv[:, D_MODEL:2 * D_MODEL].astype(BF16)
    _store_values_with_ones(v_ref, qkv[:, 2 * D_MODEL:3 * D_MODEL].astype(BF16))


def _store_values_with_ones(v_ref, v):
    ones = jnp.ones((v.shape[0], PAIR_W), BF16)
    for j in range(N_PAIRS):
        v_ref[:, 2 * PAIR_W * j:2 * PAIR_W * j + PAIR_W] = v[:, PAIR_W * j:PAIR_W * (j + 1)]
        v_ref[:, 2 * PAIR_W * j + PAIR_W:2 * PAIR_W * (j + 1)] = ones


def _premix_qkv(x, mod, row_of_step, g, wqkv, tm=512):
    t = x.shape[0]
    tok = pl.BlockSpec((tm, D_MODEL), lambda i: (i, 0))
    tok2 = pl.BlockSpec((tm, 2 * D_MODEL), lambda i: (i, 0))
    out = jax.ShapeDtypeStruct((t, D_MODEL), BF16)
    out2 = jax.ShapeDtypeStruct((t, 2 * D_MODEL), BF16)
    return pl.pallas_call(
        _premix_qkv_kernel,
        grid=(t // tm,),
        in_specs=[tok, _mod_spec(row_of_step), _const_spec((4, D_MODEL)),
                  _const_spec((D_MODEL, 3 * D_MODEL))],
        out_specs=[tok, tok, tok2],
        out_shape=[out, out, out2],
        compiler_params=_params(1),
        name="premix_qkv",
    )(x, mod, g, wqkv)


def _ctx_prep_kernel(ckt_ref, cvt_ref, k_ref, v_ref):
    k_ref[...] = ckt_ref[...].astype(BF16)
    _store_values_with_ones(v_ref, cvt_ref[...].T.astype(BF16))


def _ctx_prep(cache_kt, cache_vt, layer_j):
    cache = pl.BlockSpec((None, None, D_MODEL, PAST_LEN), lambda b: (b, layer_j, 0, 0))
    return pl.pallas_call(
        _ctx_prep_kernel,
        grid=(DEC_BATCH,),
        in_specs=[cache, cache],
        out_specs=[pl.BlockSpec((D_MODEL, PAST_LEN), lambda b: (b, 0)),
                   pl.BlockSpec((PAST_LEN, 2 * D_MODEL), lambda b: (b, 0))],
        out_shape=[jax.ShapeDtypeStruct((DEC_BATCH * D_MODEL, PAST_LEN), BF16),
                   jax.ShapeDtypeStruct((DEC_BATCH * PAST_LEN, 2 * D_MODEL), BF16)],
        compiler_params=_params(1),
        name="ctx_prep",
    )(cache_kt, cache_vt)


def _bias_table_kernel(l_ref, o_ref):
    lane = lax.broadcasted_iota(jnp.int32, (GRID_W, PAIR_W), 1)
    qcol = lax.broadcasted_iota(jnp.int32, (GRID_W, PAIR_W), 0)
    kcol = lane & (GRID_W - 1)
    start = jnp.clip(qcol - WIN_COLS // 2, 0, GRID_W - WIN_COLS)
    in_window = (kcol >= start) & (kcol < start + WIN_COLS)
    lo = lane < GRID_W

    def toeplitz(d, shift):
        row = jnp.broadcast_to(l_ref[d:d + 1, :], (GRID_W, PAIR_W))
        return pltpu.roll(row, shift, 1, stride=1, stride_axis=0)

    for d in range(N_DR_PAIRS):
        both = jnp.where(lo, toeplitz(d, GRID_W + 1), toeplitz(d + 1, 1))
        o_ref[d] = jnp.where(in_window, both, NEG_INF)


def _bias_table(rpb):
    padded = jnp.pad(rpb, ((0, 0), (0, 0), (48, 49)), mode="edge")
    return pl.pallas_call(
        _bias_table_kernel,
        grid=(N_HEADS,),
        in_specs=[pl.BlockSpec((None, N_DR, PAIR_W), lambda h: (h, 0, 0))],
        out_specs=pl.BlockSpec((None, N_DR_PAIRS, GRID_W, PAIR_W), lambda h: (h, 0, 0, 0)),
        out_shape=jax.ShapeDtypeStruct((N_HEADS, N_DR_PAIRS, GRID_W, PAIR_W), F32),
        compiler_params=_params(1),
        name="bias_table",
    )(padded)


def _na_attn_kernel(q_ref, k_ref, v_ref, ckt_ref, cv_ref, bias_ref, x_ref, mod_ref, g_ref,
                    wout_ref, o_ref, s_scr, m_scr, p_scr, att_scr, *, rows_per_step):
    blk = pl.program_id(1)
    lo = lax.broadcasted_iota(jnp.int32, (GRID_W, PAIR_W), 1) < HEAD_DIM

    def row_geometry(i):
        r = blk * rows_per_step + i
        rs = jnp.clip(r - WIN_ROWS // 2, 0, GRID_ROWS - WIN_ROWS)
        d0 = rs - r + (WIN_ROWS - 1)
        q0 = pl.multiple_of(i * GRID_W, GRID_W)
        k0 = pl.multiple_of(rs * GRID_W, GRID_W)
        return d0, q0, k0

    def scores(i, slot):
        d0, q0, k0 = row_geometry(i)
        for j in range(N_PAIRS):
            lanes = slice(PAIR_W * j, PAIR_W * (j + 1))
            qs = _split_pair(q_ref[pl.ds(q0, GRID_W), lanes], lo)
            bias = jnp.concatenate(
                [jnp.concatenate(
                    [bias_ref[(2 * j) * N_DR_PAIRS + d0 + 2 * jj],
                     bias_ref[(2 * j + 1) * N_DR_PAIRS + d0 + 2 * jj]], axis=0)
                 for jj in range(WIN_ROWS // 2)], axis=1)
            s_loc = _dot_nt(qs, k_ref[pl.ds(k0, N_LOCAL), lanes]) + bias
            s_ctx = _dot(qs, ckt_ref[lanes, :])
            mx = jnp.maximum(jnp.max(s_loc, axis=-1, keepdims=True),
                             jnp.max(s_ctx, axis=-1, keepdims=True))
            s_scr[slot, j, :, 0:N_LOCAL] = s_loc
            s_scr[slot, j, :, N_LOCAL:N_KEYS] = s_ctx
            m_scr[slot, j] = jnp.broadcast_to(mx, (2 * GRID_W, PAIR_W))

    def probs(slot):
        for j in range(N_PAIRS):
            mx = m_scr[slot, j][:, 0:1]
            p_scr[slot, j] = jnp.exp(s_scr[slot, j] - mx).astype(BF16)

    def values(i, slot):
        _, q0, k0 = row_geometry(i)
        for j in range(N_PAIRS):
            lanes2 = slice(2 * PAIR_W * j, 2 * PAIR_W * (j + 1))
            p = p_scr[slot, j]
            o2 = (_dot(p[:, 0:N_LOCAL], v_ref[pl.ds(k0, N_LOCAL), lanes2])
                  + _dot(p[:, N_LOCAL:N_KEYS], cv_ref[:, lanes2]))
            o = o2[:, 0:PAIR_W] / o2[:, PAIR_W:2 * PAIR_W]
            att_scr[pl.ds(q0, GRID_W), PAIR_W * j:PAIR_W * (j + 1)] = (
                jnp.where(lo, o[:GRID_W], o[GRID_W:]).astype(BF16))

    scores(0, 0)
    probs(0)
    scores(1, 1)

    def two_rows(t, carry):
        i = 2 * t
        values(i - 2, 0)
        probs(1)
        scores(i, 0)
        values(i - 1, 1)
        probs(0)
        scores(i + 1, 1)
        return carry

    lax.fori_loop(1, rows_per_step // 2, two_rows, 0)
    values(rows_per_step - 2, 0)
    probs(1)
    values(rows_per_step - 1, 1)

    gt = mod_ref[...][:, 2 * D_MODEL:3 * D_MODEL]
    mix = _dot(att_scr[...], wout_ref[...])
    o_ref[...] = x_ref[...] + gt * _rms(mix, g_ref[1:2, :])


def _na_attn(q, k, v, ck, cv, bias, x, mod, row_of_batch, g, wout, rows_per_step=8):
    tm = rows_per_step * GRID_W
    n_t = DEC_SEQ // tm
    tok = pl.BlockSpec((tm, D_MODEL), lambda b, t: (b * n_t + t, 0))

    def per_batch(rows, width):
        return pl.BlockSpec((rows, width), lambda b, t: (b, 0), pipeline_mode=pl.Buffered(1))

    return pl.pallas_call(
        functools.partial(_na_attn_kernel, rows_per_step=rows_per_step),
        grid=(DEC_BATCH, n_t),
        in_specs=[
            tok,
            per_batch(DEC_SEQ, D_MODEL), per_batch(DEC_SEQ, 2 * D_MODEL),
            per_batch(D_MODEL, PAST_LEN), per_batch(PAST_LEN, 2 * D_MODEL),
            _const_spec((N_HEADS * N_DR_PAIRS, GRID_W, PAIR_W)),
            tok,
            _mod_spec(lambda b, t: row_of_batch(b)),
            _const_spec((4, D_MODEL)),
            _const_spec((D_MODEL, D_MODEL)),
        ],
        out_specs=tok,
        out_shape=jax.ShapeDtypeStruct((DEC_BATCH * DEC_SEQ, D_MODEL), F32),
        scratch_shapes=[
            pltpu.VMEM((2, N_PAIRS, 2 * GRID_W, N_KEYS), F32),
            pltpu.VMEM((2, N_PAIRS, 2 * GRID_W, PAIR_W), F32),
            pltpu.VMEM((2, N_PAIRS, 2 * GRID_W, N_KEYS), BF16),
            pltpu.VMEM((tm, D_MODEL), BF16),
        ],
        compiler_params=_params(2),
        name="na_attn",
    )(q, k, v, ck, cv, bias, x, mod, g, wout)


def kernel(x_prompt, x_sample, c, cache_k, cache_v, c_ctx, ada_w, ada_b, norm_g, fourier_w_out,
           na_w_qkv, na_rpb, na_w_out, ffn_w_gate, ffn_w_up, ffn_w_down):
    n_p = BATCH * SEQ
    n_s = DEC_BATCH * DEC_SEQ
    xp = x_prompt.reshape(n_p, D_MODEL)
    xs = x_sample.reshape(n_s, D_MODEL)

    cond = jnp.concatenate(
        [c_ctx[None, :], c, jnp.zeros((COND_ROWS - 1 - DEC_BATCH, D_MODEL), F32)], axis=0)
    mod = _modulation(cond, ada_w, ada_b).reshape(DEPTH * COND_ROWS, 1, 6 * D_MODEL)

    cos_g, sin_g = _dft_cos_sin(FOURIER_GROUP)
    cs_chan = jnp.asarray(np.concatenate([cos_g, sin_g], axis=1), F32).astype(BF16)
    cos_p, sin_p = (jnp.asarray(a, F32).astype(BF16) for a in _dft_cos_sin(SEQ))
    cos_s, sin_s = (jnp.asarray(a, F32).astype(BF16) for a in _dft_cos_sin(DEC_SEQ))

    tm = 512
    wg = ffn_w_gate.astype(BF16)
    wu = ffn_w_up.astype(BF16)
    wd = ffn_w_down.astype(BF16)
    cache_kt = jnp.transpose(cache_k, (0, 1, 3, 4, 2)).reshape(DEC_BATCH, -1, D_MODEL, PAST_LEN)
    cache_vt = jnp.transpose(cache_v, (0, 1, 3, 4, 2)).reshape(DEC_BATCH, -1, D_MODEL, PAST_LEN)
    new_kt = new_vt = None
    for layer in range(DEPTH):
        g = norm_g[layer]
        base = layer * COND_ROWS
        prompt_row = lambda *_, base=base: base
        sample_row_of_batch = lambda b, base=base: base + 1 + b
        sample_row_of_tile = lambda i, base=base: base + 1 + i // (DEC_SEQ // tm)

        if layer % 2 == 0:
            w_out = fourier_w_out[layer // 2].astype(BF16)
            xp = _fourier_mixer(cos_p, sin_p, cs_chan, xp, mod, prompt_row, g, w_out, BATCH, SEQ,
                                n_sub=4, tr=SEQ)
            xs = _fourier_mixer(cos_s, sin_s, cs_chan, xs, mod, sample_row_of_batch, g, w_out,
                                DEC_BATCH, DEC_SEQ, n_sub=1, tr=512)
        else:
            j = layer // 2
            w_qkv = na_w_qkv[j].astype(BF16)
            w_out = na_w_out[j].astype(BF16)
            w_q = w_qkv[:, 0:D_MODEL]
            w_kt = w_qkv[:, D_MODEL:2 * D_MODEL].T
            w_vt = w_qkv[:, 2 * D_MODEL:3 * D_MODEL].T
            xp, new_kt, new_vt = _prompt_attn(xp, mod, base, g, w_q, w_kt, w_vt, w_out)
            q, k, v = _premix_qkv(xs, mod, sample_row_of_tile, g, w_qkv, tm)
            ckt, cvt = _ctx_prep(cache_kt, cache_vt, j)
            bias = _bias_table(na_rpb[j]).reshape(N_HEADS * N_DR_PAIRS, GRID_W, PAIR_W)
            xs = _na_attn(q, k, v, ckt, cvt, bias, xs, mod, sample_row_of_batch, g, w_out)

        xp = _ffn(xp, mod, prompt_row, g, wg, wu, wd, layer, tm)
        xs = _ffn(xs, mod, sample_row_of_tile, g, wg, wu, wd, layer, tm)

    new_k = jnp.transpose(new_kt, (0, 1, 4, 2, 3))
    new_v = jnp.transpose(new_vt, (0, 1, 4, 2, 3))
    return (xp.reshape(BATCH, SEQ, D_MODEL), xs.reshape(DEC_BATCH, DEC_SEQ, D_MODEL), new_k, new_v)
```

```python
import functools

import numpy as np
import jax
import jax.numpy as jnp
from jax import lax
from jax.experimental import pallas as pl
from jax.experimental.pallas import tpu as pltpu

D_MODEL = 1024
BATCH = 32
SEQ = 256
DEPTH = 2
DEC_BATCH = 2
DEC_SEQ = 2048
PAST_LEN = 512
GRID_W = 64
GRID_ROWS = DEC_SEQ // GRID_W
N_HEADS = 16
HEAD_DIM = D_MODEL // N_HEADS
N_FOURIER_GROUPS = 4
FOURIER_GROUP = D_MODEL // N_FOURIER_GROUPS
WIN_ROWS = 8
WIN_COLS = 16
D_FF = 2816
EPS = 1e-6
NEG_INF = -1e30

N_PAIRS = N_HEADS // 2
PAIR_W = 2 * HEAD_DIM
N_DR = 2 * WIN_ROWS - 1
N_DR_PAIRS = N_DR - 1
N_LOCAL = WIN_ROWS * GRID_W
N_KEYS = N_LOCAL + PAST_LEN
COND_ROWS = 8
LOG2E = 1.4426950408889634
Q_SCALE = HEAD_DIM ** -0.5 * LOG2E

VMEM_LIMIT = 56 * 1024 * 1024
BF16_SUBLANES = 16

F32 = jnp.float32
BF16 = jnp.bfloat16


def _dot(a, b):
    return jnp.dot(a, b, preferred_element_type=F32)


def _dot_nt(a, b):
    return lax.dot_general(a, b, (((1,), (1,)), ((), ())), preferred_element_type=F32)


def _rms(x, g):
    ms = jnp.mean(x * x, axis=-1, keepdims=True)
    return x * lax.rsqrt(ms + EPS) * g


def _silu(x):
    return x / (1.0 + jnp.exp(-x))


def _const_spec(shape):
    return pl.BlockSpec(shape, lambda *_: (0,) * len(shape), pipeline_mode=pl.Buffered(1))


def _mod_spec(row_of_step):
    return pl.BlockSpec((None, 1, 6 * D_MODEL), lambda *idx: (row_of_step(*idx), 0, 0))


def _params(n_axes):
    return pltpu.CompilerParams(
        dimension_semantics=("arbitrary",) * n_axes, vmem_limit_bytes=VMEM_LIMIT)


def _mod_kernel(cond_ref, w_ref, b_ref, o_ref):
    a = _silu(cond_ref[...])
    w = w_ref[...]
    a_hi = a.astype(BF16)
    a_lo = (a - a_hi.astype(F32)).astype(BF16)
    w_hi = w.astype(BF16)
    w_lo = (w - w_hi.astype(F32)).astype(BF16)
    o_ref[...] = _dot(a_hi, w_hi) + _dot(a_lo, w_hi) + _dot(a_hi, w_lo) + b_ref[...]


def _modulation(cond, ada_w, ada_b):
    tn = 1536
    n_out = 6 * D_MODEL
    return pl.pallas_call(
        _mod_kernel,
        grid=(DEPTH, n_out // tn),
        in_specs=[
            pl.BlockSpec((COND_ROWS, D_MODEL), lambda l, n: (0, 0)),
            pl.BlockSpec((None, D_MODEL, tn), lambda l, n: (l, 0, n)),
            pl.BlockSpec((None, 1, tn), lambda l, n: (l, 0, n)),
        ],
        out_specs=pl.BlockSpec((None, COND_ROWS, tn), lambda l, n: (l, 0, n)),
        out_shape=jax.ShapeDtypeStruct((DEPTH, COND_ROWS, n_out), F32),
        compiler_params=_params(2),
        name="modulation",
    )(cond, ada_w, ada_b.reshape(DEPTH, 1, n_out))


FFN_SUB_ROWS = 256


def _cast_specs(jobs, n_steps, step_of):
    in_specs, out_specs, out_shapes = [], [], []
    for src, layer in jobs:
        rows, cols = src.shape[1:]
        chunk = rows // n_steps
        assert chunk * n_steps == rows and chunk % BF16_SUBLANES == 0
        in_specs.append(pl.BlockSpec(
            (None, chunk, cols), lambda *idx, layer=layer: (layer, step_of(*idx), 0)))
        out_specs.append(pl.BlockSpec((chunk, cols), lambda *idx: (step_of(*idx), 0)))
        out_shapes.append(jax.ShapeDtypeStruct((rows, cols), BF16))
    return in_specs, out_specs, out_shapes


def _run_cast_jobs(src_refs, dst_refs):
    for src, dst in zip(src_refs, dst_refs):
        dst[...] = src[...].astype(BF16)


def _ffn_kernel(x_ref, mod_ref, g_ref, wg_ref, wu_ref, wd_ref, *rest, n_cast):
    o_ref = rest[n_cast]
    _run_cast_jobs(rest[:n_cast], rest[n_cast + 1:])
    m = mod_ref[...]
    sh = m[:, 3 * D_MODEL:4 * D_MODEL]
    sc = m[:, 4 * D_MODEL:5 * D_MODEL]
    gt = m[:, 5 * D_MODEL:6 * D_MODEL]
    for r0 in range(0, x_ref.shape[0], FFN_SUB_ROWS):
        rows = slice(r0, r0 + FFN_SUB_ROWS)
        x = x_ref[rows, :]
        f = (_rms(x, g_ref[2:3, :]) * (1.0 + sc) + sh).astype(BF16)
        gate = _dot(f, wg_ref[...])
        up = _dot(f, wu_ref[...])
        act = (_silu(gate) * up).astype(BF16)
        y = _dot(act, wd_ref[...])
        o_ref[rows, :] = x + gt * _rms(y, g_ref[3:4, :])


def _ffn(x, mod, row_of_step, g, wg, wu, wd, tm, cast_jobs=()):
    t = x.shape[0]
    tok = pl.BlockSpec((tm, D_MODEL), lambda i: (i, 0))
    cast_in, cast_out, cast_shapes = _cast_specs(cast_jobs, t // tm, lambda i: i)
    outs = pl.pallas_call(
        functools.partial(_ffn_kernel, n_cast=len(cast_jobs)),
        grid=(t // tm,),
        in_specs=[
            tok,
            _mod_spec(row_of_step),
            _const_spec((4, D_MODEL)),
            _const_spec((D_MODEL, D_FF)),
            _const_spec((D_MODEL, D_FF)),
            _const_spec((D_FF, D_MODEL)),
        ] + cast_in,
        out_specs=[tok] + cast_out,
        out_shape=[jax.ShapeDtypeStruct((t, D_MODEL), F32)] + cast_shapes,
        compiler_params=_params(1),
        name="ffn",
    )(x, mod, g, wg, wu, wd, *[src for src, _ in cast_jobs])
    return outs[0], outs[1:]


def _dft_cos_sin(n):
    j = np.arange(n)
    ang = 2.0 * np.pi * ((j[:, None] * j[None, :]) % n) / n
    return np.cos(ang), np.sin(ang)


PREMIX_CHUNK = 512


def _fourier_kernel(c_ref, s_ref, cs_ref, x_ref, mod_ref, g_ref, w_ref, *rest,
                    n_sub, seq, tr, n_cast):
    o_ref = rest[n_cast]
    xc_scr, xs_scr, f_scr = rest[2 * n_cast + 1:]
    _run_cast_jobs(rest[:n_cast], rest[n_cast + 1:2 * n_cast + 1])
    t = pl.program_id(1)
    m = mod_ref[...]
    sh = m[:, 0:D_MODEL]
    sc = m[:, D_MODEL:2 * D_MODEL]
    gt = m[:, 2 * D_MODEL:3 * D_MODEL]

    @pl.when(t == 0)
    def _():
        cs = cs_ref[...]
        for r0 in range(0, n_sub * seq, PREMIX_CHUNK):
            rows = slice(r0, r0 + PREMIX_CHUNK)
            h = (_rms(x_ref[rows, :], g_ref[0:1, :]) * (1.0 + sc) + sh).astype(BF16)
            for j in range(N_FOURIER_GROUPS):
                lanes = slice(FOURIER_GROUP * j, FOURIER_GROUP * (j + 1))
                r = _dot(h[:, lanes], cs)
                xc_scr[rows, lanes] = r[:, :FOURIER_GROUP].astype(BF16)
                xs_scr[rows, lanes] = r[:, FOURIER_GROUP:].astype(BF16)

    for b in range(n_sub):
        src = slice(seq * b, seq * (b + 1))
        f = _dot(c_ref[...], xc_scr[src, :]) - _dot(s_ref[...], xs_scr[src, :])
        f_scr[tr * b:tr * (b + 1), :] = f.astype(BF16)
    mix = _dot(f_scr[...], w_ref[...])
    if n_sub == 1:
        x = x_ref[pl.ds(pl.multiple_of(t * tr, tr), tr), :]
    else:
        x = x_ref[...]
    o_ref[...] = x + gt * _rms(mix, g_ref[1:2, :])


def _fourier_mixer(cmat, smat, cs, x, mod, row_of_group, g, w, n_seqs, seq, n_sub, tr,
                   cast_jobs=()):
    assert n_sub == 1 or tr == seq
    n_t = seq // tr
    n_groups = n_seqs // n_sub
    rows = n_sub * seq
    cast_in, cast_out, cast_shapes = _cast_specs(
        cast_jobs, n_groups * n_t, lambda b, t: b * n_t + t)
    outs = pl.pallas_call(
        functools.partial(_fourier_kernel, n_sub=n_sub, seq=seq, tr=tr, n_cast=len(cast_jobs)),
        grid=(n_groups, n_t),
        in_specs=[
            pl.BlockSpec((tr, seq), lambda b, t: (t, 0)),
            pl.BlockSpec((tr, seq), lambda b, t: (t, 0)),
            _const_spec((FOURIER_GROUP, 2 * FOURIER_GROUP)),
            pl.BlockSpec((rows, D_MODEL), lambda b, t: (b, 0)),
            _mod_spec(lambda b, t: row_of_group(b)),
            _const_spec((4, D_MODEL)),
            _const_spec((D_MODEL, D_MODEL)),
        ] + cast_in,
        out_specs=[pl.BlockSpec((n_sub * tr, D_MODEL), lambda b, t: (b * n_t + t, 0))] + cast_out,
        out_shape=[jax.ShapeDtypeStruct((n_seqs * seq, D_MODEL), F32)] + cast_shapes,
        scratch_shapes=[pltpu.VMEM((rows, D_MODEL), BF16), pltpu.VMEM((rows, D_MODEL), BF16),
                        pltpu.VMEM((n_sub * tr, D_MODEL), BF16)],
        compiler_params=_params(2),
        name="fourier_mixer",
    )(cmat, smat, cs, x, mod, g, w, *[src for src, _ in cast_jobs])
    return outs[0], outs[1:]


def _split_pair(q, lo):
    zero = jnp.zeros_like(q)
    return jnp.concatenate([jnp.where(lo, q, zero), jnp.where(lo, zero, q)], axis=0)


def _prompt_attn_kernel(x_ref, mod_ref, g_ref, wq_ref, wkt_ref, wvt_ref, wout_ref,
                        o_ref, kt_ref, vt_ref, h_scr, q_scr, kt_scr, vt_scr, att_scr, *, n_seq):
    x = x_ref[...]
    m = mod_ref[...]
    sh = m[:, 0:D_MODEL]
    sc = m[:, D_MODEL:2 * D_MODEL]
    gt = m[:, 2 * D_MODEL:3 * D_MODEL]
    h_scr[...] = (_rms(x, g_ref[0:1, :]) * (1.0 + sc) + sh).astype(BF16)
    q_scr[...] = (_dot(h_scr[...], wq_ref[...]) * Q_SCALE).astype(BF16)

    lo = lax.broadcasted_iota(jnp.int32, (SEQ, PAIR_W), 1) < HEAD_DIM
    ones = jnp.ones((PAIR_W, SEQ), BF16)
    for b in range(n_seq):
        rows = slice(SEQ * b, SEQ * (b + 1))
        kt = _dot_nt(wkt_ref[...], h_scr[rows, :])
        vt = _dot_nt(wvt_ref[...], h_scr[rows, :])
        kt_ref[b] = kt.reshape(N_HEADS, HEAD_DIM, SEQ)
        vt_ref[b] = vt.reshape(N_HEADS, HEAD_DIM, SEQ)
        kt_scr[b] = kt.astype(BF16)
        vt_scr[b] = vt.astype(BF16)
        for j in range(N_PAIRS):
            lanes = slice(PAIR_W * j, PAIR_W * (j + 1))
            qs = _split_pair(q_scr[rows, lanes], lo)
            s = _dot(qs, kt_scr[b, lanes, :])
            p = jnp.exp2((s - jnp.max(s, axis=-1, keepdims=True)).astype(BF16))
            o2 = _dot_nt(p, jnp.concatenate([vt_scr[b, lanes, :], ones], axis=0))
            o = o2[:, 0:PAIR_W] / o2[:, PAIR_W:2 * PAIR_W]
            att_scr[rows, lanes] = jnp.where(lo, o[:SEQ], o[SEQ:]).astype(BF16)

    mix = _dot(att_scr[...], wout_ref[...])
    o_ref[...] = x + gt * _rms(mix, g_ref[1:2, :])


def _prompt_attn(x, mod, row, g, wq, wkt, wvt, wout, n_seq=2):
    t = x.shape[0]
    tm = n_seq * SEQ
    tok = pl.BlockSpec((tm, D_MODEL), lambda i: (i, 0))
    out = jax.ShapeDtypeStruct((t, D_MODEL), F32)
    cache = pl.BlockSpec((n_seq, None, N_HEADS, HEAD_DIM, SEQ), lambda i: (i, 0, 0, 0, 0))
    cache_out = jax.ShapeDtypeStruct((t // SEQ, 1, N_HEADS, HEAD_DIM, SEQ), F32)
    square = _const_spec((D_MODEL, D_MODEL))
    return pl.pallas_call(
        functools.partial(_prompt_attn_kernel, n_seq=n_seq),
        grid=(t // tm,),
        in_specs=[tok, _mod_spec(lambda i: row), _const_spec((4, D_MODEL)),
                  square, square, square, square],
        out_specs=[tok, cache, cache],
        out_shape=[out, cache_out, cache_out],
        scratch_shapes=[pltpu.VMEM((tm, D_MODEL), BF16), pltpu.VMEM((tm, D_MODEL), BF16),
                        pltpu.VMEM((n_seq, D_MODEL, SEQ), BF16),
                        pltpu.VMEM((n_seq, D_MODEL, SEQ), BF16),
                        pltpu.VMEM((tm, D_MODEL), BF16)],
        compiler_params=_params(1),
        name="prompt_attn",
    )(x, mod, g, wq, wkt, wvt, wout)


def _premix_qkv_kernel(x_ref, mod_ref, g_ref, wqkv_ref, q_ref, k_ref, v_ref):
    x = x_ref[...]
    m = mod_ref[...]
    sh = m[:, 0:D_MODEL]
    sc = m[:, D_MODEL:2 * D_MODEL]
    h = (_rms(x, g_ref[0:1, :]) * (1.0 + sc) + sh).astype(BF16)
    qkv = _dot(h, wqkv_ref[...])
    q_ref[...] = (qkv[:, 0:D_MODEL] * Q_SCALE).astype(BF16)
    k_ref[...] = qkv[:, D_MODEL:2 * D_MODEL].astype(BF16)
    _store_values_with_ones(v_ref, qkv[:, 2 * D_MODEL:3 * D_MODEL].astype(BF16))


def _store_values_with_ones(v_ref, v):
    ones = jnp.ones((v.shape[0], PAIR_W), BF16)
    for j in range(N_PAIRS):
        v_ref[:, 2 * PAIR_W * j:2 * PAIR_W * j + PAIR_W] = v[:, PAIR_W * j:PAIR_W * (j + 1)]
        v_ref[:, 2 * PAIR_W * j + PAIR_W:2 * PAIR_W * (j + 1)] = ones


def _premix_qkv(x, mod, row_of_step, g, wqkv, tm=512):
    t = x.shape[0]
    tok = pl.BlockSpec((tm, D_MODEL), lambda i: (i, 0))
    tok2 = pl.BlockSpec((tm, 2 * D_MODEL), lambda i: (i, 0))
    out = jax.ShapeDtypeStruct((t, D_MODEL), BF16)
    out2 = jax.ShapeDtypeStruct((t, 2 * D_MODEL), BF16)
    return pl.pallas_call(
        _premix_qkv_kernel,
        grid=(t // tm,),
        in_specs=[tok, _mod_spec(row_of_step), _const_spec((4, D_MODEL)),
                  _const_spec((D_MODEL, 3 * D_MODEL))],
        out_specs=[tok, tok, tok2],
        out_shape=[out, out, out2],
        compiler_params=_params(1),
        name="premix_qkv",
    )(x, mod, g, wqkv)


def _ctx_prep_kernel(ckt_ref, cvt_ref, k_ref, v_ref):
    k_ref[...] = ckt_ref[...].astype(BF16)
    _store_values_with_ones(v_ref, cvt_ref[...].T.astype(BF16))


def _ctx_prep(cache_kt, cache_vt, layer_j):
    cache = pl.BlockSpec((None, None, D_MODEL, PAST_LEN), lambda b: (b, layer_j, 0, 0))
    return pl.pallas_call(
        _ctx_prep_kernel,
        grid=(DEC_BATCH,),
        in_specs=[cache, cache],
        out_specs=[pl.BlockSpec((D_MODEL, PAST_LEN), lambda b: (b, 0)),
                   pl.BlockSpec((PAST_LEN, 2 * D_MODEL), lambda b: (b, 0))],
        out_shape=[jax.ShapeDtypeStruct((DEC_BATCH * D_MODEL, PAST_LEN), BF16),
                   jax.ShapeDtypeStruct((DEC_BATCH * PAST_LEN, 2 * D_MODEL), BF16)],
        compiler_params=_params(1),
        name="ctx_prep",
    )(cache_kt, cache_vt)


def _bias_table_kernel(l_ref, o_ref):
    lane = lax.broadcasted_iota(jnp.int32, (GRID_W, PAIR_W), 1)
    qcol = lax.broadcasted_iota(jnp.int32, (GRID_W, PAIR_W), 0)
    kcol = lane & (GRID_W - 1)
    start = jnp.clip(qcol - WIN_COLS // 2, 0, GRID_W - WIN_COLS)
    in_window = (kcol >= start) & (kcol < start + WIN_COLS)
    lo = lane < GRID_W

    def toeplitz(d, shift):
        row = jnp.broadcast_to(l_ref[d:d + 1, :], (GRID_W, PAIR_W))
        return pltpu.roll(row, shift, 1, stride=1, stride_axis=0)

    for d in range(N_DR_PAIRS):
        both = jnp.where(lo, toeplitz(d, GRID_W + 1), toeplitz(d + 1, 1))
        o_ref[d] = jnp.where(in_window, both * LOG2E, NEG_INF)


def _bias_table(rpb):
    padded = jnp.pad(rpb, ((0, 0), (0, 0), (48, 49)), mode="edge")
    return pl.pallas_call(
        _bias_table_kernel,
        grid=(N_HEADS,),
        in_specs=[pl.BlockSpec((None, N_DR, PAIR_W), lambda h: (h, 0, 0))],
        out_specs=pl.BlockSpec((None, N_DR_PAIRS, GRID_W, PAIR_W), lambda h: (h, 0, 0, 0)),
        out_shape=jax.ShapeDtypeStruct((N_HEADS, N_DR_PAIRS, GRID_W, PAIR_W), F32),
        compiler_params=_params(1),
        name="bias_table",
    )(padded)


def _na_attn_kernel(q_ref, k_ref, v_ref, ckt_ref, cv_ref, bias_ref, x_ref, mod_ref, g_ref,
                    wout_ref, o_ref, s_scr, m_scr, p_scr, att_scr, *, rows_per_step):
    blk = pl.program_id(1)
    lo = lax.broadcasted_iota(jnp.int32, (GRID_W, PAIR_W), 1) < HEAD_DIM

    def row_geometry(i):
        r = blk * rows_per_step + i
        rs = jnp.clip(r - WIN_ROWS // 2, 0, GRID_ROWS - WIN_ROWS)
        d0 = rs - r + (WIN_ROWS - 1)
        q0 = pl.multiple_of(i * GRID_W, GRID_W)
        k0 = pl.multiple_of(rs * GRID_W, GRID_W)
        return d0, q0, k0

    def scores(i, slot):
        d0, q0, k0 = row_geometry(i)
        for j in range(N_PAIRS):
            lanes = slice(PAIR_W * j, PAIR_W * (j + 1))
            qs = _split_pair(q_ref[pl.ds(q0, GRID_W), lanes], lo)
            bias = jnp.concatenate(
                [jnp.concatenate(
                    [bias_ref[(2 * j) * N_DR_PAIRS + d0 + 2 * jj],
                     bias_ref[(2 * j + 1) * N_DR_PAIRS + d0 + 2 * jj]], axis=0)
                 for jj in range(WIN_ROWS // 2)], axis=1)
            s_loc = _dot_nt(qs, k_ref[pl.ds(k0, N_LOCAL), lanes]) + bias
            s_ctx = _dot(qs, ckt_ref[lanes, :])
            mx = jnp.maximum(jnp.max(s_loc, axis=-1, keepdims=True),
                             jnp.max(s_ctx, axis=-1, keepdims=True))
            s_scr[slot, j, :, 0:N_LOCAL] = s_loc
            s_scr[slot, j, :, N_LOCAL:N_KEYS] = s_ctx
            m_scr[slot, j] = jnp.broadcast_to(mx, (2 * GRID_W, PAIR_W))

    def probs(slot):
        for j in range(N_PAIRS):
            mx = m_scr[slot, j][:, 0:1]
            p_scr[slot, j] = jnp.exp2((s_scr[slot, j] - mx).astype(BF16))

    def values(i, slot):
        _, q0, k0 = row_geometry(i)
        for j in range(N_PAIRS):
            lanes2 = slice(2 * PAIR_W * j, 2 * PAIR_W * (j + 1))
            p = p_scr[slot, j]
            o2 = (_dot(p[:, 0:N_LOCAL], v_ref[pl.ds(k0, N_LOCAL), lanes2])
                  + _dot(p[:, N_LOCAL:N_KEYS], cv_ref[:, lanes2]))
            o = o2[:, 0:PAIR_W] / o2[:, PAIR_W:2 * PAIR_W]
            att_scr[pl.ds(q0, GRID_W), PAIR_W * j:PAIR_W * (j + 1)] = (
                jnp.where(lo, o[:GRID_W], o[GRID_W:]).astype(BF16))

    scores(0, 0)
    probs(0)
    scores(1, 1)

    def two_rows(t, carry):
        i = 2 * t
        values(i - 2, 0)
        probs(1)
        scores(i, 0)
        values(i - 1, 1)
        probs(0)
        scores(i + 1, 1)
        return carry

    lax.fori_loop(1, rows_per_step // 2, two_rows, 0)
    values(rows_per_step - 2, 0)
    probs(1)
    values(rows_per_step - 1, 1)

    gt = mod_ref[...][:, 2 * D_MODEL:3 * D_MODEL]
    mix = _dot(att_scr[...], wout_ref[...])
    o_ref[...] = x_ref[...] + gt * _rms(mix, g_ref[1:2, :])


def _na_attn(q, k, v, ck, cv, bias, x, mod, row_of_batch, g, wout, rows_per_step=8):
    tm = rows_per_step * GRID_W
    n_t = DEC_SEQ // tm
    tok = pl.BlockSpec((tm, D_MODEL), lambda b, t: (b * n_t + t, 0))

    def per_batch(rows, width):
        return pl.BlockSpec((rows, width), lambda b, t: (b, 0), pipeline_mode=pl.Buffered(1))

    return pl.pallas_call(
        functools.partial(_na_attn_kernel, rows_per_step=rows_per_step),
        grid=(DEC_BATCH, n_t),
        in_specs=[
            tok,
            per_batch(DEC_SEQ, D_MODEL), per_batch(DEC_SEQ, 2 * D_MODEL),
            per_batch(D_MODEL, PAST_LEN), per_batch(PAST_LEN, 2 * D_MODEL),
            _const_spec((N_HEADS * N_DR_PAIRS, GRID_W, PAIR_W)),
            tok,
            _mod_spec(lambda b, t: row_of_batch(b)),
            _const_spec((4, D_MODEL)),
            _const_spec((D_MODEL, D_MODEL)),
        ],
        out_specs=tok,
        out_shape=jax.ShapeDtypeStruct((DEC_BATCH * DEC_SEQ, D_MODEL), F32),
        scratch_shapes=[
            pltpu.VMEM((2, N_PAIRS, 2 * GRID_W, N_KEYS), F32),
            pltpu.VMEM((2, N_PAIRS, 2 * GRID_W, PAIR_W), F32),
            pltpu.VMEM((2, N_PAIRS, 2 * GRID_W, N_KEYS), BF16),
            pltpu.VMEM((tm, D_MODEL), BF16),
        ],
        compiler_params=_params(2),
        name="na_attn",
    )(q, k, v, ck, cv, bias, x, mod, g, wout)


def kernel(x_prompt, x_sample, c, cache_k, cache_v, c_ctx, ada_w, ada_b, norm_g, fourier_w_out,
           na_w_qkv, na_rpb, na_w_out, ffn_w_gate, ffn_w_up, ffn_w_down):
    n_p = BATCH * SEQ
    n_s = DEC_BATCH * DEC_SEQ
    xp = x_prompt.reshape(n_p, D_MODEL)
    xs = x_sample.reshape(n_s, D_MODEL)

    cond = jnp.concatenate(
        [c_ctx[None, :], c, jnp.zeros((COND_ROWS - 1 - DEC_BATCH, D_MODEL), F32)], axis=0)
    mod = _modulation(cond, ada_w, ada_b).reshape(DEPTH * COND_ROWS, 1, 6 * D_MODEL)

    cos_g, sin_g = _dft_cos_sin(FOURIER_GROUP)
    cs_chan = jnp.asarray(np.concatenate([cos_g, sin_g], axis=1), F32).astype(BF16)
    cos_p, sin_p = (jnp.asarray(a, F32).astype(BF16) for a in _dft_cos_sin(SEQ))
    cos_s, sin_s = (jnp.asarray(a, F32).astype(BF16) for a in _dft_cos_sin(DEC_SEQ))

    tm = 512
    ffn_tm = 1024
    ffn_weights = (ffn_w_gate, ffn_w_up, ffn_w_down)
    cache_kt = jnp.transpose(cache_k, (0, 1, 3, 4, 2)).reshape(DEC_BATCH, -1, D_MODEL, PAST_LEN)
    cache_vt = jnp.transpose(cache_v, (0, 1, 3, 4, 2)).reshape(DEC_BATCH, -1, D_MODEL, PAST_LEN)
    new_kt = new_vt = None
    for layer in range(DEPTH):
        g = norm_g[layer]
        base = layer * COND_ROWS
        prompt_row = lambda *_, base=base: base
        sample_row_of_batch = lambda b, base=base: base + 1 + b
        sample_row_of_tile = lambda i, base=base: base + 1 + i // (DEC_SEQ // tm)
        sample_row_of_ffn_tile = lambda i, base=base: base + 1 + i // (DEC_SEQ // ffn_tm)

        if layer % 2 == 0:
            w_out = fourier_w_out[layer // 2].astype(BF16)
            jobs = [(w, layer) for w in ffn_weights] if layer == 0 else []
            xp, cast_p = _fourier_mixer(cos_p, sin_p, cs_chan, xp, mod, prompt_row, g, w_out,
                                        BATCH, SEQ, n_sub=4, tr=SEQ, cast_jobs=jobs[:2])
            xs, cast_s = _fourier_mixer(cos_s, sin_s, cs_chan, xs, mod, sample_row_of_batch, g,
                                        w_out, DEC_BATCH, DEC_SEQ, n_sub=1, tr=512,
                                        cast_jobs=jobs[2:])
            if layer == 0:
                wg, wu, wd = list(cast_p) + list(cast_s)
        else:
            j = layer // 2
            w_qkv = na_w_qkv[j].astype(BF16)
            w_out = na_w_out[j].astype(BF16)
            w_q = w_qkv[:, 0:D_MODEL]
            w_kt = w_qkv[:, D_MODEL:2 * D_MODEL].T
            w_vt = w_qkv[:, 2 * D_MODEL:3 * D_MODEL].T
            xp, new_kt, new_vt = _prompt_attn(xp, mod, base, g, w_q, w_kt, w_vt, w_out)
            q, k, v = _premix_qkv(xs, mod, sample_row_of_tile, g, w_qkv, tm)
            ckt, cvt = _ctx_prep(cache_kt, cache_vt, j)
            bias = _bias_table(na_rpb[j]).reshape(N_HEADS * N_DR_PAIRS, GRID_W, PAIR_W)
            xs = _na_attn(q, k, v, ckt, cvt, bias, xs, mod, sample_row_of_batch, g, w_out)

        jobs = [(w, layer + 1) for w in ffn_weights] if layer + 1 < DEPTH else []
        xp, cast_next = _ffn(xp, mod, prompt_row, g, wg, wu, wd, tm, cast_jobs=jobs)
        xs, _ = _ffn(xs, mod, sample_row_of_ffn_tile, g, wg, wu, wd, ffn_tm)
        if jobs:
            wg, wu, wd = cast_next

    new_k = jnp.transpose(new_kt, (0, 1, 4, 2, 3))
    new_v = jnp.transpose(new_vt, (0, 1, 4, 2, 3))
    return (xp.reshape(BATCH, SEQ, D_MODEL), xs.reshape(DEC_BATCH, DEC_SEQ, D_MODEL), new_k, new_v)
```

```python
import functools

import numpy as np
import jax
import jax.numpy as jnp
from jax import lax
from jax.experimental import pallas as pl
from jax.experimental.pallas import tpu as pltpu

D_MODEL = 1024
BATCH = 32
SEQ = 256
DEPTH = 2
DEC_BATCH = 2
DEC_SEQ = 2048
PAST_LEN = 512
GRID_W = 64
GRID_ROWS = DEC_SEQ // GRID_W
N_HEADS = 16
HEAD_DIM = D_MODEL // N_HEADS
N_FOURIER_GROUPS = 4
FOURIER_GROUP = D_MODEL // N_FOURIER_GROUPS
WIN_ROWS = 8
WIN_COLS = 16
D_FF = 2816
EPS = 1e-6
NEG_INF = -1e30

N_PAIRS = N_HEADS // 2
PAIR_W = 2 * HEAD_DIM
N_DR = 2 * WIN_ROWS - 1
N_DR_PAIRS = N_DR - 1
N_LOCAL = WIN_ROWS * GRID_W
N_KEYS = N_LOCAL + PAST_LEN
COND_ROWS = 8
LOG2E = 1.4426950408889634
Q_SCALE = HEAD_DIM ** -0.5 * LOG2E

VMEM_LIMIT = 56 * 1024 * 1024
BF16_SUBLANES = 16

F32 = jnp.float32
BF16 = jnp.bfloat16


def _dot(a, b):
    return jnp.dot(a, b, preferred_element_type=F32)


def _dot_nt(a, b):
    return lax.dot_general(a, b, (((1,), (1,)), ((), ())), preferred_element_type=F32)


def _rms(x, g):
    ms = jnp.mean(x * x, axis=-1, keepdims=True)
    return x * lax.rsqrt(ms + EPS) * g


def _silu(x):
    return x / (1.0 + jnp.exp(-x))


def _const_spec(shape):
    return pl.BlockSpec(shape, lambda *_: (0,) * len(shape), pipeline_mode=pl.Buffered(1))


def _mod_spec(row_of_step):
    return pl.BlockSpec((None, 1, 6 * D_MODEL), lambda *idx: (row_of_step(*idx), 0, 0))


def _params(n_axes):
    return pltpu.CompilerParams(
        dimension_semantics=("arbitrary",) * n_axes, vmem_limit_bytes=VMEM_LIMIT)


def _mod_kernel(cond_ref, w_ref, b_ref, o_ref):
    a = _silu(cond_ref[...])
    a_hi = a.astype(BF16)
    a_lo = (a - a_hi.astype(F32)).astype(BF16)
    w = w_ref[...].astype(BF16)
    o_ref[...] = _dot(a_hi, w) + _dot(a_lo, w) + b_ref[...]


def _modulation(cond, ada_w, ada_b):
    tn = 3072
    n_out = 6 * D_MODEL
    return pl.pallas_call(
        _mod_kernel,
        grid=(DEPTH, n_out // tn),
        in_specs=[
            pl.BlockSpec((COND_ROWS, D_MODEL), lambda l, n: (0, 0)),
            pl.BlockSpec((None, D_MODEL, tn), lambda l, n: (l, 0, n)),
            pl.BlockSpec((None, 1, tn), lambda l, n: (l, 0, n)),
        ],
        out_specs=pl.BlockSpec((None, COND_ROWS, tn), lambda l, n: (l, 0, n)),
        out_shape=jax.ShapeDtypeStruct((DEPTH, COND_ROWS, n_out), F32),
        compiler_params=_params(2),
        name="modulation",
    )(cond, ada_w, ada_b.reshape(DEPTH, 1, n_out))


FFN_SUB_ROWS = 256


def _cast_specs(jobs, n_steps, step_of):
    in_specs, out_specs, out_shapes = [], [], []
    for src, layer in jobs:
        rows, cols = src.shape[1:]
        chunk = rows // n_steps
        assert chunk * n_steps == rows and chunk % BF16_SUBLANES == 0
        in_specs.append(pl.BlockSpec(
            (None, chunk, cols), lambda *idx, layer=layer: (layer, step_of(*idx), 0)))
        out_specs.append(pl.BlockSpec((chunk, cols), lambda *idx: (step_of(*idx), 0)))
        out_shapes.append(jax.ShapeDtypeStruct((rows, cols), BF16))
    return in_specs, out_specs, out_shapes


def _run_cast_jobs(src_refs, dst_refs):
    for src, dst in zip(src_refs, dst_refs):
        dst[...] = src[...].astype(BF16)


def _ffn_kernel(x_ref, mod_ref, g_ref, wg_ref, wu_ref, wd_ref, *rest, n_cast):
    o_ref = rest[n_cast]
    _run_cast_jobs(rest[:n_cast], rest[n_cast + 1:])
    m = mod_ref[...]
    sh = m[:, 3 * D_MODEL:4 * D_MODEL]
    sc = m[:, 4 * D_MODEL:5 * D_MODEL]
    gt = m[:, 5 * D_MODEL:6 * D_MODEL]
    for r0 in range(0, x_ref.shape[0], FFN_SUB_ROWS):
        rows = slice(r0, r0 + FFN_SUB_ROWS)
        x = x_ref[rows, :]
        f = (_rms(x, g_ref[2:3, :]) * (1.0 + sc) + sh).astype(BF16)
        gate = _dot(f, wg_ref[...])
        up = _dot(f, wu_ref[...])
        act = (_silu(gate) * up).astype(BF16)
        y = _dot(act, wd_ref[...])
        o_ref[rows, :] = x + gt * _rms(y, g_ref[3:4, :])


def _ffn(x, mod, row_of_step, g, wg, wu, wd, tm, cast_jobs=()):
    t = x.shape[0]
    tok = pl.BlockSpec((tm, D_MODEL), lambda i: (i, 0))
    cast_in, cast_out, cast_shapes = _cast_specs(cast_jobs, t // tm, lambda i: i)
    outs = pl.pallas_call(
        functools.partial(_ffn_kernel, n_cast=len(cast_jobs)),
        grid=(t // tm,),
        in_specs=[
            tok,
            _mod_spec(row_of_step),
            _const_spec((4, D_MODEL)),
            _const_spec((D_MODEL, D_FF)),
            _const_spec((D_MODEL, D_FF)),
            _const_spec((D_FF, D_MODEL)),
        ] + cast_in,
        out_specs=[tok] + cast_out,
        out_shape=[jax.ShapeDtypeStruct((t, D_MODEL), F32)] + cast_shapes,
        compiler_params=_params(1),
        name="ffn",
    )(x, mod, g, wg, wu, wd, *[src for src, _ in cast_jobs])
    return outs[0], outs[1:]


def _dft_cos_sin(n):
    j = np.arange(n)
    ang = 2.0 * np.pi * ((j[:, None] * j[None, :]) % n) / n
    return np.cos(ang), np.sin(ang)


PREMIX_CHUNK = 512


def _fourier_kernel(c_ref, s_ref, cs_ref, x_ref, mod_ref, g_ref, w_ref, *rest,
                    n_sub, seq, tr, n_cast):
    o_ref = rest[n_cast]
    xc_scr, xs_scr, f_scr, w_scr = rest[2 * n_cast + 1:]
    _run_cast_jobs(rest[:n_cast], rest[n_cast + 1:2 * n_cast + 1])
    t = pl.program_id(1)
    m = mod_ref[...]
    sh = m[:, 0:D_MODEL]
    sc = m[:, D_MODEL:2 * D_MODEL]
    gt = m[:, 2 * D_MODEL:3 * D_MODEL]

    @pl.when((pl.program_id(0) == 0) & (t == 0))
    def _():
        w_scr[...] = w_ref[...].astype(BF16)

    @pl.when(t == 0)
    def _():
        cs = cs_ref[...]
        for r0 in range(0, n_sub * seq, PREMIX_CHUNK):
            rows = slice(r0, r0 + PREMIX_CHUNK)
            h = (_rms(x_ref[rows, :], g_ref[0:1, :]) * (1.0 + sc) + sh).astype(BF16)
            for j in range(N_FOURIER_GROUPS):
                lanes = slice(FOURIER_GROUP * j, FOURIER_GROUP * (j + 1))
                r = _dot(h[:, lanes], cs)
                xc_scr[rows, lanes] = r[:, :FOURIER_GROUP].astype(BF16)
                xs_scr[rows, lanes] = r[:, FOURIER_GROUP:].astype(BF16)

    for b in range(n_sub):
        src = slice(seq * b, seq * (b + 1))
        f = _dot(c_ref[...], xc_scr[src, :]) - _dot(s_ref[...], xs_scr[src, :])
        f_scr[tr * b:tr * (b + 1), :] = f.astype(BF16)
    mix = _dot(f_scr[...], w_scr[...])
    if n_sub == 1:
        x = x_ref[pl.ds(pl.multiple_of(t * tr, tr), tr), :]
    else:
        x = x_ref[...]
    o_ref[...] = x + gt * _rms(mix, g_ref[1:2, :])


def _fourier_mixer(cmat, smat, cs, x, mod, row_of_group, g, w, n_seqs, seq, n_sub, tr,
                   cast_jobs=()):
    assert n_sub == 1 or tr == seq
    n_t = seq // tr
    n_groups = n_seqs // n_sub
    rows = n_sub * seq
    cast_in, cast_out, cast_shapes = _cast_specs(
        cast_jobs, n_groups * n_t, lambda b, t: b * n_t + t)
    outs = pl.pallas_call(
        functools.partial(_fourier_kernel, n_sub=n_sub, seq=seq, tr=tr, n_cast=len(cast_jobs)),
        grid=(n_groups, n_t),
        in_specs=[
            pl.BlockSpec((tr, seq), lambda b, t: (t, 0)),
            pl.BlockSpec((tr, seq), lambda b, t: (t, 0)),
            _const_spec((FOURIER_GROUP, 2 * FOURIER_GROUP)),
            pl.BlockSpec((rows, D_MODEL), lambda b, t: (b, 0)),
            _mod_spec(lambda b, t: row_of_group(b)),
            _const_spec((4, D_MODEL)),
            _const_spec((D_MODEL, D_MODEL)),
        ] + cast_in,
        out_specs=[pl.BlockSpec((n_sub * tr, D_MODEL), lambda b, t: (b * n_t + t, 0))] + cast_out,
        out_shape=[jax.ShapeDtypeStruct((n_seqs * seq, D_MODEL), F32)] + cast_shapes,
        scratch_shapes=[pltpu.VMEM((rows, D_MODEL), BF16), pltpu.VMEM((rows, D_MODEL), BF16),
                        pltpu.VMEM((n_sub * tr, D_MODEL), BF16),
                        pltpu.VMEM((D_MODEL, D_MODEL), BF16)],
        compiler_params=_params(2),
        name="fourier_mixer",
    )(cmat, smat, cs, x, mod, g, w, *[src for src, _ in cast_jobs])
    return outs[0], outs[1:]


def _split_pair(q, lo):
    zero = jnp.zeros_like(q)
    return jnp.concatenate([jnp.where(lo, q, zero), jnp.where(lo, zero, q)], axis=0)


def _prompt_attn_kernel(x_ref, mod_ref, g_ref, wq_ref, wk_ref, wv_ref, wout_ref,
                        o_ref, kt_ref, vt_ref, wkt_ref, wvt_ref, h_scr, q_scr, kt_scr, vt_scr,
                        att_scr, *, n_seq):
    @pl.when(pl.program_id(0) == 0)
    def _():
        wkt_ref[...] = wk_ref[...].T
        wvt_ref[...] = wv_ref[...].T

    x = x_ref[...]
    m = mod_ref[...]
    sh = m[:, 0:D_MODEL]
    sc = m[:, D_MODEL:2 * D_MODEL]
    gt = m[:, 2 * D_MODEL:3 * D_MODEL]
    h_scr[...] = (_rms(x, g_ref[0:1, :]) * (1.0 + sc) + sh).astype(BF16)
    q_scr[...] = (_dot(h_scr[...], wq_ref[...]) * Q_SCALE).astype(BF16)

    lo = lax.broadcasted_iota(jnp.int32, (SEQ, PAIR_W), 1) < HEAD_DIM
    ones = jnp.ones((PAIR_W, SEQ), BF16)
    for b in range(n_seq):
        rows = slice(SEQ * b, SEQ * (b + 1))
        kt = _dot_nt(wkt_ref[...], h_scr[rows, :])
        vt = _dot_nt(wvt_ref[...], h_scr[rows, :])
        kt_ref[b] = kt.reshape(N_HEADS, HEAD_DIM, SEQ)
        vt_ref[b] = vt.reshape(N_HEADS, HEAD_DIM, SEQ)
        kt_scr[b] = kt.astype(BF16)
        vt_scr[b] = vt.astype(BF16)
        for j in range(N_PAIRS):
            lanes = slice(PAIR_W * j, PAIR_W * (j + 1))
            qs = _split_pair(q_scr[rows, lanes], lo)
            s = _dot(qs, kt_scr[b, lanes, :])
            p = jnp.exp2((s - jnp.max(s, axis=-1, keepdims=True)).astype(BF16))
            o2 = _dot_nt(p, jnp.concatenate([vt_scr[b, lanes, :], ones], axis=0))
            o = o2[:, 0:PAIR_W] / o2[:, PAIR_W:2 * PAIR_W]
            att_scr[rows, lanes] = jnp.where(lo, o[:SEQ], o[SEQ:]).astype(BF16)

    mix = _dot(att_scr[...], wout_ref[...])
    o_ref[...] = x + gt * _rms(mix, g_ref[1:2, :])


def _prompt_attn(x, mod, row, g, wqkv, wout, n_seq=2):
    t = x.shape[0]
    tm = n_seq * SEQ
    tok = pl.BlockSpec((tm, D_MODEL), lambda i: (i, 0))
    out = jax.ShapeDtypeStruct((t, D_MODEL), F32)
    cache = pl.BlockSpec((n_seq, None, N_HEADS, HEAD_DIM, SEQ), lambda i: (i, 0, 0, 0, 0))
    cache_out = jax.ShapeDtypeStruct((t // SEQ, 1, N_HEADS, HEAD_DIM, SEQ), F32)

    def qkv_part(n):
        return pl.BlockSpec((D_MODEL, D_MODEL), lambda i: (0, n), pipeline_mode=pl.Buffered(1))

    return pl.pallas_call(
        functools.partial(_prompt_attn_kernel, n_seq=n_seq),
        grid=(t // tm,),
        in_specs=[tok, _mod_spec(lambda i: row), _const_spec((4, D_MODEL)),
                  qkv_part(0), qkv_part(1), qkv_part(2), _const_spec((D_MODEL, D_MODEL))],
        out_specs=[tok, cache, cache],
        out_shape=[out, cache_out, cache_out],
        scratch_shapes=[pltpu.VMEM((D_MODEL, D_MODEL), BF16), pltpu.VMEM((D_MODEL, D_MODEL), BF16),
                        pltpu.VMEM((tm, D_MODEL), BF16), pltpu.VMEM((tm, D_MODEL), BF16),
                        pltpu.VMEM((n_seq, D_MODEL, SEQ), BF16),
                        pltpu.VMEM((n_seq, D_MODEL, SEQ), BF16),
                        pltpu.VMEM((tm, D_MODEL), BF16)],
        compiler_params=_params(1),
        name="prompt_attn",
    )(x, mod, g, wqkv, wqkv, wqkv, wout)


def _premix_qkv_kernel(x_ref, mod_ref, g_ref, wqkv_ref, q_ref, k_ref, v_ref):
    x = x_ref[...]
    m = mod_ref[...]
    sh = m[:, 0:D_MODEL]
    sc = m[:, D_MODEL:2 * D_MODEL]
    h = (_rms(x, g_ref[0:1, :]) * (1.0 + sc) + sh).astype(BF16)
    qkv = _dot(h, wqkv_ref[...])
    q_ref[...] = (qkv[:, 0:D_MODEL] * Q_SCALE).astype(BF16)
    k_ref[...] = qkv[:, D_MODEL:2 * D_MODEL].astype(BF16)
    _store_values_with_ones(v_ref, qkv[:, 2 * D_MODEL:3 * D_MODEL].astype(BF16))


def _store_values_with_ones(v_ref, v):
    ones = jnp.ones((v.shape[0], PAIR_W), BF16)
    for j in range(N_PAIRS):
        v_ref[:, 2 * PAIR_W * j:2 * PAIR_W * j + PAIR_W] = v[:, PAIR_W * j:PAIR_W * (j + 1)]
        v_ref[:, 2 * PAIR_W * j + PAIR_W:2 * PAIR_W * (j + 1)] = ones


def _premix_qkv(x, mod, row_of_step, g, wqkv, tm=512):
    t = x.shape[0]
    tok = pl.BlockSpec((tm, D_MODEL), lambda i: (i, 0))
    tok2 = pl.BlockSpec((tm, 2 * D_MODEL), lambda i: (i, 0))
    out = jax.ShapeDtypeStruct((t, D_MODEL), BF16)
    out2 = jax.ShapeDtypeStruct((t, 2 * D_MODEL), BF16)
    return pl.pallas_call(
        _premix_qkv_kernel,
        grid=(t // tm,),
        in_specs=[tok, _mod_spec(row_of_step), _const_spec((4, D_MODEL)),
                  _const_spec((D_MODEL, 3 * D_MODEL))],
        out_specs=[tok, tok, tok2],
        out_shape=[out, out, out2],
        compiler_params=_params(1),
        name="premix_qkv",
    )(x, mod, g, wqkv)


def _ctx_prep_kernel(ckt_ref, cvt_ref, k_ref, v_ref):
    k_ref[...] = ckt_ref[...].astype(BF16)
    _store_values_with_ones(v_ref, cvt_ref[...].T.astype(BF16))


def _ctx_prep(cache_kt, cache_vt, layer_j):
    cache = pl.BlockSpec((None, None, D_MODEL, PAST_LEN), lambda b: (b, layer_j, 0, 0))
    return pl.pallas_call(
        _ctx_prep_kernel,
        grid=(DEC_BATCH,),
        in_specs=[cache, cache],
        out_specs=[pl.BlockSpec((D_MODEL, PAST_LEN), lambda b: (b, 0)),
                   pl.BlockSpec((PAST_LEN, 2 * D_MODEL), lambda b: (b, 0))],
        out_shape=[jax.ShapeDtypeStruct((DEC_BATCH * D_MODEL, PAST_LEN), BF16),
                   jax.ShapeDtypeStruct((DEC_BATCH * PAST_LEN, 2 * D_MODEL), BF16)],
        compiler_params=_params(1),
        name="ctx_prep",
    )(cache_kt, cache_vt)


def _build_bias_table(l_ref, bias_ref):
    lane = lax.broadcasted_iota(jnp.int32, (GRID_W, PAIR_W), 1)
    qcol = lax.broadcasted_iota(jnp.int32, (GRID_W, PAIR_W), 0)
    kcol = lane & (GRID_W - 1)
    start = jnp.clip(qcol - WIN_COLS // 2, 0, GRID_W - WIN_COLS)
    in_window = (kcol >= start) & (kcol < start + WIN_COLS)
    lo = lane < GRID_W

    def one_head(h, carry):
        def toeplitz(d, shift):
            row = jnp.broadcast_to(l_ref[h, d:d + 1, :], (GRID_W, PAIR_W))
            return pltpu.roll(row, shift, 1, stride=1, stride_axis=0)

        for d in range(N_DR_PAIRS):
            both = jnp.where(lo, toeplitz(d, GRID_W + 1), toeplitz(d + 1, 1))
            bias_ref[h * N_DR_PAIRS + d] = jnp.where(in_window, both * LOG2E, NEG_INF)
        return carry

    lax.fori_loop(0, N_HEADS, one_head, 0)


def _na_attn_kernel(q_ref, k_ref, v_ref, ckt_ref, cv_ref, l_ref, x_ref, mod_ref, g_ref,
                    wout_ref, o_ref, bias_ref, s_scr, m_scr, p_scr, att_scr, *, rows_per_step):
    @pl.when((pl.program_id(0) == 0) & (pl.program_id(1) == 0))
    def _():
        _build_bias_table(l_ref, bias_ref)

    blk = pl.program_id(1)
    lo = lax.broadcasted_iota(jnp.int32, (GRID_W, PAIR_W), 1) < HEAD_DIM

    def row_geometry(i):
        r = blk * rows_per_step + i
        rs = jnp.clip(r - WIN_ROWS // 2, 0, GRID_ROWS - WIN_ROWS)
        d0 = rs - r + (WIN_ROWS - 1)
        q0 = pl.multiple_of(i * GRID_W, GRID_W)
        k0 = pl.multiple_of(rs * GRID_W, GRID_W)
        return d0, q0, k0

    def scores(i, slot):
        d0, q0, k0 = row_geometry(i)
        for j in range(N_PAIRS):
            lanes = slice(PAIR_W * j, PAIR_W * (j + 1))
            qs = _split_pair(q_ref[pl.ds(q0, GRID_W), lanes], lo)
            bias = jnp.concatenate(
                [jnp.concatenate(
                    [bias_ref[(2 * j) * N_DR_PAIRS + d0 + 2 * jj],
                     bias_ref[(2 * j + 1) * N_DR_PAIRS + d0 + 2 * jj]], axis=0)
                 for jj in range(WIN_ROWS // 2)], axis=1)
            s_loc = _dot_nt(qs, k_ref[pl.ds(k0, N_LOCAL), lanes]) + bias
            s_ctx = _dot(qs, ckt_ref[lanes, :])
            mx = jnp.maximum(jnp.max(s_loc, axis=-1, keepdims=True),
                             jnp.max(s_ctx, axis=-1, keepdims=True))
            s_scr[slot, j, :, 0:N_LOCAL] = s_loc
            s_scr[slot, j, :, N_LOCAL:N_KEYS] = s_ctx
            m_scr[slot, j] = jnp.broadcast_to(mx, (2 * GRID_W, PAIR_W))

    def probs(slot):
        for j in range(N_PAIRS):
            mx = m_scr[slot, j][:, 0:1]
            p_scr[slot, j] = jnp.exp2((s_scr[slot, j] - mx).astype(BF16))

    def values(i, slot):
        _, q0, k0 = row_geometry(i)
        for j in range(N_PAIRS):
            lanes2 = slice(2 * PAIR_W * j, 2 * PAIR_W * (j + 1))
            p = p_scr[slot, j]
            o2 = (_dot(p[:, 0:N_LOCAL], v_ref[pl.ds(k0, N_LOCAL), lanes2])
                  + _dot(p[:, N_LOCAL:N_KEYS], cv_ref[:, lanes2]))
            o = o2[:, 0:PAIR_W] / o2[:, PAIR_W:2 * PAIR_W]
            att_scr[pl.ds(q0, GRID_W), PAIR_W * j:PAIR_W * (j + 1)] = (
                jnp.where(lo, o[:GRID_W], o[GRID_W:]).astype(BF16))

    scores(0, 0)
    probs(0)
    scores(1, 1)

    def two_rows(t, carry):
        i = 2 * t
        values(i - 2, 0)
        probs(1)
        scores(i, 0)
        values(i - 1, 1)
        probs(0)
        scores(i + 1, 1)
        return carry

    lax.fori_loop(1, rows_per_step // 2, two_rows, 0)
    values(rows_per_step - 2, 0)
    probs(1)
    values(rows_per_step - 1, 1)

    gt = mod_ref[...][:, 2 * D_MODEL:3 * D_MODEL]
    mix = _dot(att_scr[...], wout_ref[...])
    o_ref[...] = x_ref[...] + gt * _rms(mix, g_ref[1:2, :])


def _na_attn(q, k, v, ck, cv, rpb, x, mod, row_of_batch, g, wout, rows_per_step=8):
    rpb_rows = jnp.pad(rpb, ((0, 0), (0, 0), (48, 49)), mode="edge")
    tm = rows_per_step * GRID_W
    n_t = DEC_SEQ // tm
    tok = pl.BlockSpec((tm, D_MODEL), lambda b, t: (b * n_t + t, 0))

    def per_batch(rows, width):
        return pl.BlockSpec((rows, width), lambda b, t: (b, 0), pipeline_mode=pl.Buffered(1))

    return pl.pallas_call(
        functools.partial(_na_attn_kernel, rows_per_step=rows_per_step),
        grid=(DEC_BATCH, n_t),
        in_specs=[
            tok,
            per_batch(DEC_SEQ, D_MODEL), per_batch(DEC_SEQ, 2 * D_MODEL),
            per_batch(D_MODEL, PAST_LEN), per_batch(PAST_LEN, 2 * D_MODEL),
            _const_spec((N_HEADS, N_DR, PAIR_W)),
            tok,
            _mod_spec(lambda b, t: row_of_batch(b)),
            _const_spec((4, D_MODEL)),
            _const_spec((D_MODEL, D_MODEL)),
        ],
        out_specs=tok,
        out_shape=jax.ShapeDtypeStruct((DEC_BATCH * DEC_SEQ, D_MODEL), F32),
        scratch_shapes=[
            pltpu.VMEM((N_HEADS * N_DR_PAIRS, GRID_W, PAIR_W), F32),
            pltpu.VMEM((2, N_PAIRS, 2 * GRID_W, N_KEYS), F32),
            pltpu.VMEM((2, N_PAIRS, 2 * GRID_W, PAIR_W), F32),
            pltpu.VMEM((2, N_PAIRS, 2 * GRID_W, N_KEYS), BF16),
            pltpu.VMEM((tm, D_MODEL), BF16),
        ],
        compiler_params=_params(2),
        name="na_attn",
    )(q, k, v, ck, cv, rpb_rows, x, mod, g, wout)


def kernel(x_prompt, x_sample, c, cache_k, cache_v, c_ctx, ada_w, ada_b, norm_g, fourier_w_out,
           na_w_qkv, na_rpb, na_w_out, ffn_w_gate, ffn_w_up, ffn_w_down):
    n_p = BATCH * SEQ
    n_s = DEC_BATCH * DEC_SEQ
    xp = x_prompt.reshape(n_p, D_MODEL)
    xs = x_sample.reshape(n_s, D_MODEL)

    cond = jnp.concatenate(
        [c_ctx[None, :], c, jnp.zeros((COND_ROWS - 1 - DEC_BATCH, D_MODEL), F32)], axis=0)
    mod = _modulation(cond, ada_w, ada_b).reshape(DEPTH * COND_ROWS, 1, 6 * D_MODEL)

    cos_g, sin_g = _dft_cos_sin(FOURIER_GROUP)
    cs_chan = jnp.asarray(np.concatenate([cos_g, sin_g], axis=1), F32).astype(BF16)
    cos_p, sin_p = (jnp.asarray(a, F32).astype(BF16) for a in _dft_cos_sin(SEQ))
    cos_s, sin_s = (jnp.asarray(a, F32).astype(BF16) for a in _dft_cos_sin(DEC_SEQ))

    tm = 512
    ffn_tm = 1024
    ffn_weights = (ffn_w_gate, ffn_w_up, ffn_w_down)
    cache_kt = jnp.transpose(cache_k, (0, 1, 3, 4, 2)).reshape(DEC_BATCH, -1, D_MODEL, PAST_LEN)
    cache_vt = jnp.transpose(cache_v, (0, 1, 3, 4, 2)).reshape(DEC_BATCH, -1, D_MODEL, PAST_LEN)
    new_kt = new_vt = None
    attn_weights = None
    for layer in range(DEPTH):
        g = norm_g[layer]
        base = layer * COND_ROWS
        prompt_row = lambda *_, base=base: base
        sample_row_of_batch = lambda b, base=base: base + 1 + b
        sample_row_of_tile = lambda i, base=base: base + 1 + i // (DEC_SEQ // tm)
        sample_row_of_ffn_tile = lambda i, base=base: base + 1 + i // (DEC_SEQ // ffn_tm)

        if layer % 2 == 0:
            w_out = fourier_w_out[layer // 2]
            jobs = [(w, layer) for w in ffn_weights] if layer == 0 else []
            xp, cast_p = _fourier_mixer(cos_p, sin_p, cs_chan, xp, mod, prompt_row, g, w_out,
                                        BATCH, SEQ, n_sub=4, tr=SEQ, cast_jobs=jobs[:2])
            xs, cast_s = _fourier_mixer(cos_s, sin_s, cs_chan, xs, mod, sample_row_of_batch, g,
                                        w_out, DEC_BATCH, DEC_SEQ, n_sub=1, tr=512,
                                        cast_jobs=jobs[2:])
            if layer == 0:
                wg, wu, wd = list(cast_p) + list(cast_s)
        else:
            j = layer // 2
            if attn_weights is None:
                attn_weights = (na_w_qkv[j].astype(BF16), na_w_out[j].astype(BF16))
            w_qkv, w_out = attn_weights
            attn_weights = None
            xp, new_kt, new_vt = _prompt_attn(xp, mod, base, g, w_qkv, w_out)
            q, k, v = _premix_qkv(xs, mod, sample_row_of_tile, g, w_qkv, tm)
            ckt, cvt = _ctx_prep(cache_kt, cache_vt, j)
            xs = _na_attn(q, k, v, ckt, cvt, na_rpb[j], xs, mod, sample_row_of_batch, g, w_out)

        nxt = layer + 1
        ffn_jobs = [(w, nxt) for w in ffn_weights] if nxt < DEPTH else []
        attn_jobs = ([(na_w_qkv, nxt // 2), (na_w_out, nxt // 2)]
                     if nxt < DEPTH and nxt % 2 == 1 else [])
        xp, cast_ffn = _ffn(xp, mod, prompt_row, g, wg, wu, wd, tm, cast_jobs=ffn_jobs)
        xs, cast_attn = _ffn(xs, mod, sample_row_of_tile if attn_jobs else sample_row_of_ffn_tile,
                             g, wg, wu, wd, tm if attn_jobs else ffn_tm, cast_jobs=attn_jobs)
        if ffn_jobs:
            wg, wu, wd = cast_ffn
        if attn_jobs:
            attn_weights = tuple(cast_attn)

    new_k = jnp.transpose(new_kt, (0, 1, 4, 2, 3))
    new_v = jnp.transpose(new_vt, (0, 1, 4, 2, 3))
    return (xp.reshape(BATCH, SEQ, D_MODEL), xs.reshape(DEC_BATCH, DEC_SEQ, D_MODEL), new_k, new_v)
```

```python
import functools

import numpy as np
import jax
import jax.numpy as jnp
from jax import lax
from jax.experimental import pallas as pl
from jax.experimental.pallas import tpu as pltpu

D_MODEL = 1024
BATCH = 32
SEQ = 256
DEPTH = 2
DEC_BATCH = 2
DEC_SEQ = 2048
PAST_LEN = 512
GRID_W = 64
GRID_ROWS = DEC_SEQ // GRID_W
N_HEADS = 16
HEAD_DIM = D_MODEL // N_HEADS
N_FOURIER_GROUPS = 4
FOURIER_GROUP = D_MODEL // N_FOURIER_GROUPS
WIN_ROWS = 8
WIN_COLS = 16
D_FF = 2816
EPS = 1e-6
NEG_INF = -1e30

N_PAIRS = N_HEADS // 2
PAIR_W = 2 * HEAD_DIM
N_DR = 2 * WIN_ROWS - 1
N_DR_PAIRS = N_DR - 1
N_LOCAL = WIN_ROWS * GRID_W
N_KEYS = N_LOCAL + PAST_LEN
COND_ROWS = 8
LOG2E = 1.4426950408889634
Q_SCALE = HEAD_DIM ** -0.5 * LOG2E

VMEM_LIMIT = 56 * 1024 * 1024
BF16_SUBLANES = 16

F32 = jnp.float32
BF16 = jnp.bfloat16


def _dot(a, b):
    return jnp.dot(a, b, preferred_element_type=F32)


def _dot_nt(a, b):
    return lax.dot_general(a, b, (((1,), (1,)), ((), ())), preferred_element_type=F32)


def _rms(x, g):
    ms = jnp.mean(x * x, axis=-1, keepdims=True)
    return x * lax.rsqrt(ms + EPS) * g


def _silu(x):
    return x / (1.0 + jnp.exp(-x))


def _const_spec(shape):
    return pl.BlockSpec(shape, lambda *_: (0,) * len(shape), pipeline_mode=pl.Buffered(1))


def _mod_spec(row_of_step):
    return pl.BlockSpec((None, 1, 6 * D_MODEL), lambda *idx: (row_of_step(*idx), 0, 0))


def _params(n_axes):
    return pltpu.CompilerParams(
        dimension_semantics=("arbitrary",) * n_axes, vmem_limit_bytes=VMEM_LIMIT)


def _mod_kernel(cond_ref, w_ref, b_ref, o_ref):
    @pl.when(pl.program_id(1) == 0)
    def _():
        o_ref[...] = jnp.broadcast_to(b_ref[...], o_ref.shape)

    a = _silu(cond_ref[...])
    a_hi = a.astype(BF16)
    a_lo = (a - a_hi.astype(F32)).astype(BF16)
    w = w_ref[...].astype(BF16)
    o_ref[...] += _dot(a_hi, w) + _dot(a_lo, w)


def _modulation(cond, ada_w, ada_b):
    tk = 256
    n_out = 6 * D_MODEL
    return pl.pallas_call(
        _mod_kernel,
        grid=(DEPTH, D_MODEL // tk),
        in_specs=[
            pl.BlockSpec((COND_ROWS, tk), lambda l, k: (0, k)),
            pl.BlockSpec((None, tk, n_out), lambda l, k: (l, k, 0)),
            pl.BlockSpec((None, 1, n_out), lambda l, k: (l, 0, 0)),
        ],
        out_specs=pl.BlockSpec((None, COND_ROWS, n_out), lambda l, k: (l, 0, 0)),
        out_shape=jax.ShapeDtypeStruct((DEPTH, COND_ROWS, n_out), F32),
        compiler_params=_params(2),
        name="modulation",
    )(cond, ada_w, ada_b.reshape(DEPTH, 1, n_out))


FFN_SUB_ROWS = 256


def _cast_specs(jobs, n_steps, step_of):
    in_specs, out_specs, out_shapes = [], [], []
    for src, layer in jobs:
        rows, cols = src.shape[1:]
        chunk = rows // n_steps
        assert chunk * n_steps == rows and chunk % BF16_SUBLANES == 0
        in_specs.append(pl.BlockSpec(
            (None, chunk, cols), lambda *idx, layer=layer: (layer, step_of(*idx), 0)))
        out_specs.append(pl.BlockSpec((chunk, cols), lambda *idx: (step_of(*idx), 0)))
        out_shapes.append(jax.ShapeDtypeStruct((rows, cols), BF16))
    return in_specs, out_specs, out_shapes


def _run_cast_jobs(src_refs, dst_refs):
    for src, dst in zip(src_refs, dst_refs):
        dst[...] = src[...].astype(BF16)


def _ffn_kernel(x_ref, mod_ref, g_ref, wg_ref, wu_ref, wd_ref, *rest, n_cast):
    o_ref = rest[n_cast]
    _run_cast_jobs(rest[:n_cast], rest[n_cast + 1:])
    m = mod_ref[...]
    sh = m[:, 3 * D_MODEL:4 * D_MODEL]
    sc = m[:, 4 * D_MODEL:5 * D_MODEL]
    gt = m[:, 5 * D_MODEL:6 * D_MODEL]
    for r0 in range(0, x_ref.shape[0], FFN_SUB_ROWS):
        rows = slice(r0, r0 + FFN_SUB_ROWS)
        x = x_ref[rows, :]
        f = (_rms(x, g_ref[2:3, :]) * (1.0 + sc) + sh).astype(BF16)
        gate = _dot(f, wg_ref[...])
        up = _dot(f, wu_ref[...])
        act = (_silu(gate) * up).astype(BF16)
        y = _dot(act, wd_ref[...])
        o_ref[rows, :] = x + gt * _rms(y, g_ref[3:4, :])


def _ffn(x, mod, row_of_step, g, wg, wu, wd, tm, cast_jobs=()):
    t = x.shape[0]
    tok = pl.BlockSpec((tm, D_MODEL), lambda i: (i, 0))
    cast_in, cast_out, cast_shapes = _cast_specs(cast_jobs, t // tm, lambda i: i)
    outs = pl.pallas_call(
        functools.partial(_ffn_kernel, n_cast=len(cast_jobs)),
        grid=(t // tm,),
        in_specs=[
            tok,
            _mod_spec(row_of_step),
            _const_spec((4, D_MODEL)),
            _const_spec((D_MODEL, D_FF)),
            _const_spec((D_MODEL, D_FF)),
            _const_spec((D_FF, D_MODEL)),
        ] + cast_in,
        out_specs=[tok] + cast_out,
        out_shape=[jax.ShapeDtypeStruct((t, D_MODEL), F32)] + cast_shapes,
        compiler_params=_params(1),
        name="ffn",
    )(x, mod, g, wg, wu, wd, *[src for src, _ in cast_jobs])
    return outs[0], outs[1:]


def _dft_cos_sin(n):
    j = np.arange(n)
    ang = 2.0 * np.pi * ((j[:, None] * j[None, :]) % n) / n
    return np.cos(ang), np.sin(ang)


PREMIX_CHUNK = 512


def _premix_channel_dft(x_ref, n_rows, sc, sh, g_ref, cs_ref, xc_scr, xs_scr):
    cs = cs_ref[...]
    for r0 in range(0, n_rows, PREMIX_CHUNK):
        rows = slice(r0, r0 + PREMIX_CHUNK)
        h = (_rms(x_ref[rows, :], g_ref[0:1, :]) * (1.0 + sc) + sh).astype(BF16)
        for j in range(N_FOURIER_GROUPS):
            lanes = slice(FOURIER_GROUP * j, FOURIER_GROUP * (j + 1))
            r = _dot(h[:, lanes], cs)
            xc_scr[rows, lanes] = r[:, :FOURIER_GROUP].astype(BF16)
            xs_scr[rows, lanes] = r[:, FOURIER_GROUP:].astype(BF16)


def _fourier_kernel(c_ref, s_ref, cs_ref, x_ref, mod_ref, g_ref, w_ref, *rest,
                    n_sub, seq, tr, n_cast):
    o_ref = rest[n_cast]
    xc_scr, xs_scr, f_scr, w_scr = rest[2 * n_cast + 1:]
    _run_cast_jobs(rest[:n_cast], rest[n_cast + 1:2 * n_cast + 1])
    t = pl.program_id(1)
    m = mod_ref[...]
    sh = m[:, 0:D_MODEL]
    sc = m[:, D_MODEL:2 * D_MODEL]
    gt = m[:, 2 * D_MODEL:3 * D_MODEL]

    @pl.when((pl.program_id(0) == 0) & (t == 0))
    def _():
        w_scr[...] = w_ref[...].astype(BF16)

    @pl.when(t == 0)
    def _():
        _premix_channel_dft(x_ref, n_sub * seq, sc, sh, g_ref, cs_ref, xc_scr, xs_scr)

    for b in range(n_sub):
        src = slice(seq * b, seq * (b + 1))
        f = _dot(c_ref[...], xc_scr[src, :]) - _dot(s_ref[...], xs_scr[src, :])
        f_scr[tr * b:tr * (b + 1), :] = f.astype(BF16)
    mix = _dot(f_scr[...], w_scr[...])
    if n_sub == 1:
        x = x_ref[pl.ds(pl.multiple_of(t * tr, tr), tr), :]
    else:
        x = x_ref[...]
    o_ref[...] = x + gt * _rms(mix, g_ref[1:2, :])


def _fourier_mixer(cmat, smat, cs, x, mod, row_of_group, g, w, n_seqs, seq, n_sub, tr,
                   cast_jobs=()):
    assert n_sub == 1 or tr == seq
    n_t = seq // tr
    n_groups = n_seqs // n_sub
    rows = n_sub * seq
    cast_in, cast_out, cast_shapes = _cast_specs(
        cast_jobs, n_groups * n_t, lambda b, t: b * n_t + t)
    outs = pl.pallas_call(
        functools.partial(_fourier_kernel, n_sub=n_sub, seq=seq, tr=tr, n_cast=len(cast_jobs)),
        grid=(n_groups, n_t),
        in_specs=[
            pl.BlockSpec((tr, seq), lambda b, t: (t, 0)),
            pl.BlockSpec((tr, seq), lambda b, t: (t, 0)),
            _const_spec((FOURIER_GROUP, 2 * FOURIER_GROUP)),
            pl.BlockSpec((rows, D_MODEL), lambda b, t: (b, 0)),
            _mod_spec(lambda b, t: row_of_group(b)),
            _const_spec((4, D_MODEL)),
            _const_spec((D_MODEL, D_MODEL)),
        ] + cast_in,
        out_specs=[pl.BlockSpec((n_sub * tr, D_MODEL), lambda b, t: (b * n_t + t, 0))] + cast_out,
        out_shape=[jax.ShapeDtypeStruct((n_seqs * seq, D_MODEL), F32)] + cast_shapes,
        scratch_shapes=[pltpu.VMEM((rows, D_MODEL), BF16), pltpu.VMEM((rows, D_MODEL), BF16),
                        pltpu.VMEM((n_sub * tr, D_MODEL), BF16),
                        pltpu.VMEM((D_MODEL, D_MODEL), BF16)],
        compiler_params=_params(2),
        name="fourier_mixer",
    )(cmat, smat, cs, x, mod, g, w, *[src for src, _ in cast_jobs])
    return outs[0], outs[1:]


HERM_TILE = 512
HERM_ROWS = HERM_TILE + BF16_SUBLANES


def _half_spectrum_blocks(mat):
    n_half = mat.shape[0] // 2 // HERM_TILE
    return np.stack([mat[HERM_TILE * t:HERM_TILE * t + HERM_ROWS] for t in range(n_half)])


def _reversal_matrix():
    rev = np.zeros((HERM_TILE, HERM_ROWS), np.float32)
    rev[np.arange(HERM_TILE), HERM_TILE - np.arange(HERM_TILE)] = 1.0
    return rev


def _fourier_long_kernel(ch_ref, sh_ref, rev_ref, cs_ref, x_ref, mod_ref, g_ref, w_ref, *rest,
                         seq, n_cast):
    o_ref = rest[n_cast]
    xc_scr, xs_scr, pq_scr, f_scr, w_scr = rest[2 * n_cast + 1:]
    _run_cast_jobs(rest[:n_cast], rest[n_cast + 1:2 * n_cast + 1])
    b, t, u = pl.program_id(0), pl.program_id(1), pl.program_id(2)
    m = mod_ref[...]
    sh = m[:, 0:D_MODEL]
    sc = m[:, D_MODEL:2 * D_MODEL]
    gt = m[:, 2 * D_MODEL:3 * D_MODEL]

    @pl.when((b == 0) & (t == 0) & (u == 0))
    def _():
        w_scr[...] = w_ref[...].astype(BF16)

    @pl.when((t == 0) & (u == 0))
    def _():
        _premix_channel_dft(x_ref, seq, sc, sh, g_ref, cs_ref, xc_scr, xs_scr)

    @pl.when(u == 0)
    def _():
        p = _dot(ch_ref[...], xc_scr[...])
        q = _dot(sh_ref[...], xs_scr[...])
        f_scr[...] = (p - q)[0:HERM_TILE].astype(BF16)
        pq_scr[...] = (p + q).astype(BF16)

    @pl.when(u == 1)
    def _():
        f_scr[...] = _dot(rev_ref[...], pq_scr[...]).astype(BF16)

    n_tiles = seq // HERM_TILE
    tile = t + u * (n_tiles - 1 - 2 * t)
    mix = _dot(f_scr[...], w_scr[...])
    x = x_ref[pl.ds(pl.multiple_of(tile * HERM_TILE, HERM_TILE), HERM_TILE), :]
    o_ref[...] = x + gt * _rms(mix, g_ref[1:2, :])


def _fourier_mixer_long(cos_h, sin_h, rev, cs, x, mod, row_of_seq, g, w, n_seqs, seq, cast_jobs=()):
    n_tiles = seq // HERM_TILE
    n_half = n_tiles // 2
    tile_of = lambda t, u: t + u * (n_tiles - 1 - 2 * t)
    half = pl.BlockSpec((None, HERM_ROWS, seq), lambda b, t, u: (t, 0, 0))
    cast_in, cast_out, cast_shapes = _cast_specs(
        cast_jobs, n_seqs * n_tiles, lambda b, t, u: (b * n_half + t) * 2 + u)
    outs = pl.pallas_call(
        functools.partial(_fourier_long_kernel, seq=seq, n_cast=len(cast_jobs)),
        grid=(n_seqs, n_half, 2),
        in_specs=[
            half, half,
            _const_spec((HERM_TILE, HERM_ROWS)),
            _const_spec((FOURIER_GROUP, 2 * FOURIER_GROUP)),
            pl.BlockSpec((seq, D_MODEL), lambda b, t, u: (b, 0)),
            _mod_spec(lambda b, t, u: row_of_seq(b)),
            _const_spec((4, D_MODEL)),
            _const_spec((D_MODEL, D_MODEL)),
        ] + cast_in,
        out_specs=[pl.BlockSpec((HERM_TILE, D_MODEL),
                                lambda b, t, u: (b * n_tiles + tile_of(t, u), 0))] + cast_out,
        out_shape=[jax.ShapeDtypeStruct((n_seqs * seq, D_MODEL), F32)] + cast_shapes,
        scratch_shapes=[pltpu.VMEM((seq, D_MODEL), BF16), pltpu.VMEM((seq, D_MODEL), BF16),
                        pltpu.VMEM((HERM_ROWS, D_MODEL), BF16),
                        pltpu.VMEM((HERM_TILE, D_MODEL), BF16),
                        pltpu.VMEM((D_MODEL, D_MODEL), BF16)],
        compiler_params=_params(3),
        name="fourier_mixer_long",
    )(cos_h, sin_h, rev, cs, x, mod, g, w, *[src for src, _ in cast_jobs])
    return outs[0], outs[1:]


def _split_pair(q, lo):
    zero = jnp.zeros_like(q)
    return jnp.concatenate([jnp.where(lo, q, zero), jnp.where(lo, zero, q)], axis=0)


def _prompt_attn_kernel(x_ref, mod_ref, g_ref, wq_ref, wk_ref, wv_ref, wout_ref,
                        o_ref, kt_ref, vt_ref, wkt_ref, wvt_ref, h_scr, q_scr, kt_scr, vt_scr,
                        att_scr, *, n_seq):
    @pl.when(pl.program_id(0) == 0)
    def _():
        wkt_ref[...] = wk_ref[...].T
        wvt_ref[...] = wv_ref[...].T

    x = x_ref[...]
    m = mod_ref[...]
    sh = m[:, 0:D_MODEL]
    sc = m[:, D_MODEL:2 * D_MODEL]
    gt = m[:, 2 * D_MODEL:3 * D_MODEL]
    h_scr[...] = (_rms(x, g_ref[0:1, :]) * (1.0 + sc) + sh).astype(BF16)
    q_scr[...] = (_dot(h_scr[...], wq_ref[...]) * Q_SCALE).astype(BF16)

    lo = lax.broadcasted_iota(jnp.int32, (SEQ, PAIR_W), 1) < HEAD_DIM
    ones = jnp.ones((PAIR_W, SEQ), BF16)
    for b in range(n_seq):
        rows = slice(SEQ * b, SEQ * (b + 1))
        kt = _dot_nt(wkt_ref[...], h_scr[rows, :])
        vt = _dot_nt(wvt_ref[...], h_scr[rows, :])
        kt_ref[b] = kt.reshape(N_HEADS, HEAD_DIM, SEQ)
        vt_ref[b] = vt.reshape(N_HEADS, HEAD_DIM, SEQ)
        kt_scr[b] = kt.astype(BF16)
        vt_scr[b] = vt.astype(BF16)
        for j in range(N_PAIRS):
            lanes = slice(PAIR_W * j, PAIR_W * (j + 1))
            qs = _split_pair(q_scr[rows, lanes], lo)
            s = _dot(qs, kt_scr[b, lanes, :])
            p = jnp.exp2((s - jnp.max(s, axis=-1, keepdims=True)).astype(BF16))
            o2 = _dot_nt(p, jnp.concatenate([vt_scr[b, lanes, :], ones], axis=0))
            o = o2[:, 0:PAIR_W] / o2[:, PAIR_W:2 * PAIR_W]
            att_scr[rows, lanes] = jnp.where(lo, o[:SEQ], o[SEQ:]).astype(BF16)

    mix = _dot(att_scr[...], wout_ref[...])
    o_ref[...] = x + gt * _rms(mix, g_ref[1:2, :])


def _prompt_attn(x, mod, row, g, wqkv, wout, n_seq=2):
    t = x.shape[0]
    tm = n_seq * SEQ
    tok = pl.BlockSpec((tm, D_MODEL), lambda i: (i, 0))
    out = jax.ShapeDtypeStruct((t, D_MODEL), F32)
    cache = pl.BlockSpec((n_seq, None, N_HEADS, HEAD_DIM, SEQ), lambda i: (i, 0, 0, 0, 0))
    cache_out = jax.ShapeDtypeStruct((t // SEQ, 1, N_HEADS, HEAD_DIM, SEQ), F32)

    def qkv_part(n):
        return pl.BlockSpec((D_MODEL, D_MODEL), lambda i: (0, n), pipeline_mode=pl.Buffered(1))

    return pl.pallas_call(
        functools.partial(_prompt_attn_kernel, n_seq=n_seq),
        grid=(t // tm,),
        in_specs=[tok, _mod_spec(lambda i: row), _const_spec((4, D_MODEL)),
                  qkv_part(0), qkv_part(1), qkv_part(2), _const_spec((D_MODEL, D_MODEL))],
        out_specs=[tok, cache, cache],
        out_shape=[out, cache_out, cache_out],
        scratch_shapes=[pltpu.VMEM((D_MODEL, D_MODEL), BF16), pltpu.VMEM((D_MODEL, D_MODEL), BF16),
                        pltpu.VMEM((tm, D_MODEL), BF16), pltpu.VMEM((tm, D_MODEL), BF16),
                        pltpu.VMEM((n_seq, D_MODEL, SEQ), BF16),
                        pltpu.VMEM((n_seq, D_MODEL, SEQ), BF16),
                        pltpu.VMEM((tm, D_MODEL), BF16)],
        compiler_params=_params(1),
        name="prompt_attn",
    )(x, mod, g, wqkv, wqkv, wqkv, wout)


def _premix_qkv_kernel(x_ref, mod_ref, g_ref, wqkv_ref, q_ref, k_ref, v_ref):
    x = x_ref[...]
    m = mod_ref[...]
    sh = m[:, 0:D_MODEL]
    sc = m[:, D_MODEL:2 * D_MODEL]
    h = (_rms(x, g_ref[0:1, :]) * (1.0 + sc) + sh).astype(BF16)
    qkv = _dot(h, wqkv_ref[...])
    q_ref[...] = (qkv[:, 0:D_MODEL] * Q_SCALE).astype(BF16)
    k_ref[...] = qkv[:, D_MODEL:2 * D_MODEL].astype(BF16)
    _store_values_with_ones(v_ref, qkv[:, 2 * D_MODEL:3 * D_MODEL].astype(BF16))


def _store_values_with_ones(v_ref, v):
    ones = jnp.ones((v.shape[0], PAIR_W), BF16)
    for j in range(N_PAIRS):
        v_ref[:, 2 * PAIR_W * j:2 * PAIR_W * j + PAIR_W] = v[:, PAIR_W * j:PAIR_W * (j + 1)]
        v_ref[:, 2 * PAIR_W * j + PAIR_W:2 * PAIR_W * (j + 1)] = ones


def _premix_qkv(x, mod, row_of_step, g, wqkv, tm=512):
    t = x.shape[0]
    tok = pl.BlockSpec((tm, D_MODEL), lambda i: (i, 0))
    tok2 = pl.BlockSpec((tm, 2 * D_MODEL), lambda i: (i, 0))
    out = jax.ShapeDtypeStruct((t, D_MODEL), BF16)
    out2 = jax.ShapeDtypeStruct((t, 2 * D_MODEL), BF16)
    return pl.pallas_call(
        _premix_qkv_kernel,
        grid=(t // tm,),
        in_specs=[tok, _mod_spec(row_of_step), _const_spec((4, D_MODEL)),
                  _const_spec((D_MODEL, 3 * D_MODEL))],
        out_specs=[tok, tok, tok2],
        out_shape=[out, out, out2],
        compiler_params=_params(1),
        name="premix_qkv",
    )(x, mod, g, wqkv)


def _ctx_prep_kernel(ckt_ref, cvt_ref, k_ref, v_ref):
    k_ref[...] = ckt_ref[...].astype(BF16)
    _store_values_with_ones(v_ref, cvt_ref[...].T.astype(BF16))


def _ctx_prep(cache_kt, cache_vt, layer_j):
    cache = pl.BlockSpec((None, None, D_MODEL, PAST_LEN), lambda b: (b, layer_j, 0, 0))
    return pl.pallas_call(
        _ctx_prep_kernel,
        grid=(DEC_BATCH,),
        in_specs=[cache, cache],
        out_specs=[pl.BlockSpec((D_MODEL, PAST_LEN), lambda b: (b, 0)),
                   pl.BlockSpec((PAST_LEN, 2 * D_MODEL), lambda b: (b, 0))],
        out_shape=[jax.ShapeDtypeStruct((DEC_BATCH * D_MODEL, PAST_LEN), BF16),
                   jax.ShapeDtypeStruct((DEC_BATCH * PAST_LEN, 2 * D_MODEL), BF16)],
        compiler_params=_params(1),
        name="ctx_prep",
    )(cache_kt, cache_vt)


def _build_bias_table(l_ref, bias_ref):
    lane = lax.broadcasted_iota(jnp.int32, (GRID_W, PAIR_W), 1)
    qcol = lax.broadcasted_iota(jnp.int32, (GRID_W, PAIR_W), 0)
    kcol = lane & (GRID_W - 1)
    start = jnp.clip(qcol - WIN_COLS // 2, 0, GRID_W - WIN_COLS)
    in_window = (kcol >= start) & (kcol < start + WIN_COLS)
    lo = lane < GRID_W

    def one_head(h, carry):
        def toeplitz(d, shift):
            row = jnp.broadcast_to(l_ref[h, d:d + 1, :], (GRID_W, PAIR_W))
            return pltpu.roll(row, shift, 1, stride=1, stride_axis=0)

        for d in range(N_DR_PAIRS):
            both = jnp.where(lo, toeplitz(d, GRID_W + 1), toeplitz(d + 1, 1))
            bias_ref[h * N_DR_PAIRS + d] = jnp.where(in_window, both * LOG2E, NEG_INF)
        return carry

    lax.fori_loop(0, N_HEADS, one_head, 0)


def _na_attn_kernel(q_ref, k_ref, v_ref, ckt_ref, cv_ref, l_ref, x_ref, mod_ref, g_ref,
                    wout_ref, o_ref, bias_ref, s_scr, m_scr, p_scr, att_scr, *, rows_per_step):
    @pl.when((pl.program_id(0) == 0) & (pl.program_id(1) == 0))
    def _():
        _build_bias_table(l_ref, bias_ref)

    blk = pl.program_id(1)
    lo = lax.broadcasted_iota(jnp.int32, (GRID_W, PAIR_W), 1) < HEAD_DIM

    def row_geometry(i):
        r = blk * rows_per_step + i
        rs = jnp.clip(r - WIN_ROWS // 2, 0, GRID_ROWS - WIN_ROWS)
        d0 = rs - r + (WIN_ROWS - 1)
        q0 = pl.multiple_of(i * GRID_W, GRID_W)
        k0 = pl.multiple_of(rs * GRID_W, GRID_W)
        return d0, q0, k0

    def scores(i, slot):
        d0, q0, k0 = row_geometry(i)
        for j in range(N_PAIRS):
            lanes = slice(PAIR_W * j, PAIR_W * (j + 1))
            qs = _split_pair(q_ref[pl.ds(q0, GRID_W), lanes], lo)
            bias = jnp.concatenate(
                [jnp.concatenate(
                    [bias_ref[(2 * j) * N_DR_PAIRS + d0 + 2 * jj],
                     bias_ref[(2 * j + 1) * N_DR_PAIRS + d0 + 2 * jj]], axis=0)
                 for jj in range(WIN_ROWS // 2)], axis=1)
            s_loc = _dot_nt(qs, k_ref[pl.ds(k0, N_LOCAL), lanes]) + bias
            s_ctx = _dot(qs, ckt_ref[lanes, :])
            mx = jnp.maximum(jnp.max(s_loc, axis=-1, keepdims=True),
                             jnp.max(s_ctx, axis=-1, keepdims=True))
            s_scr[slot, j, :, 0:N_LOCAL] = s_loc
            s_scr[slot, j, :, N_LOCAL:N_KEYS] = s_ctx
            m_scr[slot, j] = jnp.broadcast_to(mx, (2 * GRID_W, PAIR_W))

    def probs(slot):
        for j in range(N_PAIRS):
            mx = m_scr[slot, j][:, 0:1]
            p_scr[slot, j] = jnp.exp2((s_scr[slot, j] - mx).astype(BF16))

    def values(i, slot):
        _, q0, k0 = row_geometry(i)
        for j in range(N_PAIRS):
            lanes2 = slice(2 * PAIR_W * j, 2 * PAIR_W * (j + 1))
            p = p_scr[slot, j]
            o2 = (_dot(p[:, 0:N_LOCAL], v_ref[pl.ds(k0, N_LOCAL), lanes2])
                  + _dot(p[:, N_LOCAL:N_KEYS], cv_ref[:, lanes2]))
            o = o2[:, 0:PAIR_W] / o2[:, PAIR_W:2 * PAIR_W]
            att_scr[pl.ds(q0, GRID_W), PAIR_W * j:PAIR_W * (j + 1)] = (
                jnp.where(lo, o[:GRID_W], o[GRID_W:]).astype(BF16))

    scores(0, 0)
    probs(0)
    scores(1, 1)

    def two_rows(t, carry):
        i = 2 * t
        values(i - 2, 0)
        probs(1)
        scores(i, 0)
        values(i - 1, 1)
        probs(0)
        scores(i + 1, 1)
        return carry

    lax.fori_loop(1, rows_per_step // 2, two_rows, 0)
    values(rows_per_step - 2, 0)
    probs(1)
    values(rows_per_step - 1, 1)

    gt = mod_ref[...][:, 2 * D_MODEL:3 * D_MODEL]
    mix = _dot(att_scr[...], wout_ref[...])
    o_ref[...] = x_ref[...] + gt * _rms(mix, g_ref[1:2, :])


def _na_attn(q, k, v, ck, cv, rpb, x, mod, row_of_batch, g, wout, rows_per_step=8):
    rpb_rows = jnp.pad(rpb, ((0, 0), (0, 0), (48, 49)), mode="edge")
    tm = rows_per_step * GRID_W
    n_t = DEC_SEQ // tm
    tok = pl.BlockSpec((tm, D_MODEL), lambda b, t: (b * n_t + t, 0))

    def per_batch(rows, width):
        return pl.BlockSpec((rows, width), lambda b, t: (b, 0), pipeline_mode=pl.Buffered(1))

    return pl.pallas_call(
        functools.partial(_na_attn_kernel, rows_per_step=rows_per_step),
        grid=(DEC_BATCH, n_t),
        in_specs=[
            tok,
            per_batch(DEC_SEQ, D_MODEL), per_batch(DEC_SEQ, 2 * D_MODEL),
            per_batch(D_MODEL, PAST_LEN), per_batch(PAST_LEN, 2 * D_MODEL),
            _const_spec((N_HEADS, N_DR, PAIR_W)),
            tok,
            _mod_spec(lambda b, t: row_of_batch(b)),
            _const_spec((4, D_MODEL)),
            _const_spec((D_MODEL, D_MODEL)),
        ],
        out_specs=tok,
        out_shape=jax.ShapeDtypeStruct((DEC_BATCH * DEC_SEQ, D_MODEL), F32),
        scratch_shapes=[
            pltpu.VMEM((N_HEADS * N_DR_PAIRS, GRID_W, PAIR_W), F32),
            pltpu.VMEM((2, N_PAIRS, 2 * GRID_W, N_KEYS), F32),
            pltpu.VMEM((2, N_PAIRS, 2 * GRID_W, PAIR_W), F32),
            pltpu.VMEM((2, N_PAIRS, 2 * GRID_W, N_KEYS), BF16),
            pltpu.VMEM((tm, D_MODEL), BF16),
        ],
        compiler_params=_params(2),
        name="na_attn",
    )(q, k, v, ck, cv, rpb_rows, x, mod, g, wout)


def kernel(x_prompt, x_sample, c, cache_k, cache_v, c_ctx, ada_w, ada_b, norm_g, fourier_w_out,
           na_w_qkv, na_rpb, na_w_out, ffn_w_gate, ffn_w_up, ffn_w_down):
    n_p = BATCH * SEQ
    n_s = DEC_BATCH * DEC_SEQ
    xp = x_prompt.reshape(n_p, D_MODEL)
    xs = x_sample.reshape(n_s, D_MODEL)

    cond = jnp.concatenate(
        [c_ctx[None, :], c, jnp.zeros((COND_ROWS - 1 - DEC_BATCH, D_MODEL), F32)], axis=0)
    mod = _modulation(cond, ada_w, ada_b).reshape(DEPTH * COND_ROWS, 1, 6 * D_MODEL)

    cos_g, sin_g = _dft_cos_sin(FOURIER_GROUP)
    cs_chan = jnp.asarray(np.concatenate([cos_g, sin_g], axis=1), F32).astype(BF16)
    cos_p, sin_p = (jnp.asarray(a, F32).astype(BF16) for a in _dft_cos_sin(SEQ))
    cos_s, sin_s = (jnp.asarray(_half_spectrum_blocks(a), F32).astype(BF16)
                    for a in _dft_cos_sin(DEC_SEQ))
    rev = jnp.asarray(_reversal_matrix(), F32).astype(BF16)

    tm = 512
    ffn_tm = 1024
    ffn_weights = (ffn_w_gate, ffn_w_up, ffn_w_down)
    cache_kt = jnp.transpose(cache_k, (0, 1, 3, 4, 2)).reshape(DEC_BATCH, -1, D_MODEL, PAST_LEN)
    cache_vt = jnp.transpose(cache_v, (0, 1, 3, 4, 2)).reshape(DEC_BATCH, -1, D_MODEL, PAST_LEN)
    new_kt = new_vt = None
    attn_weights = None
    for layer in range(DEPTH):
        g = norm_g[layer]
        base = layer * COND_ROWS
        prompt_row = lambda *_, base=base: base
        sample_row_of_batch = lambda b, base=base: base + 1 + b
        sample_row_of_tile = lambda i, base=base: base + 1 + i // (DEC_SEQ // tm)
        sample_row_of_ffn_tile = lambda i, base=base: base + 1 + i // (DEC_SEQ // ffn_tm)

        if layer % 2 == 0:
            w_out = fourier_w_out[layer // 2]
            jobs = [(w, layer) for w in ffn_weights] if layer == 0 else []
            xp, cast_p = _fourier_mixer(cos_p, sin_p, cs_chan, xp, mod, prompt_row, g, w_out,
                                        BATCH, SEQ, n_sub=4, tr=SEQ, cast_jobs=jobs[:2])
            xs, cast_s = _fourier_mixer_long(cos_s, sin_s, rev, cs_chan, xs, mod,
                                             sample_row_of_batch, g, w_out, DEC_BATCH, DEC_SEQ,
                                             cast_jobs=jobs[2:])
            if layer == 0:
                wg, wu, wd = list(cast_p) + list(cast_s)
        else:
            j = layer // 2
            if attn_weights is None:
                attn_weights = (na_w_qkv[j].astype(BF16), na_w_out[j].astype(BF16))
            w_qkv, w_out = attn_weights
            attn_weights = None
            xp, new_kt, new_vt = _prompt_attn(xp, mod, base, g, w_qkv, w_out)
            q, k, v = _premix_qkv(xs, mod, sample_row_of_tile, g, w_qkv, tm)
            ckt, cvt = _ctx_prep(cache_kt, cache_vt, j)
            xs = _na_attn(q, k, v, ckt, cvt, na_rpb[j], xs, mod, sample_row_of_batch, g, w_out)

        nxt = layer + 1
        ffn_jobs = [(w, nxt) for w in ffn_weights] if nxt < DEPTH else []
        attn_jobs = ([(na_w_qkv, nxt // 2), (na_w_out, nxt // 2)]
                     if nxt < DEPTH and nxt % 2 == 1 else [])
        xp, cast_ffn = _ffn(xp, mod, prompt_row, g, wg, wu, wd, tm, cast_jobs=ffn_jobs)
        xs, cast_attn = _ffn(xs, mod, sample_row_of_tile if attn_jobs else sample_row_of_ffn_tile,
                             g, wg, wu, wd, tm if attn_jobs else ffn_tm, cast_jobs=attn_jobs)
        if ffn_jobs:
            wg, wu, wd = cast_ffn
        if attn_jobs:
            attn_weights = tuple(cast_attn)

    new_k = jnp.transpose(new_kt, (0, 1, 4, 2, 3))
    new_v = jnp.transpose(new_vt, (0, 1, 4, 2, 3))
    return (xp.reshape(BATCH, SEQ, D_MODEL), xs.reshape(DEC_BATCH, DEC_SEQ, D_MODEL), new_k, new_v)
```

```python
import functools

import numpy as np
import jax
import jax.numpy as jnp
from jax import lax
from jax.experimental import pallas as pl
from jax.experimental.pallas import tpu as pltpu

D_MODEL = 1024
BATCH = 32
SEQ = 256
DEPTH = 2
DEC_BATCH = 2
DEC_SEQ = 2048
PAST_LEN = 512
GRID_W = 64
GRID_ROWS = DEC_SEQ // GRID_W
N_HEADS = 16
HEAD_DIM = D_MODEL // N_HEADS
N_FOURIER_GROUPS = 4
FOURIER_GROUP = D_MODEL // N_FOURIER_GROUPS
WIN_ROWS = 8
WIN_COLS = 16
D_FF = 2816
EPS = 1e-6
NEG_INF = -1e30

N_PAIRS = N_HEADS // 2
PAIR_W = 2 * HEAD_DIM
N_DR = 2 * WIN_ROWS - 1
N_DR_PAIRS = N_DR - 1
N_LOCAL = WIN_ROWS * GRID_W
N_KEYS = N_LOCAL + PAST_LEN
COND_ROWS = 8
LOG2E = 1.4426950408889634
Q_SCALE = HEAD_DIM ** -0.5 * LOG2E

VMEM_LIMIT = 56 * 1024 * 1024
BF16_SUBLANES = 16

F32 = jnp.float32
BF16 = jnp.bfloat16


def _dot(a, b):
    return jnp.dot(a, b, preferred_element_type=F32)


def _dot_nt(a, b):
    return lax.dot_general(a, b, (((1,), (1,)), ((), ())), preferred_element_type=F32)


def _rms(x, g):
    ms = jnp.mean(x * x, axis=-1, keepdims=True)
    return x * lax.rsqrt(ms + EPS) * g


def _silu(x):
    return x / (1.0 + jnp.exp(-x))


def _const_spec(shape):
    return pl.BlockSpec(shape, lambda *_: (0,) * len(shape), pipeline_mode=pl.Buffered(1))


def _mod_spec(row_of_step):
    return pl.BlockSpec((None, 1, 6 * D_MODEL), lambda *idx: (row_of_step(*idx), 0, 0))


def _params(n_axes):
    return pltpu.CompilerParams(
        dimension_semantics=("arbitrary",) * n_axes, vmem_limit_bytes=VMEM_LIMIT)


def _mod_kernel(cond_ref, w_ref, b_ref, o_ref, acc_ref):
    layer, k = pl.program_id(0), pl.program_id(1)

    @pl.when(k == 0)
    def _():
        acc_ref[...] = jnp.broadcast_to(b_ref[pl.ds(layer, 1), :], acc_ref.shape)

    a = _silu(cond_ref[...])
    a_hi = a.astype(BF16)
    a_lo = (a - a_hi.astype(F32)).astype(BF16)
    w = w_ref[...].astype(BF16)
    acc_ref[...] += _dot(a_hi, w) + _dot(a_lo, w)

    @pl.when(k == pl.num_programs(1) - 1)
    def _():
        for r in range(COND_ROWS):
            o_ref[r] = acc_ref[r:r + 1, :]


def _modulation(cond, ada_w, ada_b):
    tk = 256
    n_out = 6 * D_MODEL
    return pl.pallas_call(
        _mod_kernel,
        grid=(DEPTH, D_MODEL // tk),
        in_specs=[
            pl.BlockSpec((COND_ROWS, tk), lambda l, k: (0, k)),
            pl.BlockSpec((None, tk, n_out), lambda l, k: (l, k, 0)),
            pl.BlockSpec((DEPTH, n_out), lambda l, k: (0, 0)),
        ],
        out_specs=pl.BlockSpec((COND_ROWS, 1, n_out), lambda l, k: (l, 0, 0)),
        out_shape=jax.ShapeDtypeStruct((DEPTH * COND_ROWS, 1, n_out), F32),
        scratch_shapes=[pltpu.VMEM((COND_ROWS, n_out), F32)],
        compiler_params=_params(2),
        name="modulation",
    )(cond, ada_w, ada_b)


FFN_SUB_ROWS = 256


def _cast_specs(jobs, n_steps, step_of):
    in_specs, out_specs, out_shapes = [], [], []
    for src, layer in jobs:
        rows, cols = src.shape[1:]
        chunk = rows // n_steps
        assert chunk * n_steps == rows and chunk % BF16_SUBLANES == 0
        in_specs.append(pl.BlockSpec(
            (None, chunk, cols), lambda *idx, layer=layer: (layer, step_of(*idx), 0)))
        out_specs.append(pl.BlockSpec((chunk, cols), lambda *idx: (step_of(*idx), 0)))
        out_shapes.append(jax.ShapeDtypeStruct((rows, cols), BF16))
    return in_specs, out_specs, out_shapes


def _run_cast_jobs(src_refs, dst_refs):
    for src, dst in zip(src_refs, dst_refs):
        dst[...] = src[...].astype(BF16)


def _ffn_kernel(x_ref, mod_ref, g_ref, wg_ref, wu_ref, wd_ref, *rest, n_cast):
    o_ref = rest[n_cast]
    _run_cast_jobs(rest[:n_cast], rest[n_cast + 1:])
    m = mod_ref[...]
    sh = m[:, 3 * D_MODEL:4 * D_MODEL]
    sc = m[:, 4 * D_MODEL:5 * D_MODEL]
    gt = m[:, 5 * D_MODEL:6 * D_MODEL]
    for r0 in range(0, x_ref.shape[0], FFN_SUB_ROWS):
        rows = slice(r0, r0 + FFN_SUB_ROWS)
        x = x_ref[rows, :]
        f = (_rms(x, g_ref[2:3, :]) * (1.0 + sc) + sh).astype(BF16)
        gate = _dot(f, wg_ref[...])
        up = _dot(f, wu_ref[...])
        act = (_silu(gate) * up).astype(BF16)
        y = _dot(act, wd_ref[...])
        o_ref[rows, :] = x + gt * _rms(y, g_ref[3:4, :])


def _ffn(x, mod, row_of_step, g, wg, wu, wd, tm, cast_jobs=()):
    t = x.shape[0]
    tok = pl.BlockSpec((tm, D_MODEL), lambda i: (i, 0))
    cast_in, cast_out, cast_shapes = _cast_specs(cast_jobs, t // tm, lambda i: i)
    outs = pl.pallas_call(
        functools.partial(_ffn_kernel, n_cast=len(cast_jobs)),
        grid=(t // tm,),
        in_specs=[
            tok,
            _mod_spec(row_of_step),
            _const_spec((4, D_MODEL)),
            _const_spec((D_MODEL, D_FF)),
            _const_spec((D_MODEL, D_FF)),
            _const_spec((D_FF, D_MODEL)),
        ] + cast_in,
        out_specs=[tok] + cast_out,
        out_shape=[jax.ShapeDtypeStruct((t, D_MODEL), F32)] + cast_shapes,
        compiler_params=_params(1),
        name="ffn",
    )(x, mod, g, wg, wu, wd, *[src for src, _ in cast_jobs])
    return outs[0], outs[1:]


def _dft_cos_sin(n):
    j = np.arange(n)
    ang = 2.0 * np.pi * ((j[:, None] * j[None, :]) % n) / n
    return np.cos(ang), np.sin(ang)


PREMIX_CHUNK = 512


def _premix_channel_dft(x_ref, n_rows, sc, sh, g_ref, cs_ref, xc_scr, xs_scr):
    cs = cs_ref[...]
    for r0 in range(0, n_rows, PREMIX_CHUNK):
        rows = slice(r0, r0 + PREMIX_CHUNK)
        h = (_rms(x_ref[rows, :], g_ref[0:1, :]) * (1.0 + sc) + sh).astype(BF16)
        for j in range(N_FOURIER_GROUPS):
            lanes = slice(FOURIER_GROUP * j, FOURIER_GROUP * (j + 1))
            r = _dot(h[:, lanes], cs)
            xc_scr[rows, lanes] = r[:, :FOURIER_GROUP].astype(BF16)
            xs_scr[rows, lanes] = r[:, FOURIER_GROUP:].astype(BF16)


def _fourier_kernel(c_ref, s_ref, cs_ref, x_ref, mod_ref, g_ref, w_ref, *rest,
                    n_sub, seq, tr, n_cast):
    o_ref = rest[n_cast]
    xc_scr, xs_scr, f_scr, w_scr = rest[2 * n_cast + 1:]
    _run_cast_jobs(rest[:n_cast], rest[n_cast + 1:2 * n_cast + 1])
    t = pl.program_id(1)
    m = mod_ref[...]
    sh = m[:, 0:D_MODEL]
    sc = m[:, D_MODEL:2 * D_MODEL]
    gt = m[:, 2 * D_MODEL:3 * D_MODEL]

    @pl.when((pl.program_id(0) == 0) & (t == 0))
    def _():
        w_scr[...] = w_ref[...].astype(BF16)

    @pl.when(t == 0)
    def _():
        _premix_channel_dft(x_ref, n_sub * seq, sc, sh, g_ref, cs_ref, xc_scr, xs_scr)

    for b in range(n_sub):
        src = slice(seq * b, seq * (b + 1))
        f = _dot(c_ref[...], xc_scr[src, :]) - _dot(s_ref[...], xs_scr[src, :])
        f_scr[tr * b:tr * (b + 1), :] = f.astype(BF16)
    mix = _dot(f_scr[...], w_scr[...])
    if n_sub == 1:
        x = x_ref[pl.ds(pl.multiple_of(t * tr, tr), tr), :]
    else:
        x = x_ref[...]
    o_ref[...] = x + gt * _rms(mix, g_ref[1:2, :])


def _fourier_mixer(cmat, smat, cs, x, mod, row_of_group, g, w, n_seqs, seq, n_sub, tr,
                   cast_jobs=()):
    assert n_sub == 1 or tr == seq
    n_t = seq // tr
    n_groups = n_seqs // n_sub
    rows = n_sub * seq
    cast_in, cast_out, cast_shapes = _cast_specs(
        cast_jobs, n_groups * n_t, lambda b, t: b * n_t + t)
    outs = pl.pallas_call(
        functools.partial(_fourier_kernel, n_sub=n_sub, seq=seq, tr=tr, n_cast=len(cast_jobs)),
        grid=(n_groups, n_t),
        in_specs=[
            pl.BlockSpec((tr, seq), lambda b, t: (t, 0)),
            pl.BlockSpec((tr, seq), lambda b, t: (t, 0)),
            _const_spec((FOURIER_GROUP, 2 * FOURIER_GROUP)),
            pl.BlockSpec((rows, D_MODEL), lambda b, t: (b, 0)),
            _mod_spec(lambda b, t: row_of_group(b)),
            _const_spec((4, D_MODEL)),
            _const_spec((D_MODEL, D_MODEL)),
        ] + cast_in,
        out_specs=[pl.BlockSpec((n_sub * tr, D_MODEL), lambda b, t: (b * n_t + t, 0))] + cast_out,
        out_shape=[jax.ShapeDtypeStruct((n_seqs * seq, D_MODEL), F32)] + cast_shapes,
        scratch_shapes=[pltpu.VMEM((rows, D_MODEL), BF16), pltpu.VMEM((rows, D_MODEL), BF16),
                        pltpu.VMEM((n_sub * tr, D_MODEL), BF16),
                        pltpu.VMEM((D_MODEL, D_MODEL), BF16)],
        compiler_params=_params(2),
        name="fourier_mixer",
    )(cmat, smat, cs, x, mod, g, w, *[src for src, _ in cast_jobs])
    return outs[0], outs[1:]


HERM_TILE = 512
HERM_ROWS = HERM_TILE + BF16_SUBLANES


def _half_spectrum_blocks(mat):
    n_half = mat.shape[0] // 2 // HERM_TILE
    return np.stack([mat[HERM_TILE * t:HERM_TILE * t + HERM_ROWS] for t in range(n_half)])


def _reversal_matrix():
    rev = np.zeros((HERM_TILE, HERM_ROWS), np.float32)
    rev[np.arange(HERM_TILE), HERM_TILE - np.arange(HERM_TILE)] = 1.0
    return rev


def _fourier_long_kernel(ch_ref, sh_ref, rev_ref, cs_ref, x_ref, mod_ref, g_ref, w_ref, *rest,
                         seq, n_cast):
    o_ref = rest[n_cast]
    xc_scr, xs_scr, pq_scr, f_scr, w_scr = rest[2 * n_cast + 1:]
    _run_cast_jobs(rest[:n_cast], rest[n_cast + 1:2 * n_cast + 1])
    b, t, u = pl.program_id(0), pl.program_id(1), pl.program_id(2)
    m = mod_ref[...]
    sh = m[:, 0:D_MODEL]
    sc = m[:, D_MODEL:2 * D_MODEL]
    gt = m[:, 2 * D_MODEL:3 * D_MODEL]

    @pl.when((b == 0) & (t == 0) & (u == 0))
    def _():
        w_scr[...] = w_ref[...].astype(BF16)

    @pl.when((t == 0) & (u == 0))
    def _():
        _premix_channel_dft(x_ref, seq, sc, sh, g_ref, cs_ref, xc_scr, xs_scr)

    @pl.when(u == 0)
    def _():
        p = _dot(ch_ref[...], xc_scr[...])
        q = _dot(sh_ref[...], xs_scr[...])
        f_scr[...] = (p - q)[0:HERM_TILE].astype(BF16)
        pq_scr[...] = (p + q).astype(BF16)

    @pl.when(u == 1)
    def _():
        f_scr[...] = _dot(rev_ref[...], pq_scr[...]).astype(BF16)

    n_tiles = seq // HERM_TILE
    tile = t + u * (n_tiles - 1 - 2 * t)
    mix = _dot(f_scr[...], w_scr[...])
    x = x_ref[pl.ds(pl.multiple_of(tile * HERM_TILE, HERM_TILE), HERM_TILE), :]
    o_ref[...] = x + gt * _rms(mix, g_ref[1:2, :])


def _fourier_mixer_long(cos_h, sin_h, rev, cs, x, mod, row_of_seq, g, w, n_seqs, seq, cast_jobs=()):
    n_tiles = seq // HERM_TILE
    n_half = n_tiles // 2
    tile_of = lambda t, u: t + u * (n_tiles - 1 - 2 * t)
    half = pl.BlockSpec((None, HERM_ROWS, seq), lambda b, t, u: (t, 0, 0))
    cast_in, cast_out, cast_shapes = _cast_specs(
        cast_jobs, n_seqs * n_tiles, lambda b, t, u: (b * n_half + t) * 2 + u)
    outs = pl.pallas_call(
        functools.partial(_fourier_long_kernel, seq=seq, n_cast=len(cast_jobs)),
        grid=(n_seqs, n_half, 2),
        in_specs=[
            half, half,
            _const_spec((HERM_TILE, HERM_ROWS)),
            _const_spec((FOURIER_GROUP, 2 * FOURIER_GROUP)),
            pl.BlockSpec((seq, D_MODEL), lambda b, t, u: (b, 0)),
            _mod_spec(lambda b, t, u: row_of_seq(b)),
            _const_spec((4, D_MODEL)),
            _const_spec((D_MODEL, D_MODEL)),
        ] + cast_in,
        out_specs=[pl.BlockSpec((HERM_TILE, D_MODEL),
                                lambda b, t, u: (b * n_tiles + tile_of(t, u), 0))] + cast_out,
        out_shape=[jax.ShapeDtypeStruct((n_seqs * seq, D_MODEL), F32)] + cast_shapes,
        scratch_shapes=[pltpu.VMEM((seq, D_MODEL), BF16), pltpu.VMEM((seq, D_MODEL), BF16),
                        pltpu.VMEM((HERM_ROWS, D_MODEL), BF16),
                        pltpu.VMEM((HERM_TILE, D_MODEL), BF16),
                        pltpu.VMEM((D_MODEL, D_MODEL), BF16)],
        compiler_params=_params(3),
        name="fourier_mixer_long",
    )(cos_h, sin_h, rev, cs, x, mod, g, w, *[src for src, _ in cast_jobs])
    return outs[0], outs[1:]


def _split_pair(q, lo):
    zero = jnp.zeros_like(q)
    return jnp.concatenate([jnp.where(lo, q, zero), jnp.where(lo, zero, q)], axis=0)


def _prompt_attn_kernel(x_ref, mod_ref, g_ref, wq_ref, wk_ref, wv_ref, wout_ref,
                        o_ref, kt_ref, vt_ref, wkt_ref, wvt_ref, h_scr, q_scr, kt_scr, vt_scr,
                        att_scr, *, n_seq):
    @pl.when(pl.program_id(0) == 0)
    def _():
        wkt_ref[...] = wk_ref[...].T
        wvt_ref[...] = wv_ref[...].T

    x = x_ref[...]
    m = mod_ref[...]
    sh = m[:, 0:D_MODEL]
    sc = m[:, D_MODEL:2 * D_MODEL]
    gt = m[:, 2 * D_MODEL:3 * D_MODEL]
    h_scr[...] = (_rms(x, g_ref[0:1, :]) * (1.0 + sc) + sh).astype(BF16)
    q_scr[...] = (_dot(h_scr[...], wq_ref[...]) * Q_SCALE).astype(BF16)

    lo = lax.broadcasted_iota(jnp.int32, (SEQ, PAIR_W), 1) < HEAD_DIM
    ones = jnp.ones((PAIR_W, SEQ), BF16)
    for b in range(n_seq):
        rows = slice(SEQ * b, SEQ * (b + 1))
        kt = _dot_nt(wkt_ref[...], h_scr[rows, :])
        vt = _dot_nt(wvt_ref[...], h_scr[rows, :])
        kt_ref[b] = kt.reshape(N_HEADS, HEAD_DIM, SEQ)
        vt_ref[b] = vt.reshape(N_HEADS, HEAD_DIM, SEQ)
        kt_scr[b] = kt.astype(BF16)
        vt_scr[b] = vt.astype(BF16)
        for j in range(N_PAIRS):
            lanes = slice(PAIR_W * j, PAIR_W * (j + 1))
            qs = _split_pair(q_scr[rows, lanes], lo)
            s = _dot(qs, kt_scr[b, lanes, :])
            p = jnp.exp2((s - jnp.max(s, axis=-1, keepdims=True)).astype(BF16))
            o2 = _dot_nt(p, jnp.concatenate([vt_scr[b, lanes, :], ones], axis=0))
            o = o2[:, 0:PAIR_W] / o2[:, PAIR_W:2 * PAIR_W]
            att_scr[rows, lanes] = jnp.where(lo, o[:SEQ], o[SEQ:]).astype(BF16)

    mix = _dot(att_scr[...], wout_ref[...])
    o_ref[...] = x + gt * _rms(mix, g_ref[1:2, :])


def _prompt_attn(x, mod, row, g, wqkv, wout, n_seq=2):
    t = x.shape[0]
    tm = n_seq * SEQ
    tok = pl.BlockSpec((tm, D_MODEL), lambda i: (i, 0))
    out = jax.ShapeDtypeStruct((t, D_MODEL), F32)
    cache = pl.BlockSpec((n_seq, None, N_HEADS, HEAD_DIM, SEQ), lambda i: (i, 0, 0, 0, 0))
    cache_out = jax.ShapeDtypeStruct((t // SEQ, 1, N_HEADS, HEAD_DIM, SEQ), F32)

    def qkv_part(n):
        return pl.BlockSpec((D_MODEL, D_MODEL), lambda i: (0, n), pipeline_mode=pl.Buffered(1))

    return pl.pallas_call(
        functools.partial(_prompt_attn_kernel, n_seq=n_seq),
        grid=(t // tm,),
        in_specs=[tok, _mod_spec(lambda i: row), _const_spec((4, D_MODEL)),
                  qkv_part(0), qkv_part(1), qkv_part(2), _const_spec((D_MODEL, D_MODEL))],
        out_specs=[tok, cache, cache],
        out_shape=[out, cache_out, cache_out],
        scratch_shapes=[pltpu.VMEM((D_MODEL, D_MODEL), BF16), pltpu.VMEM((D_MODEL, D_MODEL), BF16),
                        pltpu.VMEM((tm, D_MODEL), BF16), pltpu.VMEM((tm, D_MODEL), BF16),
                        pltpu.VMEM((n_seq, D_MODEL, SEQ), BF16),
                        pltpu.VMEM((n_seq, D_MODEL, SEQ), BF16),
                        pltpu.VMEM((tm, D_MODEL), BF16)],
        compiler_params=_params(1),
        name="prompt_attn",
    )(x, mod, g, wqkv, wqkv, wqkv, wout)


def _premix_qkv_kernel(x_ref, mod_ref, g_ref, wqkv_ref, ckt_ref, cvt_ref,
                       q_ref, k_ref, v_ref, ck_ref, cv_ref):
    ck_ref[...] = ckt_ref[...].astype(BF16)
    _store_values_with_ones(cv_ref, cvt_ref[...].T.astype(BF16))

    x = x_ref[...]
    m = mod_ref[...]
    sh = m[:, 0:D_MODEL]
    sc = m[:, D_MODEL:2 * D_MODEL]
    h = (_rms(x, g_ref[0:1, :]) * (1.0 + sc) + sh).astype(BF16)
    qkv = _dot(h, wqkv_ref[...])
    q_ref[...] = (qkv[:, 0:D_MODEL] * Q_SCALE).astype(BF16)
    k_ref[...] = qkv[:, D_MODEL:2 * D_MODEL].astype(BF16)
    _store_values_with_ones(v_ref, qkv[:, 2 * D_MODEL:3 * D_MODEL].astype(BF16))


def _store_values_with_ones(v_ref, v):
    ones = jnp.ones((v.shape[0], PAIR_W), BF16)
    for j in range(v.shape[1] // PAIR_W):
        v_ref[:, 2 * PAIR_W * j:2 * PAIR_W * j + PAIR_W] = v[:, PAIR_W * j:PAIR_W * (j + 1)]
        v_ref[:, 2 * PAIR_W * j + PAIR_W:2 * PAIR_W * (j + 1)] = ones


def _premix_qkv(x, mod, row_of_step, g, wqkv, cache_kt, cache_vt, layer_j, tm=512):
    t = x.shape[0]
    steps_per_seq = DEC_SEQ // tm
    chunk = D_MODEL // steps_per_seq
    tok = pl.BlockSpec((tm, D_MODEL), lambda i: (i, 0))
    tok2 = pl.BlockSpec((tm, 2 * D_MODEL), lambda i: (i, 0))
    cache = pl.BlockSpec((None, None, chunk, PAST_LEN),
                         lambda i: (i // steps_per_seq, layer_j, i % steps_per_seq, 0))
    out = jax.ShapeDtypeStruct((t, D_MODEL), BF16)
    out2 = jax.ShapeDtypeStruct((t, 2 * D_MODEL), BF16)
    return pl.pallas_call(
        _premix_qkv_kernel,
        grid=(t // tm,),
        in_specs=[tok, _mod_spec(row_of_step), _const_spec((4, D_MODEL)),
                  _const_spec((D_MODEL, 3 * D_MODEL)), cache, cache],
        out_specs=[tok, tok, tok2,
                   pl.BlockSpec((chunk, PAST_LEN), lambda i: (i, 0)),
                   pl.BlockSpec((PAST_LEN, 2 * chunk),
                                lambda i: (i // steps_per_seq, i % steps_per_seq))],
        out_shape=[out, out, out2,
                   jax.ShapeDtypeStruct((DEC_BATCH * D_MODEL, PAST_LEN), BF16),
                   jax.ShapeDtypeStruct((DEC_BATCH * PAST_LEN, 2 * D_MODEL), BF16)],
        compiler_params=_params(1),
        name="premix_qkv",
    )(x, mod, g, wqkv, cache_kt, cache_vt)


def _build_bias_table(l_ref, bias_ref):
    lane = lax.broadcasted_iota(jnp.int32, (GRID_W, PAIR_W), 1)
    qcol = lax.broadcasted_iota(jnp.int32, (GRID_W, PAIR_W), 0)
    kcol = lane & (GRID_W - 1)
    start = jnp.clip(qcol - WIN_COLS // 2, 0, GRID_W - WIN_COLS)
    in_window = (kcol >= start) & (kcol < start + WIN_COLS)
    lo = lane < GRID_W

    def one_head(h, carry):
        def toeplitz(d, shift):
            row = jnp.broadcast_to(l_ref[h, d:d + 1, :], (GRID_W, PAIR_W))
            return pltpu.roll(row, shift, 1, stride=1, stride_axis=0)

        for d in range(N_DR_PAIRS):
            both = jnp.where(lo, toeplitz(d, GRID_W + 1), toeplitz(d + 1, 1))
            bias_ref[h * N_DR_PAIRS + d] = jnp.where(in_window, both * LOG2E, NEG_INF)
        return carry

    lax.fori_loop(0, N_HEADS, one_head, 0)


def _na_attn_kernel(q_ref, k_ref, v_ref, ckt_ref, cv_ref, l_ref, x_ref, mod_ref, g_ref,
                    wout_ref, o_ref, bias_ref, s_scr, m_scr, p_scr, att_scr, *, rows_per_step):
    @pl.when((pl.program_id(0) == 0) & (pl.program_id(1) == 0))
    def _():
        _build_bias_table(l_ref, bias_ref)

    blk = pl.program_id(1)
    lo = lax.broadcasted_iota(jnp.int32, (GRID_W, PAIR_W), 1) < HEAD_DIM

    def row_geometry(i):
        r = blk * rows_per_step + i
        rs = jnp.clip(r - WIN_ROWS // 2, 0, GRID_ROWS - WIN_ROWS)
        d0 = rs - r + (WIN_ROWS - 1)
        q0 = pl.multiple_of(i * GRID_W, GRID_W)
        k0 = pl.multiple_of(rs * GRID_W, GRID_W)
        return d0, q0, k0

    def scores(i, slot):
        d0, q0, k0 = row_geometry(i)
        for j in range(N_PAIRS):
            lanes = slice(PAIR_W * j, PAIR_W * (j + 1))
            qs = _split_pair(q_ref[pl.ds(q0, GRID_W), lanes], lo)
            bias = jnp.concatenate(
                [jnp.concatenate(
                    [bias_ref[(2 * j) * N_DR_PAIRS + d0 + 2 * jj],
                     bias_ref[(2 * j + 1) * N_DR_PAIRS + d0 + 2 * jj]], axis=0)
                 for jj in range(WIN_ROWS // 2)], axis=1)
            s_loc = _dot_nt(qs, k_ref[pl.ds(k0, N_LOCAL), lanes]) + bias
            s_ctx = _dot(qs, ckt_ref[lanes, :])
            mx = jnp.maximum(jnp.max(s_loc, axis=-1, keepdims=True),
                             jnp.max(s_ctx, axis=-1, keepdims=True))
            s_scr[slot, j, :, 0:N_LOCAL] = s_loc
            s_scr[slot, j, :, N_LOCAL:N_KEYS] = s_ctx
            m_scr[slot, j] = jnp.broadcast_to(mx, (2 * GRID_W, PAIR_W))

    def probs(slot):
        for j in range(N_PAIRS):
            mx = m_scr[slot, j][:, 0:1]
            p_scr[slot, j] = jnp.exp2((s_scr[slot, j] - mx).astype(BF16))

    def values(i, slot):
        _, q0, k0 = row_geometry(i)
        for j in range(N_PAIRS):
            lanes2 = slice(2 * PAIR_W * j, 2 * PAIR_W * (j + 1))
            p = p_scr[slot, j]
            o2 = (_dot(p[:, 0:N_LOCAL], v_ref[pl.ds(k0, N_LOCAL), lanes2])
                  + _dot(p[:, N_LOCAL:N_KEYS], cv_ref[:, lanes2]))
            o = o2[:, 0:PAIR_W] / o2[:, PAIR_W:2 * PAIR_W]
            att_scr[pl.ds(q0, GRID_W), PAIR_W * j:PAIR_W * (j + 1)] = (
                jnp.where(lo, o[:GRID_W], o[GRID_W:]).astype(BF16))

    scores(0, 0)
    probs(0)
    scores(1, 1)

    def two_rows(t, carry):
        i = 2 * t
        values(i - 2, 0)
        probs(1)
        scores(i, 0)
        values(i - 1, 1)
        probs(0)
        scores(i + 1, 1)
        return carry

    lax.fori_loop(1, rows_per_step // 2, two_rows, 0)
    values(rows_per_step - 2, 0)
    probs(1)
    values(rows_per_step - 1, 1)

    gt = mod_ref[...][:, 2 * D_MODEL:3 * D_MODEL]
    mix = _dot(att_scr[...], wout_ref[...])
    o_ref[...] = x_ref[...] + gt * _rms(mix, g_ref[1:2, :])


def _na_attn(q, k, v, ck, cv, rpb, x, mod, row_of_batch, g, wout, rows_per_step=8):
    rpb_rows = jnp.pad(rpb, ((0, 0), (0, 0), (48, 49)), mode="edge")
    tm = rows_per_step * GRID_W
    n_t = DEC_SEQ // tm
    tok = pl.BlockSpec((tm, D_MODEL), lambda b, t: (b * n_t + t, 0))

    def per_batch(rows, width):
        return pl.BlockSpec((rows, width), lambda b, t: (b, 0), pipeline_mode=pl.Buffered(1))

    return pl.pallas_call(
        functools.partial(_na_attn_kernel, rows_per_step=rows_per_step),
        grid=(DEC_BATCH, n_t),
        in_specs=[
            tok,
            per_batch(DEC_SEQ, D_MODEL), per_batch(DEC_SEQ, 2 * D_MODEL),
            per_batch(D_MODEL, PAST_LEN), per_batch(PAST_LEN, 2 * D_MODEL),
            _const_spec((N_HEADS, N_DR, PAIR_W)),
            tok,
            _mod_spec(lambda b, t: row_of_batch(b)),
            _const_spec((4, D_MODEL)),
            _const_spec((D_MODEL, D_MODEL)),
        ],
        out_specs=tok,
        out_shape=jax.ShapeDtypeStruct((DEC_BATCH * DEC_SEQ, D_MODEL), F32),
        scratch_shapes=[
            pltpu.VMEM((N_HEADS * N_DR_PAIRS, GRID_W, PAIR_W), F32),
            pltpu.VMEM((2, N_PAIRS, 2 * GRID_W, N_KEYS), F32),
            pltpu.VMEM((2, N_PAIRS, 2 * GRID_W, PAIR_W), F32),
            pltpu.VMEM((2, N_PAIRS, 2 * GRID_W, N_KEYS), BF16),
            pltpu.VMEM((tm, D_MODEL), BF16),
        ],
        compiler_params=_params(2),
        name="na_attn",
    )(q, k, v, ck, cv, rpb_rows, x, mod, g, wout)


def kernel(x_prompt, x_sample, c, cache_k, cache_v, c_ctx, ada_w, ada_b, norm_g, fourier_w_out,
           na_w_qkv, na_rpb, na_w_out, ffn_w_gate, ffn_w_up, ffn_w_down):
    n_p = BATCH * SEQ
    n_s = DEC_BATCH * DEC_SEQ
    xp = x_prompt.reshape(n_p, D_MODEL)
    xs = x_sample.reshape(n_s, D_MODEL)

    cond = jnp.concatenate(
        [c_ctx[None, :], c, jnp.zeros((COND_ROWS - 1 - DEC_BATCH, D_MODEL), F32)], axis=0)
    mod = _modulation(cond, ada_w, ada_b)

    cos_g, sin_g = _dft_cos_sin(FOURIER_GROUP)
    cs_chan = jnp.asarray(np.concatenate([cos_g, sin_g], axis=1), F32).astype(BF16)
    cos_p, sin_p = (jnp.asarray(a, F32).astype(BF16) for a in _dft_cos_sin(SEQ))
    cos_s, sin_s = (jnp.asarray(_half_spectrum_blocks(a), F32).astype(BF16)
                    for a in _dft_cos_sin(DEC_SEQ))
    rev = jnp.asarray(_reversal_matrix(), F32).astype(BF16)

    tm = 512
    ffn_tm = 1024
    ffn_weights = (ffn_w_gate, ffn_w_up, ffn_w_down)
    cache_kt = jnp.transpose(cache_k, (0, 1, 3, 4, 2)).reshape(DEC_BATCH, -1, D_MODEL, PAST_LEN)
    cache_vt = jnp.transpose(cache_v, (0, 1, 3, 4, 2)).reshape(DEC_BATCH, -1, D_MODEL, PAST_LEN)
    new_kt = new_vt = None
    attn_weights = None
    for layer in range(DEPTH):
        g = norm_g[layer]
        base = layer * COND_ROWS
        prompt_row = lambda *_, base=base: base
        sample_row_of_batch = lambda b, base=base: base + 1 + b
        sample_row_of_tile = lambda i, base=base: base + 1 + i // (DEC_SEQ // tm)
        sample_row_of_ffn_tile = lambda i, base=base: base + 1 + i // (DEC_SEQ // ffn_tm)

        if layer % 2 == 0:
            w_out = fourier_w_out[layer // 2]
            jobs = [(w, layer) for w in ffn_weights] if layer == 0 else []
            xp, cast_p = _fourier_mixer(cos_p, sin_p, cs_chan, xp, mod, prompt_row, g, w_out,
                                        BATCH, SEQ, n_sub=4, tr=SEQ, cast_jobs=jobs[:2])
            xs, cast_s = _fourier_mixer_long(cos_s, sin_s, rev, cs_chan, xs, mod,
                                             sample_row_of_batch, g, w_out, DEC_BATCH, DEC_SEQ,
                                             cast_jobs=jobs[2:])
            if layer == 0:
                wg, wu, wd = list(cast_p) + list(cast_s)
        else:
            j = layer // 2
            if attn_weights is None:
                attn_weights = (na_w_qkv[j].astype(BF16), na_w_out[j].astype(BF16))
            w_qkv, w_out = attn_weights
            attn_weights = None
            xp, new_kt, new_vt = _prompt_attn(xp, mod, base, g, w_qkv, w_out)
            q, k, v, ckt, cvt = _premix_qkv(xs, mod, sample_row_of_tile, g, w_qkv,
                                            cache_kt, cache_vt, j, tm)
            xs = _na_attn(q, k, v, ckt, cvt, na_rpb[j], xs, mod, sample_row_of_batch, g, w_out)

        nxt = layer + 1
        ffn_jobs = [(w, nxt) for w in ffn_weights] if nxt < DEPTH else []
        attn_jobs = ([(na_w_qkv, nxt // 2), (na_w_out, nxt // 2)]
                     if nxt < DEPTH and nxt % 2 == 1 else [])
        xp, cast_ffn = _ffn(xp, mod, prompt_row, g, wg, wu, wd, tm if ffn_jobs else ffn_tm,
                            cast_jobs=ffn_jobs)
        xs, cast_attn = _ffn(xs, mod, sample_row_of_tile if attn_jobs else sample_row_of_ffn_tile,
                             g, wg, wu, wd, tm if attn_jobs else ffn_tm, cast_jobs=attn_jobs)
        if ffn_jobs:
            wg, wu, wd = cast_ffn
        if attn_jobs:
            attn_weights = tuple(cast_attn)

    new_k = jnp.transpose(new_kt, (0, 1, 4, 2, 3))
    new_v = jnp.transpose(new_vt, (0, 1, 4, 2, 3))
    return (xp.reshape(BATCH, SEQ, D_MODEL), xs.reshape(DEC_BATCH, DEC_SEQ, D_MODEL), new_k, new_v)
```

```python
import functools

import numpy as np
import jax
import jax.numpy as jnp
from jax import lax
from jax.experimental import pallas as pl
from jax.experimental.pallas import tpu as pltpu

D_MODEL = 1024
BATCH = 32
SEQ = 256
DEPTH = 2
DEC_BATCH = 2
DEC_SEQ = 2048
PAST_LEN = 512
GRID_W = 64
GRID_ROWS = DEC_SEQ // GRID_W
N_HEADS = 16
HEAD_DIM = D_MODEL // N_HEADS
N_FOURIER_GROUPS = 4
FOURIER_GROUP = D_MODEL // N_FOURIER_GROUPS
WIN_ROWS = 8
WIN_COLS = 16
D_FF = 2816
EPS = 1e-6
NEG_INF = -1e30

N_PAIRS = N_HEADS // 2
PAIR_W = 2 * HEAD_DIM
N_DR = 2 * WIN_ROWS - 1
N_DR_PAIRS = N_DR - 1
N_LOCAL = WIN_ROWS * GRID_W
N_KEYS = N_LOCAL + PAST_LEN
COND_ROWS = 8
LOG2E = 1.4426950408889634
Q_SCALE = HEAD_DIM ** -0.5 * LOG2E

VMEM_LIMIT = 56 * 1024 * 1024
BF16_SUBLANES = 16

F32 = jnp.float32
BF16 = jnp.bfloat16


def _dot(a, b):
    return jnp.dot(a, b, preferred_element_type=F32)


def _dot_nt(a, b):
    return lax.dot_general(a, b, (((1,), (1,)), ((), ())), preferred_element_type=F32)


def _rms(x, g):
    ms = jnp.mean(x * x, axis=-1, keepdims=True)
    return x * lax.rsqrt(ms + EPS) * g


def _silu(x):
    return x / (1.0 + jnp.exp(-x))


def _const_spec(shape):
    return pl.BlockSpec(shape, lambda *_: (0,) * len(shape), pipeline_mode=pl.Buffered(1))


def _mod_spec(row_of_step):
    return pl.BlockSpec((None, 1, 6 * D_MODEL), lambda *idx: (row_of_step(*idx), 0, 0))


def _params(n_axes):
    return pltpu.CompilerParams(
        dimension_semantics=("arbitrary",) * n_axes, vmem_limit_bytes=VMEM_LIMIT)


def _mod_kernel(cond_ref, w_ref, b_ref, o_ref, acc_ref):
    layer, k = pl.program_id(0), pl.program_id(1)

    @pl.when(k == 0)
    def _():
        acc_ref[...] = jnp.broadcast_to(b_ref[pl.ds(layer, 1), :], acc_ref.shape)

    a = _silu(cond_ref[...])
    a_hi = a.astype(BF16)
    a_lo = (a - a_hi.astype(F32)).astype(BF16)
    w = w_ref[...].astype(BF16)
    acc_ref[...] += _dot(a_hi, w) + _dot(a_lo, w)

    @pl.when(k == pl.num_programs(1) - 1)
    def _():
        for r in range(COND_ROWS):
            o_ref[r] = acc_ref[r:r + 1, :]


def _modulation(cond, ada_w, ada_b):
    tk = 256
    n_out = 6 * D_MODEL
    return pl.pallas_call(
        _mod_kernel,
        grid=(DEPTH, D_MODEL // tk),
        in_specs=[
            pl.BlockSpec((COND_ROWS, tk), lambda l, k: (0, k)),
            pl.BlockSpec((None, tk, n_out), lambda l, k: (l, k, 0)),
            pl.BlockSpec((DEPTH, n_out), lambda l, k: (0, 0)),
        ],
        out_specs=pl.BlockSpec((COND_ROWS, 1, n_out), lambda l, k: (l, 0, 0)),
        out_shape=jax.ShapeDtypeStruct((DEPTH * COND_ROWS, 1, n_out), F32),
        scratch_shapes=[pltpu.VMEM((COND_ROWS, n_out), F32)],
        compiler_params=_params(2),
        name="modulation",
    )(cond, ada_w, ada_b)


FFN_SUB_ROWS = 256


def _cast_specs(jobs, n_steps, step_of):
    in_specs, out_specs, out_shapes = [], [], []
    for src, layer in jobs:
        rows, cols = src.shape[1:]
        chunk = rows // n_steps
        assert chunk * n_steps == rows and chunk % BF16_SUBLANES == 0
        in_specs.append(pl.BlockSpec(
            (None, chunk, cols), lambda *idx, layer=layer: (layer, step_of(*idx), 0)))
        out_specs.append(pl.BlockSpec((chunk, cols), lambda *idx: (step_of(*idx), 0)))
        out_shapes.append(jax.ShapeDtypeStruct((rows, cols), BF16))
    return in_specs, out_specs, out_shapes


def _run_cast_jobs(src_refs, dst_refs):
    for src, dst in zip(src_refs, dst_refs):
        dst[...] = src[...].astype(BF16)


def _ffn_kernel(x_ref, mod_ref, g_ref, wg_ref, wu_ref, wd_ref, *rest, n_cast):
    o_ref = rest[n_cast]
    _run_cast_jobs(rest[:n_cast], rest[n_cast + 1:])
    m = mod_ref[...]
    sh = m[:, 3 * D_MODEL:4 * D_MODEL]
    sc = m[:, 4 * D_MODEL:5 * D_MODEL]
    gt = m[:, 5 * D_MODEL:6 * D_MODEL]
    for r0 in range(0, x_ref.shape[0], FFN_SUB_ROWS):
        rows = slice(r0, r0 + FFN_SUB_ROWS)
        x = x_ref[rows, :]
        f = (_rms(x, g_ref[2:3, :]) * (1.0 + sc) + sh).astype(BF16)
        gate = _dot(f, wg_ref[...])
        up = _dot(f, wu_ref[...])
        act = (_silu(gate) * up).astype(BF16)
        y = _dot(act, wd_ref[...])
        o_ref[rows, :] = x + gt * _rms(y, g_ref[3:4, :])


def _ffn(x, mod, row_of_step, g, wg, wu, wd, tm, cast_jobs=()):
    t = x.shape[0]
    tok = pl.BlockSpec((tm, D_MODEL), lambda i: (i, 0))
    cast_in, cast_out, cast_shapes = _cast_specs(cast_jobs, t // tm, lambda i: i)
    outs = pl.pallas_call(
        functools.partial(_ffn_kernel, n_cast=len(cast_jobs)),
        grid=(t // tm,),
        in_specs=[
            tok,
            _mod_spec(row_of_step),
            _const_spec((4, D_MODEL)),
            _const_spec((D_MODEL, D_FF)),
            _const_spec((D_MODEL, D_FF)),
            _const_spec((D_FF, D_MODEL)),
        ] + cast_in,
        out_specs=[tok] + cast_out,
        out_shape=[jax.ShapeDtypeStruct((t, D_MODEL), F32)] + cast_shapes,
        compiler_params=_params(1),
        name="ffn",
    )(x, mod, g, wg, wu, wd, *[src for src, _ in cast_jobs])
    return outs[0], outs[1:]


def _dft_cos_sin(n):
    j = np.arange(n)
    ang = 2.0 * np.pi * ((j[:, None] * j[None, :]) % n) / n
    return np.cos(ang), np.sin(ang)


PREMIX_CHUNK = 512


def _premix_channel_dft(x_ref, n_rows, sc, sh, g_ref, cs_ref, xc_scr, xs_scr):
    cs = cs_ref[...]
    for r0 in range(0, n_rows, PREMIX_CHUNK):
        rows = slice(r0, r0 + PREMIX_CHUNK)
        h = (_rms(x_ref[rows, :], g_ref[0:1, :]) * (1.0 + sc) + sh).astype(BF16)
        for j in range(N_FOURIER_GROUPS):
            lanes = slice(FOURIER_GROUP * j, FOURIER_GROUP * (j + 1))
            r = _dot(h[:, lanes], cs)
            xc_scr[rows, lanes] = r[:, :FOURIER_GROUP].astype(BF16)
            xs_scr[rows, lanes] = r[:, FOURIER_GROUP:].astype(BF16)


def _fourier_kernel(c_ref, s_ref, cs_ref, x_ref, mod_ref, g_ref, w_ref, *rest,
                    n_sub, seq, tr, n_cast):
    o_ref = rest[n_cast]
    xc_scr, xs_scr, f_scr, w_scr = rest[2 * n_cast + 1:]
    _run_cast_jobs(rest[:n_cast], rest[n_cast + 1:2 * n_cast + 1])
    t = pl.program_id(1)
    m = mod_ref[...]
    sh = m[:, 0:D_MODEL]
    sc = m[:, D_MODEL:2 * D_MODEL]
    gt = m[:, 2 * D_MODEL:3 * D_MODEL]

    @pl.when((pl.program_id(0) == 0) & (t == 0))
    def _():
        w_scr[...] = w_ref[...].astype(BF16)

    @pl.when(t == 0)
    def _():
        _premix_channel_dft(x_ref, n_sub * seq, sc, sh, g_ref, cs_ref, xc_scr, xs_scr)

    for b in range(n_sub):
        src = slice(seq * b, seq * (b + 1))
        f = _dot(c_ref[...], xc_scr[src, :]) - _dot(s_ref[...], xs_scr[src, :])
        f_scr[tr * b:tr * (b + 1), :] = f.astype(BF16)
    mix = _dot(f_scr[...], w_scr[...])
    if n_sub == 1:
        x = x_ref[pl.ds(pl.multiple_of(t * tr, tr), tr), :]
    else:
        x = x_ref[...]
    o_ref[...] = x + gt * _rms(mix, g_ref[1:2, :])


def _fourier_mixer(cmat, smat, cs, x, mod, row_of_group, g, w, n_seqs, seq, n_sub, tr,
                   cast_jobs=()):
    assert n_sub == 1 or tr == seq
    n_t = seq // tr
    n_groups = n_seqs // n_sub
    rows = n_sub * seq
    cast_in, cast_out, cast_shapes = _cast_specs(
        cast_jobs, n_groups * n_t, lambda b, t: b * n_t + t)
    outs = pl.pallas_call(
        functools.partial(_fourier_kernel, n_sub=n_sub, seq=seq, tr=tr, n_cast=len(cast_jobs)),
        grid=(n_groups, n_t),
        in_specs=[
            pl.BlockSpec((tr, seq), lambda b, t: (t, 0)),
            pl.BlockSpec((tr, seq), lambda b, t: (t, 0)),
            _const_spec((FOURIER_GROUP, 2 * FOURIER_GROUP)),
            pl.BlockSpec((rows, D_MODEL), lambda b, t: (b, 0)),
            _mod_spec(lambda b, t: row_of_group(b)),
            _const_spec((4, D_MODEL)),
            _const_spec((D_MODEL, D_MODEL)),
        ] + cast_in,
        out_specs=[pl.BlockSpec((n_sub * tr, D_MODEL), lambda b, t: (b * n_t + t, 0))] + cast_out,
        out_shape=[jax.ShapeDtypeStruct((n_seqs * seq, D_MODEL), F32)] + cast_shapes,
        scratch_shapes=[pltpu.VMEM((rows, D_MODEL), BF16), pltpu.VMEM((rows, D_MODEL), BF16),
                        pltpu.VMEM((n_sub * tr, D_MODEL), BF16),
                        pltpu.VMEM((D_MODEL, D_MODEL), BF16)],
        compiler_params=_params(2),
        name="fourier_mixer",
    )(cmat, smat, cs, x, mod, g, w, *[src for src, _ in cast_jobs])
    return outs[0], outs[1:]


HERM_TILE = 512
HERM_ROWS = HERM_TILE + BF16_SUBLANES


def _half_spectrum_blocks(mat):
    n_half = mat.shape[0] // 2 // HERM_TILE
    return np.stack([mat[HERM_TILE * t:HERM_TILE * t + HERM_ROWS] for t in range(n_half)])


def _reversal_matrix():
    rev = np.zeros((HERM_TILE, HERM_ROWS), np.float32)
    rev[np.arange(HERM_TILE), HERM_TILE - np.arange(HERM_TILE)] = 1.0
    return rev


def _fourier_long_kernel(ch_ref, sh_ref, rev_ref, cs_ref, x_ref, mod_ref, g_ref, w_ref, *rest,
                         seq, n_cast):
    o_ref = rest[n_cast]
    xc_scr, xs_scr, pq_scr, f_scr, w_scr = rest[2 * n_cast + 1:]
    _run_cast_jobs(rest[:n_cast], rest[n_cast + 1:2 * n_cast + 1])
    b, t, u = pl.program_id(0), pl.program_id(1), pl.program_id(2)
    m = mod_ref[...]
    sh = m[:, 0:D_MODEL]
    sc = m[:, D_MODEL:2 * D_MODEL]
    gt = m[:, 2 * D_MODEL:3 * D_MODEL]

    @pl.when((b == 0) & (t == 0) & (u == 0))
    def _():
        w_scr[...] = w_ref[...].astype(BF16)

    @pl.when((t == 0) & (u == 0))
    def _():
        _premix_channel_dft(x_ref, seq, sc, sh, g_ref, cs_ref, xc_scr, xs_scr)

    @pl.when(u == 0)
    def _():
        p = _dot(ch_ref[...], xc_scr[...])
        q = _dot(sh_ref[...], xs_scr[...])
        f_scr[...] = (p - q)[0:HERM_TILE].astype(BF16)
        pq_scr[...] = (p + q).astype(BF16)

    @pl.when(u == 1)
    def _():
        f_scr[...] = _dot(rev_ref[...], pq_scr[...]).astype(BF16)

    n_tiles = seq // HERM_TILE
    tile = t + u * (n_tiles - 1 - 2 * t)
    mix = _dot(f_scr[...], w_scr[...])
    x = x_ref[pl.ds(pl.multiple_of(tile * HERM_TILE, HERM_TILE), HERM_TILE), :]
    o_ref[...] = x + gt * _rms(mix, g_ref[1:2, :])


def _fourier_mixer_long(cos_h, sin_h, rev, cs, x, mod, row_of_seq, g, w, n_seqs, seq, cast_jobs=()):
    n_tiles = seq // HERM_TILE
    n_half = n_tiles // 2
    tile_of = lambda t, u: t + u * (n_tiles - 1 - 2 * t)
    half = pl.BlockSpec((None, HERM_ROWS, seq), lambda b, t, u: (t, 0, 0))
    cast_in, cast_out, cast_shapes = _cast_specs(
        cast_jobs, n_seqs * n_tiles, lambda b, t, u: (b * n_half + t) * 2 + u)
    outs = pl.pallas_call(
        functools.partial(_fourier_long_kernel, seq=seq, n_cast=len(cast_jobs)),
        grid=(n_seqs, n_half, 2),
        in_specs=[
            half, half,
            _const_spec((HERM_TILE, HERM_ROWS)),
            _const_spec((FOURIER_GROUP, 2 * FOURIER_GROUP)),
            pl.BlockSpec((seq, D_MODEL), lambda b, t, u: (b, 0)),
            _mod_spec(lambda b, t, u: row_of_seq(b)),
            _const_spec((4, D_MODEL)),
            _const_spec((D_MODEL, D_MODEL)),
        ] + cast_in,
        out_specs=[pl.BlockSpec((HERM_TILE, D_MODEL),
                                lambda b, t, u: (b * n_tiles + tile_of(t, u), 0))] + cast_out,
        out_shape=[jax.ShapeDtypeStruct((n_seqs * seq, D_MODEL), F32)] + cast_shapes,
        scratch_shapes=[pltpu.VMEM((seq, D_MODEL), BF16), pltpu.VMEM((seq, D_MODEL), BF16),
                        pltpu.VMEM((HERM_ROWS, D_MODEL), BF16),
                        pltpu.VMEM((HERM_TILE, D_MODEL), BF16),
                        pltpu.VMEM((D_MODEL, D_MODEL), BF16)],
        compiler_params=_params(3),
        name="fourier_mixer_long",
    )(cos_h, sin_h, rev, cs, x, mod, g, w, *[src for src, _ in cast_jobs])
    return outs[0], outs[1:]


def _split_pair(q, lo):
    zero = jnp.zeros_like(q)
    return jnp.concatenate([jnp.where(lo, q, zero), jnp.where(lo, zero, q)], axis=0)


def _prompt_attn_kernel(x_ref, mod_ref, g_ref, wq_ref, wk_ref, wv_ref, wout_ref,
                        o_ref, kt_ref, vt_ref, wkt_ref, wvt_ref, h_scr, q_scr, kt_scr, vt_scr,
                        att_scr, *, n_seq):
    @pl.when(pl.program_id(0) == 0)
    def _():
        wkt_ref[...] = wk_ref[...].T
        wvt_ref[...] = wv_ref[...].T

    x = x_ref[...]
    m = mod_ref[...]
    sh = m[:, 0:D_MODEL]
    sc = m[:, D_MODEL:2 * D_MODEL]
    gt = m[:, 2 * D_MODEL:3 * D_MODEL]
    h_scr[...] = (_rms(x, g_ref[0:1, :]) * (1.0 + sc) + sh).astype(BF16)
    q_scr[...] = (_dot(h_scr[...], wq_ref[...]) * Q_SCALE).astype(BF16)

    lo = lax.broadcasted_iota(jnp.int32, (SEQ, PAIR_W), 1) < HEAD_DIM
    ones = jnp.ones((PAIR_W, SEQ), BF16)
    for b in range(n_seq):
        rows = slice(SEQ * b, SEQ * (b + 1))
        kt = _dot_nt(wkt_ref[...], h_scr[rows, :])
        vt = _dot_nt(wvt_ref[...], h_scr[rows, :])
        kt_ref[b] = kt.reshape(N_HEADS, HEAD_DIM, SEQ)
        vt_ref[b] = vt.reshape(N_HEADS, HEAD_DIM, SEQ)
        kt_scr[b] = kt.astype(BF16)
        vt_scr[b] = vt.astype(BF16)
        for j in range(N_PAIRS):
            lanes = slice(PAIR_W * j, PAIR_W * (j + 1))
            qs = _split_pair(q_scr[rows, lanes], lo)
            s = _dot(qs, kt_scr[b, lanes, :])
            p = jnp.exp2((s - jnp.max(s, axis=-1, keepdims=True)).astype(BF16))
            o2 = _dot_nt(p, jnp.concatenate([vt_scr[b, lanes, :], ones], axis=0))
            o = o2[:, 0:PAIR_W] / o2[:, PAIR_W:2 * PAIR_W]
            att_scr[rows, lanes] = jnp.where(lo, o[:SEQ], o[SEQ:]).astype(BF16)

    mix = _dot(att_scr[...], wout_ref[...])
    o_ref[...] = x + gt * _rms(mix, g_ref[1:2, :])


def _prompt_attn(x, mod, row, g, wqkv, wout, n_seq=4):
    t = x.shape[0]
    tm = n_seq * SEQ
    tok = pl.BlockSpec((tm, D_MODEL), lambda i: (i, 0))
    out = jax.ShapeDtypeStruct((t, D_MODEL), F32)
    cache = pl.BlockSpec((n_seq, None, N_HEADS, HEAD_DIM, SEQ), lambda i: (i, 0, 0, 0, 0))
    cache_out = jax.ShapeDtypeStruct((t // SEQ, 1, N_HEADS, HEAD_DIM, SEQ), F32)

    def qkv_part(n):
        return pl.BlockSpec((D_MODEL, D_MODEL), lambda i: (0, n), pipeline_mode=pl.Buffered(1))

    return pl.pallas_call(
        functools.partial(_prompt_attn_kernel, n_seq=n_seq),
        grid=(t // tm,),
        in_specs=[tok, _mod_spec(lambda i: row), _const_spec((4, D_MODEL)),
                  qkv_part(0), qkv_part(1), qkv_part(2), _const_spec((D_MODEL, D_MODEL))],
        out_specs=[tok, cache, cache],
        out_shape=[out, cache_out, cache_out],
        scratch_shapes=[pltpu.VMEM((D_MODEL, D_MODEL), BF16), pltpu.VMEM((D_MODEL, D_MODEL), BF16),
                        pltpu.VMEM((tm, D_MODEL), BF16), pltpu.VMEM((tm, D_MODEL), BF16),
                        pltpu.VMEM((n_seq, D_MODEL, SEQ), BF16),
                        pltpu.VMEM((n_seq, D_MODEL, SEQ), BF16),
                        pltpu.VMEM((tm, D_MODEL), BF16)],
        compiler_params=_params(1),
        name="prompt_attn",
    )(x, mod, g, wqkv, wqkv, wqkv, wout)


def _premix_qkv_kernel(x_ref, mod_ref, g_ref, wqkv_ref, ckt_ref, cvt_ref,
                       q_ref, k_ref, v_ref, ck_ref, cv_ref):
    ck_ref[...] = ckt_ref[...].astype(BF16)
    _store_values_with_ones(cv_ref, cvt_ref[...].T.astype(BF16))

    x = x_ref[...]
    m = mod_ref[...]
    sh = m[:, 0:D_MODEL]
    sc = m[:, D_MODEL:2 * D_MODEL]
    h = (_rms(x, g_ref[0:1, :]) * (1.0 + sc) + sh).astype(BF16)
    qkv = _dot(h, wqkv_ref[...])
    q_ref[...] = (qkv[:, 0:D_MODEL] * Q_SCALE).astype(BF16)
    k_ref[...] = qkv[:, D_MODEL:2 * D_MODEL].astype(BF16)
    _store_values_with_ones(v_ref, qkv[:, 2 * D_MODEL:3 * D_MODEL].astype(BF16))


def _store_values_with_ones(v_ref, v):
    ones = jnp.ones((v.shape[0], PAIR_W), BF16)
    for j in range(v.shape[1] // PAIR_W):
        v_ref[:, 2 * PAIR_W * j:2 * PAIR_W * j + PAIR_W] = v[:, PAIR_W * j:PAIR_W * (j + 1)]
        v_ref[:, 2 * PAIR_W * j + PAIR_W:2 * PAIR_W * (j + 1)] = ones


def _premix_qkv(x, mod, row_of_step, g, wqkv, cache_kt, cache_vt, layer_j, tm=512):
    t = x.shape[0]
    steps_per_seq = DEC_SEQ // tm
    chunk = D_MODEL // steps_per_seq
    tok = pl.BlockSpec((tm, D_MODEL), lambda i: (i, 0))
    tok2 = pl.BlockSpec((tm, 2 * D_MODEL), lambda i: (i, 0))
    cache = pl.BlockSpec((None, None, chunk, PAST_LEN),
                         lambda i: (i // steps_per_seq, layer_j, i % steps_per_seq, 0))
    out = jax.ShapeDtypeStruct((t, D_MODEL), BF16)
    out2 = jax.ShapeDtypeStruct((t, 2 * D_MODEL), BF16)
    return pl.pallas_call(
        _premix_qkv_kernel,
        grid=(t // tm,),
        in_specs=[tok, _mod_spec(row_of_step), _const_spec((4, D_MODEL)),
                  _const_spec((D_MODEL, 3 * D_MODEL)), cache, cache],
        out_specs=[tok, tok, tok2,
                   pl.BlockSpec((chunk, PAST_LEN), lambda i: (i, 0)),
                   pl.BlockSpec((PAST_LEN, 2 * chunk),
                                lambda i: (i // steps_per_seq, i % steps_per_seq))],
        out_shape=[out, out, out2,
                   jax.ShapeDtypeStruct((DEC_BATCH * D_MODEL, PAST_LEN), BF16),
                   jax.ShapeDtypeStruct((DEC_BATCH * PAST_LEN, 2 * D_MODEL), BF16)],
        compiler_params=_params(1),
        name="premix_qkv",
    )(x, mod, g, wqkv, cache_kt, cache_vt)


def _build_bias_table(l_ref, bias_ref):
    lane = lax.broadcasted_iota(jnp.int32, (GRID_W, PAIR_W), 1)
    qcol = lax.broadcasted_iota(jnp.int32, (GRID_W, PAIR_W), 0)
    kcol = lane & (GRID_W - 1)
    start = jnp.clip(qcol - WIN_COLS // 2, 0, GRID_W - WIN_COLS)
    in_window = (kcol >= start) & (kcol < start + WIN_COLS)
    lo = lane < GRID_W

    def one_head(h, carry):
        def toeplitz(d, shift):
            row = jnp.broadcast_to(l_ref[h, d:d + 1, :], (GRID_W, PAIR_W))
            return pltpu.roll(row, shift, 1, stride=1, stride_axis=0)

        for d in range(N_DR_PAIRS):
            both = jnp.where(lo, toeplitz(d, GRID_W + 1), toeplitz(d + 1, 1))
            bias_ref[h * N_DR_PAIRS + d] = jnp.where(in_window, both * LOG2E, NEG_INF)
        return carry

    lax.fori_loop(0, N_HEADS, one_head, 0)


def _na_attn_kernel(q_ref, k_ref, v_ref, ckt_ref, cv_ref, l_ref, x_ref, mod_ref, g_ref,
                    wout_ref, o_ref, bias_ref, s_scr, m_scr, p_scr, att_scr, *, rows_per_step):
    @pl.when((pl.program_id(0) == 0) & (pl.program_id(1) == 0))
    def _():
        _build_bias_table(l_ref, bias_ref)

    blk = pl.program_id(1)
    lo = lax.broadcasted_iota(jnp.int32, (GRID_W, PAIR_W), 1) < HEAD_DIM

    def row_geometry(i):
        r = blk * rows_per_step + i
        rs = jnp.clip(r - WIN_ROWS // 2, 0, GRID_ROWS - WIN_ROWS)
        d0 = rs - r + (WIN_ROWS - 1)
        q0 = pl.multiple_of(i * GRID_W, GRID_W)
        k0 = pl.multiple_of(rs * GRID_W, GRID_W)
        return d0, q0, k0

    def scores(i, slot):
        d0, q0, k0 = row_geometry(i)
        for j in range(N_PAIRS):
            lanes = slice(PAIR_W * j, PAIR_W * (j + 1))
            qs = _split_pair(q_ref[pl.ds(q0, GRID_W), lanes], lo)
            bias = jnp.concatenate(
                [jnp.concatenate(
                    [bias_ref[(2 * j) * N_DR_PAIRS + d0 + 2 * jj],
                     bias_ref[(2 * j + 1) * N_DR_PAIRS + d0 + 2 * jj]], axis=0)
                 for jj in range(WIN_ROWS // 2)], axis=1)
            s_loc = _dot_nt(qs, k_ref[pl.ds(k0, N_LOCAL), lanes]) + bias
            s_ctx = _dot(qs, ckt_ref[lanes, :])
            mx = jnp.maximum(jnp.max(s_loc, axis=-1, keepdims=True),
                             jnp.max(s_ctx, axis=-1, keepdims=True))
            s_scr[slot, j, :, 0:N_LOCAL] = s_loc
            s_scr[slot, j, :, N_LOCAL:N_KEYS] = s_ctx
            m_scr[slot, j] = jnp.broadcast_to(mx, (2 * GRID_W, PAIR_W))

    def probs(slot):
        for j in range(N_PAIRS):
            mx = m_scr[slot, j][:, 0:1]
            p_scr[slot, j] = jnp.exp2((s_scr[slot, j] - mx).astype(BF16))

    def values(i, slot):
        _, q0, k0 = row_geometry(i)
        for j in range(N_PAIRS):
            lanes2 = slice(2 * PAIR_W * j, 2 * PAIR_W * (j + 1))
            p = p_scr[slot, j]
            o2 = (_dot(p[:, 0:N_LOCAL], v_ref[pl.ds(k0, N_LOCAL), lanes2])
                  + _dot(p[:, N_LOCAL:N_KEYS], cv_ref[:, lanes2]))
            o = o2[:, 0:PAIR_W] / o2[:, PAIR_W:2 * PAIR_W]
            att_scr[pl.ds(q0, GRID_W), PAIR_W * j:PAIR_W * (j + 1)] = (
                jnp.where(lo, o[:GRID_W], o[GRID_W:]).astype(BF16))

    scores(0, 0)
    probs(0)
    scores(1, 1)

    def two_rows(t, carry):
        i = 2 * t
        values(i - 2, 0)
        probs(1)
        scores(i, 0)
        values(i - 1, 1)
        probs(0)
        scores(i + 1, 1)
        return carry

    lax.fori_loop(1, rows_per_step // 2, two_rows, 0)
    values(rows_per_step - 2, 0)
    probs(1)
    values(rows_per_step - 1, 1)

    gt = mod_ref[...][:, 2 * D_MODEL:3 * D_MODEL]
    mix = _dot(att_scr[...], wout_ref[...])
    o_ref[...] = x_ref[...] + gt * _rms(mix, g_ref[1:2, :])


def _na_attn(q, k, v, ck, cv, rpb, x, mod, row_of_batch, g, wout, rows_per_step=8):
    rpb_rows = jnp.pad(rpb, ((0, 0), (0, 0), (48, 49)), mode="edge")
    tm = rows_per_step * GRID_W
    n_t = DEC_SEQ // tm
    tok = pl.BlockSpec((tm, D_MODEL), lambda b, t: (b * n_t + t, 0))

    def per_batch(rows, width):
        return pl.BlockSpec((rows, width), lambda b, t: (b, 0), pipeline_mode=pl.Buffered(1))

    return pl.pallas_call(
        functools.partial(_na_attn_kernel, rows_per_step=rows_per_step),
        grid=(DEC_BATCH, n_t),
        in_specs=[
            tok,
            per_batch(DEC_SEQ, D_MODEL), per_batch(DEC_SEQ, 2 * D_MODEL),
            per_batch(D_MODEL, PAST_LEN), per_batch(PAST_LEN, 2 * D_MODEL),
            _const_spec((N_HEADS, N_DR, PAIR_W)),
            tok,
            _mod_spec(lambda b, t: row_of_batch(b)),
            _const_spec((4, D_MODEL)),
            _const_spec((D_MODEL, D_MODEL)),
        ],
        out_specs=tok,
        out_shape=jax.ShapeDtypeStruct((DEC_BATCH * DEC_SEQ, D_MODEL), F32),
        scratch_shapes=[
            pltpu.VMEM((N_HEADS * N_DR_PAIRS, GRID_W, PAIR_W), F32),
            pltpu.VMEM((2, N_PAIRS, 2 * GRID_W, N_KEYS), F32),
            pltpu.VMEM((2, N_PAIRS, 2 * GRID_W, PAIR_W), F32),
            pltpu.VMEM((2, N_PAIRS, 2 * GRID_W, N_KEYS), BF16),
            pltpu.VMEM((tm, D_MODEL), BF16),
        ],
        compiler_params=_params(2),
        name="na_attn",
    )(q, k, v, ck, cv, rpb_rows, x, mod, g, wout)


def kernel(x_prompt, x_sample, c, cache_k, cache_v, c_ctx, ada_w, ada_b, norm_g, fourier_w_out,
           na_w_qkv, na_rpb, na_w_out, ffn_w_gate, ffn_w_up, ffn_w_down):
    n_p = BATCH * SEQ
    n_s = DEC_BATCH * DEC_SEQ
    xp = x_prompt.reshape(n_p, D_MODEL)
    xs = x_sample.reshape(n_s, D_MODEL)

    cond = jnp.concatenate(
        [c_ctx[None, :], c, jnp.zeros((COND_ROWS - 1 - DEC_BATCH, D_MODEL), F32)], axis=0)
    mod = _modulation(cond, ada_w, ada_b)

    cos_g, sin_g = _dft_cos_sin(FOURIER_GROUP)
    cs_chan = jnp.asarray(np.concatenate([cos_g, sin_g], axis=1), F32).astype(BF16)
    cos_p, sin_p = (jnp.asarray(a, F32).astype(BF16) for a in _dft_cos_sin(SEQ))
    cos_s, sin_s = (jnp.asarray(_half_spectrum_blocks(a), F32).astype(BF16)
                    for a in _dft_cos_sin(DEC_SEQ))
    rev = jnp.asarray(_reversal_matrix(), F32).astype(BF16)

    tm = 512
    ffn_tm = 1024
    ffn_weights = (ffn_w_gate, ffn_w_up, ffn_w_down)
    cache_kt = jnp.transpose(cache_k, (0, 1, 3, 4, 2)).reshape(DEC_BATCH, -1, D_MODEL, PAST_LEN)
    cache_vt = jnp.transpose(cache_v, (0, 1, 3, 4, 2)).reshape(DEC_BATCH, -1, D_MODEL, PAST_LEN)
    new_kt = new_vt = None
    attn_weights = None
    for layer in range(DEPTH):
        g = norm_g[layer]
        base = layer * COND_ROWS
        prompt_row = lambda *_, base=base: base
        sample_row_of_batch = lambda b, base=base: base + 1 + b
        sample_row_of_tile = lambda i, base=base: base + 1 + i // (DEC_SEQ // tm)
        sample_row_of_ffn_tile = lambda i, base=base: base + 1 + i // (DEC_SEQ // ffn_tm)

        if layer % 2 == 0:
            w_out = fourier_w_out[layer // 2]
            jobs = [(w, layer) for w in ffn_weights] if layer == 0 else []
            xp, cast_p = _fourier_mixer(cos_p, sin_p, cs_chan, xp, mod, prompt_row, g, w_out,
                                        BATCH, SEQ, n_sub=4, tr=SEQ, cast_jobs=jobs[:2])
            xs, cast_s = _fourier_mixer_long(cos_s, sin_s, rev, cs_chan, xs, mod,
                                             sample_row_of_batch, g, w_out, DEC_BATCH, DEC_SEQ,
                                             cast_jobs=jobs[2:])
            if layer == 0:
                wg, wu, wd = list(cast_p) + list(cast_s)
        else:
            j = layer // 2
            if attn_weights is None:
                attn_weights = (na_w_qkv[j].astype(BF16), na_w_out[j].astype(BF16))
            w_qkv, w_out = attn_weights
            attn_weights = None
            xp, new_kt, new_vt = _prompt_attn(xp, mod, base, g, w_qkv, w_out)
            q, k, v, ckt, cvt = _premix_qkv(xs, mod, sample_row_of_tile, g, w_qkv,
                                            cache_kt, cache_vt, j, tm)
            xs = _na_attn(q, k, v, ckt, cvt, na_rpb[j], xs, mod, sample_row_of_batch, g, w_out)

        nxt = layer + 1
        ffn_jobs = [(w, nxt) for w in ffn_weights] if nxt < DEPTH else []
        attn_jobs = ([(na_w_qkv, nxt // 2), (na_w_out, nxt // 2)]
                     if nxt < DEPTH and nxt % 2 == 1 else [])
        xp, cast_ffn = _ffn(xp, mod, prompt_row, g, wg, wu, wd, tm if ffn_jobs else ffn_tm,
                            cast_jobs=ffn_jobs)
        xs, cast_attn = _ffn(xs, mod, sample_row_of_tile if attn_jobs else sample_row_of_ffn_tile,
                             g, wg, wu, wd, tm if attn_jobs else ffn_tm, cast_jobs=attn_jobs)
        if ffn_jobs:
            wg, wu, wd = cast_ffn
        if attn_jobs:
            attn_weights = tuple(cast_attn)

    new_k = jnp.transpose(new_kt, (0, 1, 4, 2, 3))
    new_v = jnp.transpose(new_vt, (0, 1, 4, 2, 3))
    return (xp.reshape(BATCH, SEQ, D_MODEL), xs.reshape(DEC_BATCH, DEC_SEQ, D_MODEL), new_k, new_v)
```

```python
import functools

import numpy as np
import jax
import jax.numpy as jnp
from jax import lax
from jax.experimental import pallas as pl
from jax.experimental.pallas import tpu as pltpu

D_MODEL = 1024
BATCH = 32
SEQ = 256
DEPTH = 2
DEC_BATCH = 2
DEC_SEQ = 2048
PAST_LEN = 512
GRID_W = 64
GRID_ROWS = DEC_SEQ // GRID_W
N_HEADS = 16
HEAD_DIM = D_MODEL // N_HEADS
N_FOURIER_GROUPS = 4
FOURIER_GROUP = D_MODEL // N_FOURIER_GROUPS
WIN_ROWS = 8
WIN_COLS = 16
D_FF = 2816
EPS = 1e-6
NEG_INF = -1e30

N_PAIRS = N_HEADS // 2
PAIR_W = 2 * HEAD_DIM
N_DR = 2 * WIN_ROWS - 1
N_DR_PAIRS = N_DR - 1
N_LOCAL = WIN_ROWS * GRID_W
N_KEYS = N_LOCAL + PAST_LEN
COND_ROWS = 8
LOG2E = 1.4426950408889634
Q_SCALE = HEAD_DIM ** -0.5 * LOG2E

VMEM_LIMIT = 56 * 1024 * 1024
BF16_SUBLANES = 16

F32 = jnp.float32
BF16 = jnp.bfloat16


def _dot(a, b):
    return jnp.dot(a, b, preferred_element_type=F32)


def _dot_nt(a, b):
    return lax.dot_general(a, b, (((1,), (1,)), ((), ())), preferred_element_type=F32)


def _rms(x, g):
    ms = jnp.mean(x * x, axis=-1, keepdims=True)
    return x * lax.rsqrt(ms + EPS) * g


def _silu(x):
    return x / (1.0 + jnp.exp(-x))


def _const_spec(shape):
    return pl.BlockSpec(shape, lambda *_: (0,) * len(shape), pipeline_mode=pl.Buffered(1))


def _mod_spec(row_of_step):
    return pl.BlockSpec((None, 1, 6 * D_MODEL), lambda *idx: (row_of_step(*idx), 0, 0))


def _params(n_axes):
    return pltpu.CompilerParams(
        dimension_semantics=("arbitrary",) * n_axes, vmem_limit_bytes=VMEM_LIMIT)


def _mod_kernel(cond_ref, w_ref, b_ref, o_ref, acc_ref):
    layer, k = pl.program_id(0), pl.program_id(1)

    @pl.when(k == 0)
    def _():
        acc_ref[...] = jnp.broadcast_to(b_ref[pl.ds(layer, 1), :], acc_ref.shape)

    a = _silu(cond_ref[...])
    a_hi = a.astype(BF16)
    a_lo = (a - a_hi.astype(F32)).astype(BF16)
    w = w_ref[...].astype(BF16)
    acc_ref[...] += _dot(a_hi, w) + _dot(a_lo, w)

    @pl.when(k == pl.num_programs(1) - 1)
    def _():
        for r in range(COND_ROWS):
            o_ref[r] = acc_ref[r:r + 1, :]


def _modulation(cond, ada_w, ada_b):
    tk = 256
    n_out = 6 * D_MODEL
    return pl.pallas_call(
        _mod_kernel,
        grid=(DEPTH, D_MODEL // tk),
        in_specs=[
            pl.BlockSpec((COND_ROWS, tk), lambda l, k: (0, k)),
            pl.BlockSpec((None, tk, n_out), lambda l, k: (l, k, 0)),
            pl.BlockSpec((DEPTH, n_out), lambda l, k: (0, 0)),
        ],
        out_specs=pl.BlockSpec((COND_ROWS, 1, n_out), lambda l, k: (l, 0, 0)),
        out_shape=jax.ShapeDtypeStruct((DEPTH * COND_ROWS, 1, n_out), F32),
        scratch_shapes=[pltpu.VMEM((COND_ROWS, n_out), F32)],
        compiler_params=_params(2),
        name="modulation",
    )(cond, ada_w, ada_b)


FFN_SUB_ROWS = 256


def _cast_specs(jobs, n_steps, step_of):
    in_specs, out_specs, out_shapes = [], [], []
    for src, layer in jobs:
        rows, cols = src.shape[1:]
        chunk = rows // n_steps
        assert chunk * n_steps == rows and chunk % BF16_SUBLANES == 0
        in_specs.append(pl.BlockSpec(
            (None, chunk, cols), lambda *idx, layer=layer: (layer, step_of(*idx), 0)))
        out_specs.append(pl.BlockSpec((chunk, cols), lambda *idx: (step_of(*idx), 0)))
        out_shapes.append(jax.ShapeDtypeStruct((rows, cols), BF16))
    return in_specs, out_specs, out_shapes


def _run_cast_jobs(src_refs, dst_refs):
    for src, dst in zip(src_refs, dst_refs):
        dst[...] = src[...].astype(BF16)


def _ffn_kernel(x_ref, mod_ref, g_ref, wg_ref, wu_ref, wd_ref, *rest, n_cast):
    o_ref = rest[n_cast]
    _run_cast_jobs(rest[:n_cast], rest[n_cast + 1:])
    m = mod_ref[...]
    sh = m[:, 3 * D_MODEL:4 * D_MODEL]
    sc = m[:, 4 * D_MODEL:5 * D_MODEL]
    gt = m[:, 5 * D_MODEL:6 * D_MODEL]
    for r0 in range(0, x_ref.shape[0], FFN_SUB_ROWS):
        rows = slice(r0, r0 + FFN_SUB_ROWS)
        x = x_ref[rows, :]
        f = (_rms(x, g_ref[2:3, :]) * (1.0 + sc) + sh).astype(BF16)
        gate = _dot(f, wg_ref[...])
        up = _dot(f, wu_ref[...])
        act = (_silu(gate) * up).astype(BF16)
        y = _dot(act, wd_ref[...])
        o_ref[rows, :] = x + gt * _rms(y, g_ref[3:4, :])


def _ffn(x, mod, row_of_step, g, wg, wu, wd, tm, cast_jobs=()):
    t = x.shape[0]
    tok = pl.BlockSpec((tm, D_MODEL), lambda i: (i, 0))
    cast_in, cast_out, cast_shapes = _cast_specs(cast_jobs, t // tm, lambda i: i)
    outs = pl.pallas_call(
        functools.partial(_ffn_kernel, n_cast=len(cast_jobs)),
        grid=(t // tm,),
        in_specs=[
            tok,
            _mod_spec(row_of_step),
            _const_spec((4, D_MODEL)),
            _const_spec((D_MODEL, D_FF)),
            _const_spec((D_MODEL, D_FF)),
            _const_spec((D_FF, D_MODEL)),
        ] + cast_in,
        out_specs=[tok] + cast_out,
        out_shape=[jax.ShapeDtypeStruct((t, D_MODEL), F32)] + cast_shapes,
        compiler_params=_params(1),
        name="ffn",
    )(x, mod, g, wg, wu, wd, *[src for src, _ in cast_jobs])
    return outs[0], outs[1:]


def _dft_cos_sin(n):
    j = np.arange(n)
    ang = 2.0 * np.pi * ((j[:, None] * j[None, :]) % n) / n
    return np.cos(ang), np.sin(ang)


PREMIX_CHUNK = 512


def _premix_channel_dft(x_ref, n_rows, sc, sh, g_ref, cs_ref, xc_scr, xs_scr):
    cs = cs_ref[...]
    for r0 in range(0, n_rows, PREMIX_CHUNK):
        rows = slice(r0, r0 + PREMIX_CHUNK)
        h = (_rms(x_ref[rows, :], g_ref[0:1, :]) * (1.0 + sc) + sh).astype(BF16)
        for j in range(N_FOURIER_GROUPS):
            lanes = slice(FOURIER_GROUP * j, FOURIER_GROUP * (j + 1))
            r = _dot(h[:, lanes], cs)
            xc_scr[rows, lanes] = r[:, :FOURIER_GROUP].astype(BF16)
            xs_scr[rows, lanes] = r[:, FOURIER_GROUP:].astype(BF16)


def _fourier_kernel(c_ref, s_ref, cs_ref, x_ref, mod_ref, g_ref, w_ref, *rest,
                    n_sub, seq, tr, n_cast):
    o_ref = rest[n_cast]
    xc_scr, xs_scr, f_scr, w_scr = rest[2 * n_cast + 1:]
    _run_cast_jobs(rest[:n_cast], rest[n_cast + 1:2 * n_cast + 1])
    t = pl.program_id(1)
    m = mod_ref[...]
    sh = m[:, 0:D_MODEL]
    sc = m[:, D_MODEL:2 * D_MODEL]
    gt = m[:, 2 * D_MODEL:3 * D_MODEL]

    @pl.when((pl.program_id(0) == 0) & (t == 0))
    def _():
        w_scr[...] = w_ref[...].astype(BF16)

    @pl.when(t == 0)
    def _():
        _premix_channel_dft(x_ref, n_sub * seq, sc, sh, g_ref, cs_ref, xc_scr, xs_scr)

    for b in range(n_sub):
        src = slice(seq * b, seq * (b + 1))
        f = _dot(c_ref[...], xc_scr[src, :]) - _dot(s_ref[...], xs_scr[src, :])
        f_scr[tr * b:tr * (b + 1), :] = f.astype(BF16)
    mix = _dot(f_scr[...], w_scr[...])
    if n_sub == 1:
        x = x_ref[pl.ds(pl.multiple_of(t * tr, tr), tr), :]
    else:
        x = x_ref[...]
    o_ref[...] = x + gt * _rms(mix, g_ref[1:2, :])


def _fourier_mixer(cmat, smat, cs, x, mod, row_of_group, g, w, n_seqs, seq, n_sub, tr,
                   cast_jobs=()):
    assert n_sub == 1 or tr == seq
    n_t = seq // tr
    n_groups = n_seqs // n_sub
    rows = n_sub * seq
    cast_in, cast_out, cast_shapes = _cast_specs(
        cast_jobs, n_groups * n_t, lambda b, t: b * n_t + t)
    outs = pl.pallas_call(
        functools.partial(_fourier_kernel, n_sub=n_sub, seq=seq, tr=tr, n_cast=len(cast_jobs)),
        grid=(n_groups, n_t),
        in_specs=[
            pl.BlockSpec((tr, seq), lambda b, t: (t, 0)),
            pl.BlockSpec((tr, seq), lambda b, t: (t, 0)),
            _const_spec((FOURIER_GROUP, 2 * FOURIER_GROUP)),
            pl.BlockSpec((rows, D_MODEL), lambda b, t: (b, 0)),
            _mod_spec(lambda b, t: row_of_group(b)),
            _const_spec((4, D_MODEL)),
            _const_spec((D_MODEL, D_MODEL)),
        ] + cast_in,
        out_specs=[pl.BlockSpec((n_sub * tr, D_MODEL), lambda b, t: (b * n_t + t, 0))] + cast_out,
        out_shape=[jax.ShapeDtypeStruct((n_seqs * seq, D_MODEL), F32)] + cast_shapes,
        scratch_shapes=[pltpu.VMEM((rows, D_MODEL), BF16), pltpu.VMEM((rows, D_MODEL), BF16),
                        pltpu.VMEM((n_sub * tr, D_MODEL), BF16),
                        pltpu.VMEM((D_MODEL, D_MODEL), BF16)],
        compiler_params=_params(2),
        name="fourier_mixer",
    )(cmat, smat, cs, x, mod, g, w, *[src for src, _ in cast_jobs])
    return outs[0], outs[1:]


HERM_TILE = 512
HERM_ROWS = HERM_TILE + BF16_SUBLANES


def _half_spectrum_blocks(mat):
    n_half = mat.shape[0] // 2 // HERM_TILE
    return np.stack([mat[HERM_TILE * t:HERM_TILE * t + HERM_ROWS] for t in range(n_half)])


def _reversal_matrix():
    rev = np.zeros((HERM_TILE, HERM_ROWS), np.float32)
    rev[np.arange(HERM_TILE), HERM_TILE - np.arange(HERM_TILE)] = 1.0
    return rev


def _fourier_long_kernel(ch_ref, sh_ref, rev_ref, cs_ref, x_ref, mod_ref, g_ref, w_ref, *rest,
                         seq, n_cast):
    o_ref = rest[n_cast]
    xc_scr, xs_scr, pq_scr, f_scr, w_scr = rest[2 * n_cast + 1:]
    _run_cast_jobs(rest[:n_cast], rest[n_cast + 1:2 * n_cast + 1])
    b, t, u = pl.program_id(0), pl.program_id(1), pl.program_id(2)
    m = mod_ref[...]
    sh = m[:, 0:D_MODEL]
    sc = m[:, D_MODEL:2 * D_MODEL]
    gt = m[:, 2 * D_MODEL:3 * D_MODEL]

    @pl.when((b == 0) & (t == 0) & (u == 0))
    def _():
        w_scr[...] = w_ref[...].astype(BF16)

    @pl.when((t == 0) & (u == 0))
    def _():
        _premix_channel_dft(x_ref, seq, sc, sh, g_ref, cs_ref, xc_scr, xs_scr)

    @pl.when(u == 0)
    def _():
        p = _dot(ch_ref[...], xc_scr[...])
        q = _dot(sh_ref[...], xs_scr[...])
        f_scr[...] = (p - q)[0:HERM_TILE].astype(BF16)
        pq_scr[...] = (p + q).astype(BF16)

    @pl.when(u == 1)
    def _():
        f_scr[...] = _dot(rev_ref[...], pq_scr[...]).astype(BF16)

    n_tiles = seq // HERM_TILE
    tile = t + u * (n_tiles - 1 - 2 * t)
    mix = _dot(f_scr[...], w_scr[...])
    x = x_ref[pl.ds(pl.multiple_of(tile * HERM_TILE, HERM_TILE), HERM_TILE), :]
    o_ref[...] = x + gt * _rms(mix, g_ref[1:2, :])


def _fourier_mixer_long(cos_h, sin_h, rev, cs, x, mod, row_of_seq, g, w, n_seqs, seq, cast_jobs=()):
    n_tiles = seq // HERM_TILE
    n_half = n_tiles // 2
    tile_of = lambda t, u: t + u * (n_tiles - 1 - 2 * t)
    half = pl.BlockSpec((None, HERM_ROWS, seq), lambda b, t, u: (t, 0, 0))
    cast_in, cast_out, cast_shapes = _cast_specs(
        cast_jobs, n_seqs * n_tiles, lambda b, t, u: (b * n_half + t) * 2 + u)
    outs = pl.pallas_call(
        functools.partial(_fourier_long_kernel, seq=seq, n_cast=len(cast_jobs)),
        grid=(n_seqs, n_half, 2),
        in_specs=[
            half, half,
            _const_spec((HERM_TILE, HERM_ROWS)),
            _const_spec((FOURIER_GROUP, 2 * FOURIER_GROUP)),
            pl.BlockSpec((seq, D_MODEL), lambda b, t, u: (b, 0)),
            _mod_spec(lambda b, t, u: row_of_seq(b)),
            _const_spec((4, D_MODEL)),
            _const_spec((D_MODEL, D_MODEL)),
        ] + cast_in,
        out_specs=[pl.BlockSpec((HERM_TILE, D_MODEL),
                                lambda b, t, u: (b * n_tiles + tile_of(t, u), 0))] + cast_out,
        out_shape=[jax.ShapeDtypeStruct((n_seqs * seq, D_MODEL), F32)] + cast_shapes,
        scratch_shapes=[pltpu.VMEM((seq, D_MODEL), BF16), pltpu.VMEM((seq, D_MODEL), BF16),
                        pltpu.VMEM((HERM_ROWS, D_MODEL), BF16),
                        pltpu.VMEM((HERM_TILE, D_MODEL), BF16),
                        pltpu.VMEM((D_MODEL, D_MODEL), BF16)],
        compiler_params=_params(3),
        name="fourier_mixer_long",
    )(cos_h, sin_h, rev, cs, x, mod, g, w, *[src for src, _ in cast_jobs])
    return outs[0], outs[1:]


def _split_pair(q, lo):
    zero = jnp.zeros_like(q)
    return jnp.concatenate([jnp.where(lo, q, zero), jnp.where(lo, zero, q)], axis=0)


def _prompt_attn_kernel(x_ref, mod_ref, g_ref, wq_ref, wk_ref, wv_ref, wout_ref,
                        o_ref, kt_ref, vt_ref, wkt_ref, wvt_ref, h_scr, q_scr, kt_scr, vt_scr,
                        att_scr, *, n_seq):
    @pl.when(pl.program_id(0) == 0)
    def _():
        wkt_ref[...] = wk_ref[...].T
        wvt_ref[...] = wv_ref[...].T

    x = x_ref[...]
    m = mod_ref[...]
    sh = m[:, 0:D_MODEL]
    sc = m[:, D_MODEL:2 * D_MODEL]
    gt = m[:, 2 * D_MODEL:3 * D_MODEL]
    h_scr[...] = (_rms(x, g_ref[0:1, :]) * (1.0 + sc) + sh).astype(BF16)
    q_scr[...] = (_dot(h_scr[...], wq_ref[...]) * Q_SCALE).astype(BF16)

    lo = lax.broadcasted_iota(jnp.int32, (SEQ, PAIR_W), 1) < HEAD_DIM
    ones = jnp.ones((PAIR_W, SEQ), BF16)
    for b in range(n_seq):
        rows = slice(SEQ * b, SEQ * (b + 1))
        kt = _dot_nt(wkt_ref[...], h_scr[rows, :])
        vt = _dot_nt(wvt_ref[...], h_scr[rows, :])
        kt_ref[b] = kt.reshape(N_HEADS, HEAD_DIM, SEQ)
        vt_ref[b] = vt.reshape(N_HEADS, HEAD_DIM, SEQ)
        kt_scr[b] = kt.astype(BF16)
        vt_scr[b] = vt.astype(BF16)
        for j in range(N_PAIRS):
            lanes = slice(PAIR_W * j, PAIR_W * (j + 1))
            qs = _split_pair(q_scr[rows, lanes], lo)
            s = _dot(qs, kt_scr[b, lanes, :])
            p = jnp.exp2((s - jnp.max(s, axis=-1, keepdims=True)).astype(BF16))
            o2 = _dot_nt(p, jnp.concatenate([vt_scr[b, lanes, :], ones], axis=0))
            o = o2[:, 0:PAIR_W] / o2[:, PAIR_W:2 * PAIR_W]
            att_scr[rows, lanes] = jnp.where(lo, o[:SEQ], o[SEQ:]).astype(BF16)

    mix = _dot(att_scr[...], wout_ref[...])
    o_ref[...] = x + gt * _rms(mix, g_ref[1:2, :])


def _prompt_attn(x, mod, row, g, wqkv, wout, n_seq=4):
    t = x.shape[0]
    tm = n_seq * SEQ
    tok = pl.BlockSpec((tm, D_MODEL), lambda i: (i, 0))
    out = jax.ShapeDtypeStruct((t, D_MODEL), F32)
    cache = pl.BlockSpec((n_seq, None, N_HEADS, HEAD_DIM, SEQ), lambda i: (i, 0, 0, 0, 0))
    cache_out = jax.ShapeDtypeStruct((t // SEQ, 1, N_HEADS, HEAD_DIM, SEQ), F32)

    def qkv_part(n):
        return pl.BlockSpec((D_MODEL, D_MODEL), lambda i: (0, n), pipeline_mode=pl.Buffered(1))

    return pl.pallas_call(
        functools.partial(_prompt_attn_kernel, n_seq=n_seq),
        grid=(t // tm,),
        in_specs=[tok, _mod_spec(lambda i: row), _const_spec((4, D_MODEL)),
                  qkv_part(0), qkv_part(1), qkv_part(2), _const_spec((D_MODEL, D_MODEL))],
        out_specs=[tok, cache, cache],
        out_shape=[out, cache_out, cache_out],
        scratch_shapes=[pltpu.VMEM((D_MODEL, D_MODEL), BF16), pltpu.VMEM((D_MODEL, D_MODEL), BF16),
                        pltpu.VMEM((tm, D_MODEL), BF16), pltpu.VMEM((tm, D_MODEL), BF16),
                        pltpu.VMEM((n_seq, D_MODEL, SEQ), BF16),
                        pltpu.VMEM((n_seq, D_MODEL, SEQ), BF16),
                        pltpu.VMEM((tm, D_MODEL), BF16)],
        compiler_params=_params(1),
        name="prompt_attn",
    )(x, mod, g, wqkv, wqkv, wqkv, wout)


def _premix_qkv_kernel(x_ref, mod_ref, g_ref, wqkv_ref, ckt_ref, cvt_ref, l_ref,
                       q_ref, k_ref, v_ref, ck_ref, cv_ref, bias_ref, *, heads_per_step):
    _build_bias_tiles(l_ref, pl.program_id(0) * heads_per_step, heads_per_step, bias_ref)

    ck_ref[...] = ckt_ref[...].astype(BF16)
    _store_values_with_ones(cv_ref, cvt_ref[...].T.astype(BF16))

    x = x_ref[...]
    m = mod_ref[...]
    sh = m[:, 0:D_MODEL]
    sc = m[:, D_MODEL:2 * D_MODEL]
    h = (_rms(x, g_ref[0:1, :]) * (1.0 + sc) + sh).astype(BF16)
    qkv = _dot(h, wqkv_ref[...])
    q_ref[...] = (qkv[:, 0:D_MODEL] * Q_SCALE).astype(BF16)
    k_ref[...] = qkv[:, D_MODEL:2 * D_MODEL].astype(BF16)
    _store_values_with_ones(v_ref, qkv[:, 2 * D_MODEL:3 * D_MODEL].astype(BF16))


def _store_values_with_ones(v_ref, v):
    ones = jnp.ones((v.shape[0], PAIR_W), BF16)
    for j in range(v.shape[1] // PAIR_W):
        v_ref[:, 2 * PAIR_W * j:2 * PAIR_W * j + PAIR_W] = v[:, PAIR_W * j:PAIR_W * (j + 1)]
        v_ref[:, 2 * PAIR_W * j + PAIR_W:2 * PAIR_W * (j + 1)] = ones


def _premix_qkv(x, mod, row_of_step, g, wqkv, cache_kt, cache_vt, layer_j, rpb, tm=512):
    t = x.shape[0]
    heads_per_step = N_HEADS // (t // tm)
    rpb_rows = jnp.pad(rpb, ((0, 0), (0, 0), (48, 49)), mode="edge")
    steps_per_seq = DEC_SEQ // tm
    chunk = D_MODEL // steps_per_seq
    tok = pl.BlockSpec((tm, D_MODEL), lambda i: (i, 0))
    tok2 = pl.BlockSpec((tm, 2 * D_MODEL), lambda i: (i, 0))
    cache = pl.BlockSpec((None, None, chunk, PAST_LEN),
                         lambda i: (i // steps_per_seq, layer_j, i % steps_per_seq, 0))
    out = jax.ShapeDtypeStruct((t, D_MODEL), BF16)
    out2 = jax.ShapeDtypeStruct((t, 2 * D_MODEL), BF16)
    return pl.pallas_call(
        functools.partial(_premix_qkv_kernel, heads_per_step=heads_per_step),
        grid=(t // tm,),
        in_specs=[tok, _mod_spec(row_of_step), _const_spec((4, D_MODEL)),
                  _const_spec((D_MODEL, 3 * D_MODEL)), cache, cache,
                  _const_spec((N_HEADS, N_DR, PAIR_W))],
        out_specs=[tok, tok, tok2,
                   pl.BlockSpec((chunk, PAST_LEN), lambda i: (i, 0)),
                   pl.BlockSpec((PAST_LEN, 2 * chunk),
                                lambda i: (i // steps_per_seq, i % steps_per_seq)),
                   pl.BlockSpec((heads_per_step * N_DR_PAIRS, GRID_W, PAIR_W),
                                lambda i: (i, 0, 0))],
        out_shape=[out, out, out2,
                   jax.ShapeDtypeStruct((DEC_BATCH * D_MODEL, PAST_LEN), BF16),
                   jax.ShapeDtypeStruct((DEC_BATCH * PAST_LEN, 2 * D_MODEL), BF16),
                   jax.ShapeDtypeStruct((N_HEADS * N_DR_PAIRS, GRID_W, PAIR_W), F32)],
        compiler_params=_params(1),
        name="premix_qkv",
    )(x, mod, g, wqkv, cache_kt, cache_vt, rpb_rows)


def _build_bias_tiles(l_ref, first_head, n_heads, bias_ref):
    lane = lax.broadcasted_iota(jnp.int32, (GRID_W, PAIR_W), 1)
    qcol = lax.broadcasted_iota(jnp.int32, (GRID_W, PAIR_W), 0)
    kcol = lane & (GRID_W - 1)
    start = jnp.clip(qcol - WIN_COLS // 2, 0, GRID_W - WIN_COLS)
    in_window = (kcol >= start) & (kcol < start + WIN_COLS)
    lo = lane < GRID_W

    for hh in range(n_heads):
        def toeplitz(d, shift):
            row = jnp.broadcast_to(l_ref[first_head + hh, d:d + 1, :], (GRID_W, PAIR_W))
            return pltpu.roll(row, shift, 1, stride=1, stride_axis=0)

        for d in range(N_DR_PAIRS):
            both = jnp.where(lo, toeplitz(d, GRID_W + 1), toeplitz(d + 1, 1))
            bias_ref[hh * N_DR_PAIRS + d] = jnp.where(in_window, both * LOG2E, NEG_INF)


def _kv_window_start(blk, rows_per_step):
    return jnp.clip(blk * rows_per_step - WIN_ROWS // 2, 0,
                    GRID_ROWS - (rows_per_step + WIN_ROWS))


def _na_attn_kernel(q_ref, k_ref, v_ref, ckt_ref, cv_ref, bias_ref, x_ref, mod_ref, g_ref,
                    wout_ref, o_ref, s_scr, m_scr, p_scr, att_scr, *, rows_per_step):
    blk = pl.program_id(1)
    lo = lax.broadcasted_iota(jnp.int32, (GRID_W, PAIR_W), 1) < HEAD_DIM

    win_start = _kv_window_start(blk, rows_per_step)

    def row_geometry(i):
        r = blk * rows_per_step + i
        rs = jnp.clip(r - WIN_ROWS // 2, 0, GRID_ROWS - WIN_ROWS)
        d0 = rs - r + (WIN_ROWS - 1)
        q0 = pl.multiple_of(i * GRID_W, GRID_W)
        k0 = pl.multiple_of((rs - win_start) * GRID_W, GRID_W)
        return d0, q0, k0

    def scores(i, slot):
        d0, q0, k0 = row_geometry(i)
        for j in range(N_PAIRS):
            lanes = slice(PAIR_W * j, PAIR_W * (j + 1))
            qs = _split_pair(q_ref[pl.ds(q0, GRID_W), lanes], lo)
            bias = jnp.concatenate(
                [jnp.concatenate(
                    [bias_ref[(2 * j) * N_DR_PAIRS + d0 + 2 * jj],
                     bias_ref[(2 * j + 1) * N_DR_PAIRS + d0 + 2 * jj]], axis=0)
                 for jj in range(WIN_ROWS // 2)], axis=1)
            s_loc = _dot_nt(qs, k_ref[pl.ds(k0, N_LOCAL), lanes]) + bias
            s_ctx = _dot(qs, ckt_ref[lanes, :])
            mx = jnp.maximum(jnp.max(s_loc, axis=-1, keepdims=True),
                             jnp.max(s_ctx, axis=-1, keepdims=True))
            s_scr[slot, j, :, 0:N_LOCAL] = s_loc
            s_scr[slot, j, :, N_LOCAL:N_KEYS] = s_ctx
            m_scr[slot, j] = jnp.broadcast_to(mx, (2 * GRID_W, PAIR_W))

    def probs(slot):
        for j in range(N_PAIRS):
            mx = m_scr[slot, j][:, 0:1]
            p_scr[slot, j] = jnp.exp2((s_scr[slot, j] - mx).astype(BF16))

    def values(i, slot):
        _, q0, k0 = row_geometry(i)
        for j in range(N_PAIRS):
            lanes2 = slice(2 * PAIR_W * j, 2 * PAIR_W * (j + 1))
            p = p_scr[slot, j]
            o2 = (_dot(p[:, 0:N_LOCAL], v_ref[pl.ds(k0, N_LOCAL), lanes2])
                  + _dot(p[:, N_LOCAL:N_KEYS], cv_ref[:, lanes2]))
            o = o2[:, 0:PAIR_W] / o2[:, PAIR_W:2 * PAIR_W]
            att_scr[pl.ds(q0, GRID_W), PAIR_W * j:PAIR_W * (j + 1)] = (
                jnp.where(lo, o[:GRID_W], o[GRID_W:]).astype(BF16))

    scores(0, 0)
    probs(0)
    scores(1, 1)

    def two_rows(t, carry):
        i = 2 * t
        values(i - 2, 0)
        probs(1)
        scores(i, 0)
        values(i - 1, 1)
        probs(0)
        scores(i + 1, 1)
        return carry

    lax.fori_loop(1, rows_per_step // 2, two_rows, 0)
    values(rows_per_step - 2, 0)
    probs(1)
    values(rows_per_step - 1, 1)

    gt = mod_ref[...][:, 2 * D_MODEL:3 * D_MODEL]
    mix = _dot(att_scr[...], wout_ref[...])
    o_ref[...] = x_ref[...] + gt * _rms(mix, g_ref[1:2, :])


def _na_attn(q, k, v, ck, cv, bias, x, mod, row_of_batch, g, wout, rows_per_step=8):
    tm = rows_per_step * GRID_W
    n_t = DEC_SEQ // tm
    tok = pl.BlockSpec((tm, D_MODEL), lambda b, t: (b * n_t + t, 0))

    def per_batch(rows, width):
        return pl.BlockSpec((rows, width), lambda b, t: (b, 0), pipeline_mode=pl.Buffered(1))

    def kv_window(width):
        def start(b, t):
            row = _kv_window_start(t, rows_per_step)
            return (pl.multiple_of(b * DEC_SEQ + row * GRID_W, GRID_W), 0)
        return pl.BlockSpec(
            (pl.Element((rows_per_step + WIN_ROWS) * GRID_W), pl.Element(width)), start)

    return pl.pallas_call(
        functools.partial(_na_attn_kernel, rows_per_step=rows_per_step),
        grid=(DEC_BATCH, n_t),
        in_specs=[
            tok,
            kv_window(D_MODEL), kv_window(2 * D_MODEL),
            per_batch(D_MODEL, PAST_LEN), per_batch(PAST_LEN, 2 * D_MODEL),
            _const_spec((N_HEADS * N_DR_PAIRS, GRID_W, PAIR_W)),
            tok,
            _mod_spec(lambda b, t: row_of_batch(b)),
            _const_spec((4, D_MODEL)),
            _const_spec((D_MODEL, D_MODEL)),
        ],
        out_specs=tok,
        out_shape=jax.ShapeDtypeStruct((DEC_BATCH * DEC_SEQ, D_MODEL), F32),
        scratch_shapes=[
            pltpu.VMEM((2, N_PAIRS, 2 * GRID_W, N_KEYS), F32),
            pltpu.VMEM((2, N_PAIRS, 2 * GRID_W, PAIR_W), F32),
            pltpu.VMEM((2, N_PAIRS, 2 * GRID_W, N_KEYS), BF16),
            pltpu.VMEM((tm, D_MODEL), BF16),
        ],
        compiler_params=_params(2),
        name="na_attn",
    )(q, k, v, ck, cv, bias, x, mod, g, wout)


def kernel(x_prompt, x_sample, c, cache_k, cache_v, c_ctx, ada_w, ada_b, norm_g, fourier_w_out,
           na_w_qkv, na_rpb, na_w_out, ffn_w_gate, ffn_w_up, ffn_w_down):
    n_p = BATCH * SEQ
    n_s = DEC_BATCH * DEC_SEQ
    xp = x_prompt.reshape(n_p, D_MODEL)
    xs = x_sample.reshape(n_s, D_MODEL)

    cond = jnp.concatenate(
        [c_ctx[None, :], c, jnp.zeros((COND_ROWS - 1 - DEC_BATCH, D_MODEL), F32)], axis=0)
    mod = _modulation(cond, ada_w, ada_b)

    cos_g, sin_g = _dft_cos_sin(FOURIER_GROUP)
    cs_chan = jnp.asarray(np.concatenate([cos_g, sin_g], axis=1), F32).astype(BF16)
    cos_p, sin_p = (jnp.asarray(a, F32).astype(BF16) for a in _dft_cos_sin(SEQ))
    cos_s, sin_s = (jnp.asarray(_half_spectrum_blocks(a), F32).astype(BF16)
                    for a in _dft_cos_sin(DEC_SEQ))
    rev = jnp.asarray(_reversal_matrix(), F32).astype(BF16)

    tm = 512
    ffn_tm = 1024
    ffn_weights = (ffn_w_gate, ffn_w_up, ffn_w_down)
    cache_kt = jnp.transpose(cache_k, (0, 1, 3, 4, 2)).reshape(DEC_BATCH, -1, D_MODEL, PAST_LEN)
    cache_vt = jnp.transpose(cache_v, (0, 1, 3, 4, 2)).reshape(DEC_BATCH, -1, D_MODEL, PAST_LEN)
    new_kt = new_vt = None
    attn_weights = None
    for layer in range(DEPTH):
        g = norm_g[layer]
        base = layer * COND_ROWS
        prompt_row = lambda *_, base=base: base
        sample_row_of_batch = lambda b, base=base: base + 1 + b
        sample_row_of_tile = lambda i, base=base: base + 1 + i // (DEC_SEQ // tm)
        sample_row_of_ffn_tile = lambda i, base=base: base + 1 + i // (DEC_SEQ // ffn_tm)

        if layer % 2 == 0:
            w_out = fourier_w_out[layer // 2]
            jobs = [(w, layer) for w in ffn_weights] if layer == 0 else []
            xp, cast_p = _fourier_mixer(cos_p, sin_p, cs_chan, xp, mod, prompt_row, g, w_out,
                                        BATCH, SEQ, n_sub=4, tr=SEQ, cast_jobs=jobs[:2])
            xs, cast_s = _fourier_mixer_long(cos_s, sin_s, rev, cs_chan, xs, mod,
                                             sample_row_of_batch, g, w_out, DEC_BATCH, DEC_SEQ,
                                             cast_jobs=jobs[2:])
            if layer == 0:
                wg, wu, wd = list(cast_p) + list(cast_s)
        else:
            j = layer // 2
            if attn_weights is None:
                attn_weights = (na_w_qkv[j].astype(BF16), na_w_out[j].astype(BF16))
            w_qkv, w_out = attn_weights
            attn_weights = None
            xp, new_kt, new_vt = _prompt_attn(xp, mod, base, g, w_qkv, w_out)
            q, k, v, ckt, cvt, bias = _premix_qkv(xs, mod, sample_row_of_tile, g, w_qkv,
                                                  cache_kt, cache_vt, j, na_rpb[j], tm)
            xs = _na_attn(q, k, v, ckt, cvt, bias, xs, mod, sample_row_of_batch, g, w_out)

        nxt = layer + 1
        ffn_jobs = [(w, nxt) for w in ffn_weights] if nxt < DEPTH else []
        attn_jobs = ([(na_w_qkv, nxt // 2), (na_w_out, nxt // 2)]
                     if nxt < DEPTH and nxt % 2 == 1 else [])
        xp, cast_ffn = _ffn(xp, mod, prompt_row, g, wg, wu, wd, tm if ffn_jobs else ffn_tm,
                            cast_jobs=ffn_jobs)
        xs, cast_attn = _ffn(xs, mod, sample_row_of_tile if attn_jobs else sample_row_of_ffn_tile,
                             g, wg, wu, wd, tm if attn_jobs else ffn_tm, cast_jobs=attn_jobs)
        if ffn_jobs:
            wg, wu, wd = cast_ffn
        if attn_jobs:
            attn_weights = tuple(cast_attn)

    new_k = jnp.transpose(new_kt, (0, 1, 4, 2, 3))
    new_v = jnp.transpose(new_vt, (0, 1, 4, 2, 3))
    return (xp.reshape(BATCH, SEQ, D_MODEL), xs.reshape(DEC_BATCH, DEC_SEQ, D_MODEL), new_k, new_v)
```

```python
import functools

import numpy as np
import jax
import jax.numpy as jnp
from jax import lax
from jax.experimental import pallas as pl
from jax.experimental.pallas import tpu as pltpu

D_MODEL = 1024
BATCH = 32
SEQ = 256
DEPTH = 2
DEC_BATCH = 2
DEC_SEQ = 2048
PAST_LEN = 512
GRID_W = 64
GRID_ROWS = DEC_SEQ // GRID_W
N_HEADS = 16
HEAD_DIM = D_MODEL // N_HEADS
N_FOURIER_GROUPS = 4
FOURIER_GROUP = D_MODEL // N_FOURIER_GROUPS
WIN_ROWS = 8
WIN_COLS = 16
D_FF = 2816
EPS = 1e-6
NEG_INF = -1e30

N_PAIRS = N_HEADS // 2
PAIR_W = 2 * HEAD_DIM
N_DR = 2 * WIN_ROWS - 1
N_DR_PAIRS = N_DR - 1
N_LOCAL = WIN_ROWS * GRID_W
N_KEYS = N_LOCAL + PAST_LEN
COND_ROWS = 8
LOG2E = 1.4426950408889634
Q_SCALE = HEAD_DIM ** -0.5 * LOG2E

VMEM_LIMIT = 56 * 1024 * 1024
BF16_SUBLANES = 16

F32 = jnp.float32
BF16 = jnp.bfloat16


def _dot(a, b):
    return jnp.dot(a, b, preferred_element_type=F32)


def _dot_nt(a, b):
    return lax.dot_general(a, b, (((1,), (1,)), ((), ())), preferred_element_type=F32)


def _rms(x, g):
    ms = jnp.mean(x * x, axis=-1, keepdims=True)
    return x * lax.rsqrt(ms + EPS) * g


def _silu(x):
    return x / (1.0 + jnp.exp(-x))


def _const_spec(shape):
    return pl.BlockSpec(shape, lambda *_: (0,) * len(shape), pipeline_mode=pl.Buffered(1))


def _mod_spec(row_of_step):
    return pl.BlockSpec((None, 1, 6 * D_MODEL), lambda *idx: (row_of_step(*idx), 0, 0))


def _params(n_axes):
    return pltpu.CompilerParams(
        dimension_semantics=("arbitrary",) * n_axes, vmem_limit_bytes=VMEM_LIMIT)


def _mod_accumulate(cond_ref, w_ref, b_ref, o_ref, acc_ref, layer, step, n_steps):
    @pl.when(step == 0)
    def _():
        acc_ref[...] = jnp.broadcast_to(b_ref[layer:layer + 1, :], acc_ref.shape)

    a = _silu(cond_ref[...])
    a_hi = a.astype(BF16)
    a_lo = (a - a_hi.astype(F32)).astype(BF16)
    w = w_ref[...].astype(BF16)
    acc_ref[...] += _dot(a_hi, w) + _dot(a_lo, w)

    @pl.when(step == n_steps - 1)
    def _():
        for r in range(COND_ROWS):
            o_ref[r] = acc_ref[r:r + 1, :]


def _mod_specs(layer, n_steps, step_of):
    tk = D_MODEL // n_steps
    n_out = 6 * D_MODEL
    in_specs = [
        pl.BlockSpec((COND_ROWS, tk), lambda *idx: (0, step_of(*idx))),
        pl.BlockSpec((None, tk, n_out), lambda *idx: (layer, step_of(*idx), 0)),
        pl.BlockSpec((DEPTH, n_out), lambda *idx: (0, 0)),
    ]
    out_spec = pl.BlockSpec((COND_ROWS, 1, n_out), lambda *idx: (0, 0, 0))
    out_shape = jax.ShapeDtypeStruct((COND_ROWS, 1, n_out), F32)
    scratch = pltpu.VMEM((COND_ROWS, n_out), F32)
    return in_specs, out_spec, out_shape, scratch


def _mod_kernel(cond_ref, w_ref, b_ref, o_ref, acc_ref, *, layer):
    _mod_accumulate(cond_ref, w_ref, b_ref, o_ref, acc_ref, layer,
                    pl.program_id(0), pl.num_programs(0))


def _modulation(cond, ada_w, ada_b, layer):
    n_steps = 4
    in_specs, out_spec, out_shape, scratch = _mod_specs(layer, n_steps, lambda k: k)
    return pl.pallas_call(
        functools.partial(_mod_kernel, layer=layer),
        grid=(n_steps,),
        in_specs=in_specs,
        out_specs=out_spec,
        out_shape=out_shape,
        scratch_shapes=[scratch],
        compiler_params=_params(1),
        name="modulation",
    )(cond, ada_w, ada_b)


FFN_SUB_ROWS = 256


def _cast_specs(jobs, n_steps, step_of):
    in_specs, out_specs, out_shapes = [], [], []
    for src, layer in jobs:
        rows, cols = src.shape[1:]
        chunk = rows // n_steps
        assert chunk * n_steps == rows and chunk % BF16_SUBLANES == 0
        in_specs.append(pl.BlockSpec(
            (None, chunk, cols), lambda *idx, layer=layer: (layer, step_of(*idx), 0)))
        out_specs.append(pl.BlockSpec((chunk, cols), lambda *idx: (step_of(*idx), 0)))
        out_shapes.append(jax.ShapeDtypeStruct((rows, cols), BF16))
    return in_specs, out_specs, out_shapes


def _run_cast_jobs(src_refs, dst_refs):
    for src, dst in zip(src_refs, dst_refs):
        dst[...] = src[...].astype(BF16)


def _ffn_kernel(x_ref, mod_ref, g_ref, wg_ref, wu_ref, wd_ref, *rest, n_cast):
    o_ref = rest[n_cast]
    _run_cast_jobs(rest[:n_cast], rest[n_cast + 1:])
    m = mod_ref[...]
    sh = m[:, 3 * D_MODEL:4 * D_MODEL]
    sc = m[:, 4 * D_MODEL:5 * D_MODEL]
    gt = m[:, 5 * D_MODEL:6 * D_MODEL]
    for r0 in range(0, x_ref.shape[0], FFN_SUB_ROWS):
        rows = slice(r0, r0 + FFN_SUB_ROWS)
        x = x_ref[rows, :]
        f = (_rms(x, g_ref[2:3, :]) * (1.0 + sc) + sh).astype(BF16)
        gate = _dot(f, wg_ref[...])
        up = _dot(f, wu_ref[...])
        act = (_silu(gate) * up).astype(BF16)
        y = _dot(act, wd_ref[...])
        o_ref[rows, :] = x + gt * _rms(y, g_ref[3:4, :])


def _ffn(x, mod, row_of_step, g, wg, wu, wd, tm, cast_jobs=()):
    t = x.shape[0]
    tok = pl.BlockSpec((tm, D_MODEL), lambda i: (i, 0))
    cast_in, cast_out, cast_shapes = _cast_specs(cast_jobs, t // tm, lambda i: i)
    outs = pl.pallas_call(
        functools.partial(_ffn_kernel, n_cast=len(cast_jobs)),
        grid=(t // tm,),
        in_specs=[
            tok,
            _mod_spec(row_of_step),
            _const_spec((4, D_MODEL)),
            _const_spec((D_MODEL, D_FF)),
            _const_spec((D_MODEL, D_FF)),
            _const_spec((D_FF, D_MODEL)),
        ] + cast_in,
        out_specs=[tok] + cast_out,
        out_shape=[jax.ShapeDtypeStruct((t, D_MODEL), F32)] + cast_shapes,
        compiler_params=_params(1),
        name="ffn",
    )(x, mod, g, wg, wu, wd, *[src for src, _ in cast_jobs])
    return outs[0], outs[1:]


def _dft_cos_sin(n):
    j = np.arange(n)
    ang = 2.0 * np.pi * ((j[:, None] * j[None, :]) % n) / n
    return np.cos(ang), np.sin(ang)


PREMIX_CHUNK = 512


def _premix_channel_dft(x_ref, n_rows, sc, sh, g_ref, cs_ref, xc_scr, xs_scr):
    cs = cs_ref[...]
    for r0 in range(0, n_rows, PREMIX_CHUNK):
        rows = slice(r0, r0 + PREMIX_CHUNK)
        h = (_rms(x_ref[rows, :], g_ref[0:1, :]) * (1.0 + sc) + sh).astype(BF16)
        for j in range(N_FOURIER_GROUPS):
            lanes = slice(FOURIER_GROUP * j, FOURIER_GROUP * (j + 1))
            r = _dot(h[:, lanes], cs)
            xc_scr[rows, lanes] = r[:, :FOURIER_GROUP].astype(BF16)
            xs_scr[rows, lanes] = r[:, FOURIER_GROUP:].astype(BF16)


def _fourier_kernel(c_ref, s_ref, cs_ref, x_ref, mod_ref, g_ref, w_ref, *rest,
                    n_sub, seq, tr, n_cast, mod_layer):
    n_mod = 0 if mod_layer is None else 1
    n_in = n_cast + 3 * n_mod
    o_ref = rest[n_in]
    n_out_end = n_in + 1 + n_cast + n_mod
    xc_scr, xs_scr, f_scr, w_scr = rest[n_out_end:n_out_end + 4]
    _run_cast_jobs(rest[:n_cast], rest[n_in + 1:n_in + 1 + n_cast])
    t = pl.program_id(1)
    if n_mod:
        step = pl.program_id(0) * pl.num_programs(1) + t
        _mod_accumulate(*rest[n_cast:n_in], rest[n_out_end - 1], rest[n_out_end + 4], mod_layer,
                        step, pl.num_programs(0) * pl.num_programs(1))
    m = mod_ref[...]
    sh = m[:, 0:D_MODEL]
    sc = m[:, D_MODEL:2 * D_MODEL]
    gt = m[:, 2 * D_MODEL:3 * D_MODEL]

    @pl.when((pl.program_id(0) == 0) & (t == 0))
    def _():
        w_scr[...] = w_ref[...].astype(BF16)

    @pl.when(t == 0)
    def _():
        _premix_channel_dft(x_ref, n_sub * seq, sc, sh, g_ref, cs_ref, xc_scr, xs_scr)

    for b in range(n_sub):
        src = slice(seq * b, seq * (b + 1))
        f = _dot(c_ref[...], xc_scr[src, :]) - _dot(s_ref[...], xs_scr[src, :])
        f_scr[tr * b:tr * (b + 1), :] = f.astype(BF16)
    mix = _dot(f_scr[...], w_scr[...])
    if n_sub == 1:
        x = x_ref[pl.ds(pl.multiple_of(t * tr, tr), tr), :]
    else:
        x = x_ref[...]
    o_ref[...] = x + gt * _rms(mix, g_ref[1:2, :])


def _fourier_mixer(cmat, smat, cs, x, mod, row_of_group, g, w, n_seqs, seq, n_sub, tr,
                   cast_jobs=(), mod_job=None):
    assert n_sub == 1 or tr == seq
    n_t = seq // tr
    n_groups = n_seqs // n_sub
    rows = n_sub * seq
    step_of = lambda b, t: b * n_t + t
    cast_in, cast_out, cast_shapes = _cast_specs(cast_jobs, n_groups * n_t, step_of)
    mod_in, mod_out, mod_shapes, mod_scratch, mod_args, mod_layer = [], [], [], [], [], None
    if mod_job is not None:
        mod_layer = mod_job[3]
        mod_in, out_spec, out_shape, scratch = _mod_specs(mod_layer, n_groups * n_t, step_of)
        mod_out, mod_shapes, mod_scratch, mod_args = [out_spec], [out_shape], [scratch], mod_job[:3]
    outs = pl.pallas_call(
        functools.partial(_fourier_kernel, n_sub=n_sub, seq=seq, tr=tr, n_cast=len(cast_jobs),
                          mod_layer=mod_layer),
        grid=(n_groups, n_t),
        in_specs=[
            pl.BlockSpec((tr, seq), lambda b, t: (t, 0)),
            pl.BlockSpec((tr, seq), lambda b, t: (t, 0)),
            _const_spec((FOURIER_GROUP, 2 * FOURIER_GROUP)),
            pl.BlockSpec((rows, D_MODEL), lambda b, t: (b, 0)),
            _mod_spec(lambda b, t: row_of_group(b)),
            _const_spec((4, D_MODEL)),
            _const_spec((D_MODEL, D_MODEL)),
        ] + cast_in + mod_in,
        out_specs=([pl.BlockSpec((n_sub * tr, D_MODEL), lambda b, t: (b * n_t + t, 0))]
                   + cast_out + mod_out),
        out_shape=[jax.ShapeDtypeStruct((n_seqs * seq, D_MODEL), F32)] + cast_shapes + mod_shapes,
        scratch_shapes=[pltpu.VMEM((rows, D_MODEL), BF16), pltpu.VMEM((rows, D_MODEL), BF16),
                        pltpu.VMEM((n_sub * tr, D_MODEL), BF16),
                        pltpu.VMEM((D_MODEL, D_MODEL), BF16)] + mod_scratch,
        compiler_params=_params(2),
        name="fourier_mixer",
    )(cmat, smat, cs, x, mod, g, w, *[src for src, _ in cast_jobs], *mod_args)
    n_cast = len(cast_jobs)
    return outs[0], outs[1:1 + n_cast], (outs[1 + n_cast] if mod_job is not None else None)


HERM_TILE = 512
HERM_ROWS = HERM_TILE + BF16_SUBLANES


def _half_spectrum_blocks(mat):
    n_half = mat.shape[0] // 2 // HERM_TILE
    return np.stack([mat[HERM_TILE * t:HERM_TILE * t + HERM_ROWS] for t in range(n_half)])


def _reversal_matrix():
    rev = np.zeros((HERM_TILE, HERM_ROWS), np.float32)
    rev[np.arange(HERM_TILE), HERM_TILE - np.arange(HERM_TILE)] = 1.0
    return rev


def _fourier_long_kernel(ch_ref, sh_ref, rev_ref, cs_ref, x_ref, mod_ref, g_ref, w_ref, *rest,
                         seq, n_cast):
    o_ref = rest[n_cast]
    xc_scr, xs_scr, pq_scr, f_scr, w_scr = rest[2 * n_cast + 1:]
    _run_cast_jobs(rest[:n_cast], rest[n_cast + 1:2 * n_cast + 1])
    b, t, u = pl.program_id(0), pl.program_id(1), pl.program_id(2)
    m = mod_ref[...]
    sh = m[:, 0:D_MODEL]
    sc = m[:, D_MODEL:2 * D_MODEL]
    gt = m[:, 2 * D_MODEL:3 * D_MODEL]

    @pl.when((b == 0) & (t == 0) & (u == 0))
    def _():
        w_scr[...] = w_ref[...].astype(BF16)

    @pl.when((t == 0) & (u == 0))
    def _():
        _premix_channel_dft(x_ref, seq, sc, sh, g_ref, cs_ref, xc_scr, xs_scr)

    @pl.when(u == 0)
    def _():
        p = _dot(ch_ref[...], xc_scr[...])
        q = _dot(sh_ref[...], xs_scr[...])
        f_scr[...] = (p - q)[0:HERM_TILE].astype(BF16)
        pq_scr[...] = (p + q).astype(BF16)

    @pl.when(u == 1)
    def _():
        f_scr[...] = _dot(rev_ref[...], pq_scr[...]).astype(BF16)

    n_tiles = seq // HERM_TILE
    tile = t + u * (n_tiles - 1 - 2 * t)
    mix = _dot(f_scr[...], w_scr[...])
    x = x_ref[pl.ds(pl.multiple_of(tile * HERM_TILE, HERM_TILE), HERM_TILE), :]
    o_ref[...] = x + gt * _rms(mix, g_ref[1:2, :])


def _fourier_mixer_long(cos_h, sin_h, rev, cs, x, mod, row_of_seq, g, w, n_seqs, seq, cast_jobs=()):
    n_tiles = seq // HERM_TILE
    n_half = n_tiles // 2
    tile_of = lambda t, u: t + u * (n_tiles - 1 - 2 * t)
    half = pl.BlockSpec((None, HERM_ROWS, seq), lambda b, t, u: (t, 0, 0))
    cast_in, cast_out, cast_shapes = _cast_specs(
        cast_jobs, n_seqs * n_tiles, lambda b, t, u: (b * n_half + t) * 2 + u)
    outs = pl.pallas_call(
        functools.partial(_fourier_long_kernel, seq=seq, n_cast=len(cast_jobs)),
        grid=(n_seqs, n_half, 2),
        in_specs=[
            half, half,
            _const_spec((HERM_TILE, HERM_ROWS)),
            _const_spec((FOURIER_GROUP, 2 * FOURIER_GROUP)),
            pl.BlockSpec((seq, D_MODEL), lambda b, t, u: (b, 0)),
            _mod_spec(lambda b, t, u: row_of_seq(b)),
            _const_spec((4, D_MODEL)),
            _const_spec((D_MODEL, D_MODEL)),
        ] + cast_in,
        out_specs=[pl.BlockSpec((HERM_TILE, D_MODEL),
                                lambda b, t, u: (b * n_tiles + tile_of(t, u), 0))] + cast_out,
        out_shape=[jax.ShapeDtypeStruct((n_seqs * seq, D_MODEL), F32)] + cast_shapes,
        scratch_shapes=[pltpu.VMEM((seq, D_MODEL), BF16), pltpu.VMEM((seq, D_MODEL), BF16),
                        pltpu.VMEM((HERM_ROWS, D_MODEL), BF16),
                        pltpu.VMEM((HERM_TILE, D_MODEL), BF16),
                        pltpu.VMEM((D_MODEL, D_MODEL), BF16)],
        compiler_params=_params(3),
        name="fourier_mixer_long",
    )(cos_h, sin_h, rev, cs, x, mod, g, w, *[src for src, _ in cast_jobs])
    return outs[0], outs[1:]


def _split_pair(q, lo):
    zero = jnp.zeros_like(q)
    return jnp.concatenate([jnp.where(lo, q, zero), jnp.where(lo, zero, q)], axis=0)


def _prompt_attn_kernel(x_ref, mod_ref, g_ref, wq_ref, wk_ref, wv_ref, wout_ref,
                        o_ref, kt_ref, vt_ref, wkt_ref, wvt_ref, h_scr, q_scr, kt_scr, vt_scr,
                        att_scr, *, n_seq):
    @pl.when(pl.program_id(0) == 0)
    def _():
        wkt_ref[...] = wk_ref[...].T
        wvt_ref[...] = wv_ref[...].T

    x = x_ref[...]
    m = mod_ref[...]
    sh = m[:, 0:D_MODEL]
    sc = m[:, D_MODEL:2 * D_MODEL]
    gt = m[:, 2 * D_MODEL:3 * D_MODEL]
    h_scr[...] = (_rms(x, g_ref[0:1, :]) * (1.0 + sc) + sh).astype(BF16)
    q_scr[...] = (_dot(h_scr[...], wq_ref[...]) * Q_SCALE).astype(BF16)

    lo = lax.broadcasted_iota(jnp.int32, (SEQ, PAIR_W), 1) < HEAD_DIM
    ones = jnp.ones((PAIR_W, SEQ), BF16)
    for b in range(n_seq):
        rows = slice(SEQ * b, SEQ * (b + 1))
        kt = _dot_nt(wkt_ref[...], h_scr[rows, :])
        vt = _dot_nt(wvt_ref[...], h_scr[rows, :])
        kt_ref[b] = kt.reshape(N_HEADS, HEAD_DIM, SEQ)
        vt_ref[b] = vt.reshape(N_HEADS, HEAD_DIM, SEQ)
        kt_scr[b] = kt.astype(BF16)
        vt_scr[b] = vt.astype(BF16)
        for j in range(N_PAIRS):
            lanes = slice(PAIR_W * j, PAIR_W * (j + 1))
            qs = _split_pair(q_scr[rows, lanes], lo)
            s = _dot(qs, kt_scr[b, lanes, :])
            p = jnp.exp2((s - jnp.max(s, axis=-1, keepdims=True)).astype(BF16))
            o2 = _dot_nt(p, jnp.concatenate([vt_scr[b, lanes, :], ones], axis=0))
            o = o2[:, 0:PAIR_W] / o2[:, PAIR_W:2 * PAIR_W]
            att_scr[rows, lanes] = jnp.where(lo, o[:SEQ], o[SEQ:]).astype(BF16)

    mix = _dot(att_scr[...], wout_ref[...])
    o_ref[...] = x + gt * _rms(mix, g_ref[1:2, :])


def _prompt_attn(x, mod, row, g, wqkv, wout, n_seq=4):
    t = x.shape[0]
    tm = n_seq * SEQ
    tok = pl.BlockSpec((tm, D_MODEL), lambda i: (i, 0))
    out = jax.ShapeDtypeStruct((t, D_MODEL), F32)
    cache = pl.BlockSpec((n_seq, None, N_HEADS, HEAD_DIM, SEQ), lambda i: (i, 0, 0, 0, 0))
    cache_out = jax.ShapeDtypeStruct((t // SEQ, 1, N_HEADS, HEAD_DIM, SEQ), F32)

    def qkv_part(n):
        return pl.BlockSpec((D_MODEL, D_MODEL), lambda i: (0, n), pipeline_mode=pl.Buffered(1))

    return pl.pallas_call(
        functools.partial(_prompt_attn_kernel, n_seq=n_seq),
        grid=(t // tm,),
        in_specs=[tok, _mod_spec(lambda i: row), _const_spec((4, D_MODEL)),
                  qkv_part(0), qkv_part(1), qkv_part(2), _const_spec((D_MODEL, D_MODEL))],
        out_specs=[tok, cache, cache],
        out_shape=[out, cache_out, cache_out],
        scratch_shapes=[pltpu.VMEM((D_MODEL, D_MODEL), BF16), pltpu.VMEM((D_MODEL, D_MODEL), BF16),
                        pltpu.VMEM((tm, D_MODEL), BF16), pltpu.VMEM((tm, D_MODEL), BF16),
                        pltpu.VMEM((n_seq, D_MODEL, SEQ), BF16),
                        pltpu.VMEM((n_seq, D_MODEL, SEQ), BF16),
                        pltpu.VMEM((tm, D_MODEL), BF16)],
        compiler_params=_params(1),
        name="prompt_attn",
    )(x, mod, g, wqkv, wqkv, wqkv, wout)


def _premix_qkv_kernel(x_ref, mod_ref, g_ref, wqkv_ref, ckt_ref, cvt_ref, l_ref,
                       q_ref, k_ref, v_ref, ck_ref, cv_ref, bias_ref, *, heads_per_step):
    _build_bias_tiles(l_ref, pl.program_id(0) * heads_per_step, heads_per_step, bias_ref)

    ck_ref[...] = ckt_ref[...].astype(BF16)
    _store_values_with_ones(cv_ref, cvt_ref[...].T.astype(BF16))

    x = x_ref[...]
    m = mod_ref[...]
    sh = m[:, 0:D_MODEL]
    sc = m[:, D_MODEL:2 * D_MODEL]
    h = (_rms(x, g_ref[0:1, :]) * (1.0 + sc) + sh).astype(BF16)
    qkv = _dot(h, wqkv_ref[...])
    q_ref[...] = (qkv[:, 0:D_MODEL] * Q_SCALE).astype(BF16)
    k_ref[...] = qkv[:, D_MODEL:2 * D_MODEL].astype(BF16)
    _store_values_with_ones(v_ref, qkv[:, 2 * D_MODEL:3 * D_MODEL].astype(BF16))


def _store_values_with_ones(v_ref, v):
    ones = jnp.ones((v.shape[0], PAIR_W), BF16)
    for j in range(v.shape[1] // PAIR_W):
        v_ref[:, 2 * PAIR_W * j:2 * PAIR_W * j + PAIR_W] = v[:, PAIR_W * j:PAIR_W * (j + 1)]
        v_ref[:, 2 * PAIR_W * j + PAIR_W:2 * PAIR_W * (j + 1)] = ones


def _premix_qkv(x, mod, row_of_step, g, wqkv, cache_kt, cache_vt, layer_j, rpb, tm=512):
    t = x.shape[0]
    heads_per_step = N_HEADS // (t // tm)
    rpb_rows = jnp.pad(rpb, ((0, 0), (0, 0), (48, 49)), mode="edge")
    steps_per_seq = DEC_SEQ // tm
    chunk = D_MODEL // steps_per_seq
    tok = pl.BlockSpec((tm, D_MODEL), lambda i: (i, 0))
    tok2 = pl.BlockSpec((tm, 2 * D_MODEL), lambda i: (i, 0))
    cache = pl.BlockSpec((None, None, chunk, PAST_LEN),
                         lambda i: (i // steps_per_seq, layer_j, i % steps_per_seq, 0))
    out = jax.ShapeDtypeStruct((t, D_MODEL), BF16)
    out2 = jax.ShapeDtypeStruct((t, 2 * D_MODEL), BF16)
    return pl.pallas_call(
        functools.partial(_premix_qkv_kernel, heads_per_step=heads_per_step),
        grid=(t // tm,),
        in_specs=[tok, _mod_spec(row_of_step), _const_spec((4, D_MODEL)),
                  _const_spec((D_MODEL, 3 * D_MODEL)), cache, cache,
                  _const_spec((N_HEADS, N_DR, PAIR_W))],
        out_specs=[tok, tok, tok2,
                   pl.BlockSpec((chunk, PAST_LEN), lambda i: (i, 0)),
                   pl.BlockSpec((PAST_LEN, 2 * chunk),
                                lambda i: (i // steps_per_seq, i % steps_per_seq)),
                   pl.BlockSpec((heads_per_step * N_DR_PAIRS, GRID_W, PAIR_W),
                                lambda i: (i, 0, 0))],
        out_shape=[out, out, out2,
                   jax.ShapeDtypeStruct((DEC_BATCH * D_MODEL, PAST_LEN), BF16),
                   jax.ShapeDtypeStruct((DEC_BATCH * PAST_LEN, 2 * D_MODEL), BF16),
                   jax.ShapeDtypeStruct((N_HEADS * N_DR_PAIRS, GRID_W, PAIR_W), F32)],
        compiler_params=_params(1),
        name="premix_qkv",
    )(x, mod, g, wqkv, cache_kt, cache_vt, rpb_rows)


def _build_bias_tiles(l_ref, first_head, n_heads, bias_ref):
    lane = lax.broadcasted_iota(jnp.int32, (GRID_W, PAIR_W), 1)
    qcol = lax.broadcasted_iota(jnp.int32, (GRID_W, PAIR_W), 0)
    kcol = lane & (GRID_W - 1)
    start = jnp.clip(qcol - WIN_COLS // 2, 0, GRID_W - WIN_COLS)
    in_window = (kcol >= start) & (kcol < start + WIN_COLS)
    lo = lane < GRID_W

    for hh in range(n_heads):
        def toeplitz(d, shift):
            row = jnp.broadcast_to(l_ref[first_head + hh, d:d + 1, :], (GRID_W, PAIR_W))
            return pltpu.roll(row, shift, 1, stride=1, stride_axis=0)

        for d in range(N_DR_PAIRS):
            both = jnp.where(lo, toeplitz(d, GRID_W + 1), toeplitz(d + 1, 1))
            bias_ref[hh * N_DR_PAIRS + d] = jnp.where(in_window, both * LOG2E, NEG_INF)


def _kv_window_start(blk, rows_per_step):
    return jnp.clip(blk * rows_per_step - WIN_ROWS // 2, 0,
                    GRID_ROWS - (rows_per_step + WIN_ROWS))


def _na_attn_kernel(q_ref, k_ref, v_ref, ckt_ref, cv_ref, bias_ref, x_ref, mod_ref, g_ref,
                    wout_ref, o_ref, s_scr, m_scr, p_scr, att_scr, *, rows_per_step):
    blk = pl.program_id(1)
    lo = lax.broadcasted_iota(jnp.int32, (GRID_W, PAIR_W), 1) < HEAD_DIM

    win_start = _kv_window_start(blk, rows_per_step)

    def row_geometry(i):
        r = blk * rows_per_step + i
        rs = jnp.clip(r - WIN_ROWS // 2, 0, GRID_ROWS - WIN_ROWS)
        d0 = rs - r + (WIN_ROWS - 1)
        q0 = pl.multiple_of(i * GRID_W, GRID_W)
        k0 = pl.multiple_of((rs - win_start) * GRID_W, GRID_W)
        return d0, q0, k0

    def scores(i, slot):
        d0, q0, k0 = row_geometry(i)
        for j in range(N_PAIRS):
            lanes = slice(PAIR_W * j, PAIR_W * (j + 1))
            qs = _split_pair(q_ref[pl.ds(q0, GRID_W), lanes], lo)
            bias = jnp.concatenate(
                [jnp.concatenate(
                    [bias_ref[(2 * j) * N_DR_PAIRS + d0 + 2 * jj],
                     bias_ref[(2 * j + 1) * N_DR_PAIRS + d0 + 2 * jj]], axis=0)
                 for jj in range(WIN_ROWS // 2)], axis=1)
            s_loc = _dot_nt(qs, k_ref[pl.ds(k0, N_LOCAL), lanes]) + bias
            s_ctx = _dot(qs, ckt_ref[lanes, :])
            mx = jnp.maximum(jnp.max(s_loc, axis=-1, keepdims=True),
                             jnp.max(s_ctx, axis=-1, keepdims=True))
            s_scr[slot, j, :, 0:N_LOCAL] = s_loc
            s_scr[slot, j, :, N_LOCAL:N_KEYS] = s_ctx
            m_scr[slot, j] = jnp.broadcast_to(mx, (2 * GRID_W, PAIR_W))

    def probs(slot):
        for j in range(N_PAIRS):
            mx = m_scr[slot, j][:, 0:1]
            p_scr[slot, j] = jnp.exp2((s_scr[slot, j] - mx).astype(BF16))

    def values(i, slot):
        _, q0, k0 = row_geometry(i)
        for j in range(N_PAIRS):
            lanes2 = slice(2 * PAIR_W * j, 2 * PAIR_W * (j + 1))
            p = p_scr[slot, j]
            o2 = (_dot(p[:, 0:N_LOCAL], v_ref[pl.ds(k0, N_LOCAL), lanes2])
                  + _dot(p[:, N_LOCAL:N_KEYS], cv_ref[:, lanes2]))
            o = o2[:, 0:PAIR_W] / o2[:, PAIR_W:2 * PAIR_W]
            att_scr[pl.ds(q0, GRID_W), PAIR_W * j:PAIR_W * (j + 1)] = (
                jnp.where(lo, o[:GRID_W], o[GRID_W:]).astype(BF16))

    scores(0, 0)
    probs(0)
    scores(1, 1)

    def two_rows(t, carry):
        i = 2 * t
        values(i - 2, 0)
        probs(1)
        scores(i, 0)
        values(i - 1, 1)
        probs(0)
        scores(i + 1, 1)
        return carry

    lax.fori_loop(1, rows_per_step // 2, two_rows, 0)
    values(rows_per_step - 2, 0)
    probs(1)
    values(rows_per_step - 1, 1)

    gt = mod_ref[...][:, 2 * D_MODEL:3 * D_MODEL]
    mix = _dot(att_scr[...], wout_ref[...])
    o_ref[...] = x_ref[...] + gt * _rms(mix, g_ref[1:2, :])


def _na_attn(q, k, v, ck, cv, bias, x, mod, row_of_batch, g, wout, rows_per_step=8):
    tm = rows_per_step * GRID_W
    n_t = DEC_SEQ // tm
    tok = pl.BlockSpec((tm, D_MODEL), lambda b, t: (b * n_t + t, 0))

    def per_batch(rows, width):
        return pl.BlockSpec((rows, width), lambda b, t: (b, 0), pipeline_mode=pl.Buffered(1))

    def kv_window(width):
        def start(b, t):
            row = _kv_window_start(t, rows_per_step)
            return (pl.multiple_of(b * DEC_SEQ + row * GRID_W, GRID_W), 0)
        return pl.BlockSpec(
            (pl.Element((rows_per_step + WIN_ROWS) * GRID_W), pl.Element(width)), start)

    return pl.pallas_call(
        functools.partial(_na_attn_kernel, rows_per_step=rows_per_step),
        grid=(DEC_BATCH, n_t),
        in_specs=[
            tok,
            kv_window(D_MODEL), kv_window(2 * D_MODEL),
            per_batch(D_MODEL, PAST_LEN), per_batch(PAST_LEN, 2 * D_MODEL),
            _const_spec((N_HEADS * N_DR_PAIRS, GRID_W, PAIR_W)),
            tok,
            _mod_spec(lambda b, t: row_of_batch(b)),
            _const_spec((4, D_MODEL)),
            _const_spec((D_MODEL, D_MODEL)),
        ],
        out_specs=tok,
        out_shape=jax.ShapeDtypeStruct((DEC_BATCH * DEC_SEQ, D_MODEL), F32),
        scratch_shapes=[
            pltpu.VMEM((2, N_PAIRS, 2 * GRID_W, N_KEYS), F32),
            pltpu.VMEM((2, N_PAIRS, 2 * GRID_W, PAIR_W), F32),
            pltpu.VMEM((2, N_PAIRS, 2 * GRID_W, N_KEYS), BF16),
            pltpu.VMEM((tm, D_MODEL), BF16),
        ],
        compiler_params=_params(2),
        name="na_attn",
    )(q, k, v, ck, cv, bias, x, mod, g, wout)


def kernel(x_prompt, x_sample, c, cache_k, cache_v, c_ctx, ada_w, ada_b, norm_g, fourier_w_out,
           na_w_qkv, na_rpb, na_w_out, ffn_w_gate, ffn_w_up, ffn_w_down):
    n_p = BATCH * SEQ
    n_s = DEC_BATCH * DEC_SEQ
    xp = x_prompt.reshape(n_p, D_MODEL)
    xs = x_sample.reshape(n_s, D_MODEL)

    cond = jnp.concatenate(
        [c_ctx[None, :], c, jnp.zeros((COND_ROWS - 1 - DEC_BATCH, D_MODEL), F32)], axis=0)
    mods = {0: _modulation(cond, ada_w, ada_b, 0)}

    cos_g, sin_g = _dft_cos_sin(FOURIER_GROUP)
    cs_chan = jnp.asarray(np.concatenate([cos_g, sin_g], axis=1), F32).astype(BF16)
    cos_p, sin_p = (jnp.asarray(a, F32).astype(BF16) for a in _dft_cos_sin(SEQ))
    cos_s, sin_s = (jnp.asarray(_half_spectrum_blocks(a), F32).astype(BF16)
                    for a in _dft_cos_sin(DEC_SEQ))
    rev = jnp.asarray(_reversal_matrix(), F32).astype(BF16)

    tm = 512
    ffn_tm = 1024
    ffn_weights = (ffn_w_gate, ffn_w_up, ffn_w_down)
    cache_kt = jnp.transpose(cache_k, (0, 1, 3, 4, 2)).reshape(DEC_BATCH, -1, D_MODEL, PAST_LEN)
    cache_vt = jnp.transpose(cache_v, (0, 1, 3, 4, 2)).reshape(DEC_BATCH, -1, D_MODEL, PAST_LEN)
    new_kt = new_vt = None
    attn_weights = None
    for layer in range(DEPTH):
        g = norm_g[layer]
        if layer not in mods:
            mods[layer] = _modulation(cond, ada_w, ada_b, layer)
        mod = mods[layer]
        base = 0
        prompt_row = lambda *_, base=base: base
        sample_row_of_batch = lambda b, base=base: base + 1 + b
        sample_row_of_tile = lambda i, base=base: base + 1 + i // (DEC_SEQ // tm)
        sample_row_of_ffn_tile = lambda i, base=base: base + 1 + i // (DEC_SEQ // ffn_tm)

        if layer % 2 == 0:
            w_out = fourier_w_out[layer // 2]
            jobs = [(w, layer) for w in ffn_weights] if layer == 0 else []
            mod_job = (cond, ada_w, ada_b, layer + 1) if layer + 1 < DEPTH else None
            xp, cast_p, next_mod = _fourier_mixer(
                cos_p, sin_p, cs_chan, xp, mod, prompt_row, g, w_out, BATCH, SEQ, n_sub=4, tr=SEQ,
                cast_jobs=jobs[:2], mod_job=mod_job)
            if mod_job is not None:
                mods[layer + 1] = next_mod
            xs, cast_s = _fourier_mixer_long(cos_s, sin_s, rev, cs_chan, xs, mod,
                                             sample_row_of_batch, g, w_out, DEC_BATCH, DEC_SEQ,
                                             cast_jobs=jobs[2:])
            if layer == 0:
                wg, wu, wd = list(cast_p) + list(cast_s)
        else:
            j = layer // 2
            if attn_weights is None:
                attn_weights = (na_w_qkv[j].astype(BF16), na_w_out[j].astype(BF16))
            w_qkv, w_out = attn_weights
            attn_weights = None
            xp, new_kt, new_vt = _prompt_attn(xp, mod, base, g, w_qkv, w_out)
            q, k, v, ckt, cvt, bias = _premix_qkv(xs, mod, sample_row_of_tile, g, w_qkv,
                                                  cache_kt, cache_vt, j, na_rpb[j], tm)
            xs = _na_attn(q, k, v, ckt, cvt, bias, xs, mod, sample_row_of_batch, g, w_out)

        nxt = layer + 1
        ffn_jobs = [(w, nxt) for w in ffn_weights] if nxt < DEPTH else []
        attn_jobs = ([(na_w_qkv, nxt // 2), (na_w_out, nxt // 2)]
                     if nxt < DEPTH and nxt % 2 == 1 else [])
        xp, cast_ffn = _ffn(xp, mod, prompt_row, g, wg, wu, wd, tm if ffn_jobs else ffn_tm,
                            cast_jobs=ffn_jobs)
        xs, cast_attn = _ffn(xs, mod, sample_row_of_tile if attn_jobs else sample_row_of_ffn_tile,
                             g, wg, wu, wd, tm if attn_jobs else ffn_tm, cast_jobs=attn_jobs)
        if ffn_jobs:
            wg, wu, wd = cast_ffn
        if attn_jobs:
            attn_weights = tuple(cast_attn)

    new_k = jnp.transpose(new_kt, (0, 1, 4, 2, 3))
    new_v = jnp.transpose(new_vt, (0, 1, 4, 2, 3))
    return (xp.reshape(BATCH, SEQ, D_MODEL), xs.reshape(DEC_BATCH, DEC_SEQ, D_MODEL), new_k, new_v)
```

```python
import functools

import numpy as np
import jax
import jax.numpy as jnp
from jax import lax
from jax.experimental import pallas as pl
from jax.experimental.pallas import tpu as pltpu

D_MODEL = 1024
BATCH = 32
SEQ = 256
DEPTH = 2
DEC_BATCH = 2
DEC_SEQ = 2048
PAST_LEN = 512
GRID_W = 64
GRID_ROWS = DEC_SEQ // GRID_W
N_HEADS = 16
HEAD_DIM = D_MODEL // N_HEADS
N_FOURIER_GROUPS = 4
FOURIER_GROUP = D_MODEL // N_FOURIER_GROUPS
WIN_ROWS = 8
WIN_COLS = 16
D_FF = 2816
EPS = 1e-6
NEG_INF = -1e30

N_PAIRS = N_HEADS // 2
PAIR_W = 2 * HEAD_DIM
N_DR = 2 * WIN_ROWS - 1
N_DR_PAIRS = N_DR - 1
N_LOCAL = WIN_ROWS * GRID_W
N_KEYS = N_LOCAL + PAST_LEN
COND_ROWS = 8
LOG2E = 1.4426950408889634
Q_SCALE = HEAD_DIM ** -0.5 * LOG2E

VMEM_LIMIT = 56 * 1024 * 1024
BF16_SUBLANES = 16
TOKEN_TILE = 512
FFN_TILE = 1024
MOD_K_STEPS = 4

F32 = jnp.float32
BF16 = jnp.bfloat16


def _dot(a, b):
    return jnp.dot(a, b, preferred_element_type=F32)


def _dot_nt(a, b):
    return lax.dot_general(a, b, (((1,), (1,)), ((), ())), preferred_element_type=F32)


def _rms(x, g):
    ms = jnp.mean(x * x, axis=-1, keepdims=True)
    return x * lax.rsqrt(ms + EPS) * g


def _silu(x):
    return x / (1.0 + jnp.exp(-x))


def _const_spec(shape):
    return pl.BlockSpec(shape, lambda *_: (0,) * len(shape), pipeline_mode=pl.Buffered(1))


def _mod_spec(row_of_step):
    return pl.BlockSpec((None, 1, 6 * D_MODEL), lambda *idx: (row_of_step(*idx), 0, 0))


def _params(n_axes):
    return pltpu.CompilerParams(
        dimension_semantics=("arbitrary",) * n_axes, vmem_limit_bytes=VMEM_LIMIT)


def _mod_accumulate(cond_ref, w_ref, b_ref, o_ref, acc_ref, layer, step, n_steps):
    @pl.when(step == 0)
    def _():
        acc_ref[...] = jnp.broadcast_to(b_ref[layer:layer + 1, :], acc_ref.shape)

    a = _silu(cond_ref[...])
    a_hi = a.astype(BF16)
    a_lo = (a - a_hi.astype(F32)).astype(BF16)
    r = _dot(jnp.concatenate([a_hi, a_lo], axis=0), w_ref[...].astype(BF16))
    acc_ref[...] += r[:COND_ROWS] + r[COND_ROWS:]

    @pl.when(step == n_steps - 1)
    def _():
        for row in range(COND_ROWS):
            o_ref[row] = acc_ref[row:row + 1, :]


def _mod_specs(layer, n_steps, step_of):
    tk = D_MODEL // n_steps
    n_out = 6 * D_MODEL
    in_specs = [
        pl.BlockSpec((COND_ROWS, tk), lambda *idx: (0, step_of(*idx))),
        pl.BlockSpec((None, tk, n_out), lambda *idx: (layer, step_of(*idx), 0)),
        pl.BlockSpec((DEPTH, n_out), lambda *idx: (0, 0)),
    ]
    out_spec = pl.BlockSpec((COND_ROWS, 1, n_out), lambda *idx: (0, 0, 0))
    out_shape = jax.ShapeDtypeStruct((COND_ROWS, 1, n_out), F32)
    scratch = pltpu.VMEM((COND_ROWS, n_out), F32)
    return in_specs, out_spec, out_shape, scratch


def _mod_kernel(cond_ref, w_ref, b_ref, o_ref, acc_ref, *, layer):
    _mod_accumulate(cond_ref, w_ref, b_ref, o_ref, acc_ref, layer,
                    pl.program_id(0), pl.num_programs(0))


def _modulation(cond, ada_w, ada_b, layer):
    n_steps = MOD_K_STEPS
    in_specs, out_spec, out_shape, scratch = _mod_specs(layer, n_steps, lambda k: k)
    return pl.pallas_call(
        functools.partial(_mod_kernel, layer=layer),
        grid=(n_steps,),
        in_specs=in_specs,
        out_specs=out_spec,
        out_shape=out_shape,
        scratch_shapes=[scratch],
        compiler_params=_params(1),
        name="modulation",
    )(cond, ada_w, ada_b)


FFN_SUB_ROWS = 256


def _cast_specs(jobs, n_steps, step_of):
    in_specs, out_specs, out_shapes = [], [], []
    for src, layer in jobs:
        rows, cols = src.shape[1:]
        chunk = rows // n_steps
        assert chunk * n_steps == rows and chunk % BF16_SUBLANES == 0
        in_specs.append(pl.BlockSpec(
            (None, chunk, cols), lambda *idx, layer=layer: (layer, step_of(*idx), 0)))
        out_specs.append(pl.BlockSpec((chunk, cols), lambda *idx: (step_of(*idx), 0)))
        out_shapes.append(jax.ShapeDtypeStruct((rows, cols), BF16))
    return in_specs, out_specs, out_shapes


def _run_cast_jobs(src_refs, dst_refs):
    for src, dst in zip(src_refs, dst_refs):
        dst[...] = src[...].astype(BF16)


def _ffn_kernel(x_ref, mod_ref, g_ref, wg_ref, wu_ref, wd_ref, *rest, n_cast, mod_layer):
    n_mod = 0 if mod_layer is None else 1
    n_in = n_cast + 3 * n_mod
    o_ref = rest[n_in]
    _run_cast_jobs(rest[:n_cast], rest[n_in + 1:n_in + 1 + n_cast])
    if n_mod:
        _mod_accumulate(*rest[n_cast:n_in], rest[n_in + 1 + n_cast], rest[n_in + 2 + n_cast],
                        mod_layer, pl.program_id(0), pl.num_programs(0))
    m = mod_ref[...]
    sh = m[:, 3 * D_MODEL:4 * D_MODEL]
    sc = m[:, 4 * D_MODEL:5 * D_MODEL]
    gt = m[:, 5 * D_MODEL:6 * D_MODEL]
    for r0 in range(0, x_ref.shape[0], FFN_SUB_ROWS):
        rows = slice(r0, r0 + FFN_SUB_ROWS)
        x = x_ref[rows, :]
        f = (_rms(x, g_ref[2:3, :]) * (1.0 + sc) + sh).astype(BF16)
        gate = _dot(f, wg_ref[...])
        up = _dot(f, wu_ref[...])
        act = (_silu(gate) * up).astype(BF16)
        y = _dot(act, wd_ref[...])
        o_ref[rows, :] = x + gt * _rms(y, g_ref[3:4, :])


def _ffn(x, mod, row_of_step, g, wg, wu, wd, tm, cast_jobs=(), mod_job=None):
    t = x.shape[0]
    n_steps = t // tm
    tok = pl.BlockSpec((tm, D_MODEL), lambda i: (i, 0))
    cast_in, cast_out, cast_shapes = _cast_specs(cast_jobs, n_steps, lambda i: i)
    mod_in, mod_out, mod_shapes, mod_scratch, mod_args, mod_layer = [], [], [], [], (), None
    if mod_job is not None:
        mod_layer = mod_job[3]
        mod_in, out_spec, out_shape, scratch = _mod_specs(mod_layer, n_steps, lambda i: i)
        mod_out, mod_shapes, mod_scratch, mod_args = [out_spec], [out_shape], [scratch], mod_job[:3]
    outs = pl.pallas_call(
        functools.partial(_ffn_kernel, n_cast=len(cast_jobs), mod_layer=mod_layer),
        grid=(n_steps,),
        in_specs=[
            tok,
            _mod_spec(row_of_step),
            _const_spec((4, D_MODEL)),
            _const_spec((D_MODEL, D_FF)),
            _const_spec((D_MODEL, D_FF)),
            _const_spec((D_FF, D_MODEL)),
        ] + cast_in + mod_in,
        out_specs=[tok] + cast_out + mod_out,
        out_shape=[jax.ShapeDtypeStruct((t, D_MODEL), F32)] + cast_shapes + mod_shapes,
        scratch_shapes=mod_scratch,
        compiler_params=_params(1),
        name="ffn",
    )(x, mod, g, wg, wu, wd, *[src for src, _ in cast_jobs], *mod_args)
    n_cast = len(cast_jobs)
    return outs[0], outs[1:1 + n_cast], (outs[1 + n_cast] if mod_job is not None else None)


def _dft_cos_sin(n):
    j = np.arange(n)
    ang = 2.0 * np.pi * ((j[:, None] * j[None, :]) % n) / n
    return np.cos(ang), np.sin(ang)


PREMIX_CHUNK = 512


def _premix_channel_dft(x_ref, n_rows, sc, sh, g_ref, cs_ref, xc_scr, xs_scr):
    cs = cs_ref[...]
    for r0 in range(0, n_rows, PREMIX_CHUNK):
        rows = slice(r0, r0 + PREMIX_CHUNK)
        h = (_rms(x_ref[rows, :], g_ref[0:1, :]) * (1.0 + sc) + sh).astype(BF16)
        for j in range(N_FOURIER_GROUPS):
            lanes = slice(FOURIER_GROUP * j, FOURIER_GROUP * (j + 1))
            r = _dot(h[:, lanes], cs)
            xc_scr[rows, lanes] = r[:, :FOURIER_GROUP].astype(BF16)
            xs_scr[rows, lanes] = r[:, FOURIER_GROUP:].astype(BF16)


def _fourier_kernel(c_ref, s_ref, cs_ref, x_ref, mod_ref, g_ref, w_ref, *rest,
                    n_sub, seq, n_cast):
    o_ref = rest[n_cast]
    xc_scr, xs_scr, f_scr, w_scr = rest[2 * n_cast + 1:]
    _run_cast_jobs(rest[:n_cast], rest[n_cast + 1:2 * n_cast + 1])
    m = mod_ref[...]
    sh = m[:, 0:D_MODEL]
    sc = m[:, D_MODEL:2 * D_MODEL]
    gt = m[:, 2 * D_MODEL:3 * D_MODEL]

    @pl.when(pl.program_id(0) == 0)
    def _():
        w_scr[...] = w_ref[...].astype(BF16)

    _premix_channel_dft(x_ref, n_sub * seq, sc, sh, g_ref, cs_ref, xc_scr, xs_scr)
    for b in range(n_sub):
        rows = slice(seq * b, seq * (b + 1))
        f = _dot(c_ref[...], xc_scr[rows, :]) - _dot(s_ref[...], xs_scr[rows, :])
        f_scr[rows, :] = f.astype(BF16)
    mix = _dot(f_scr[...], w_scr[...])
    o_ref[...] = x_ref[...] + gt * _rms(mix, g_ref[1:2, :])


def _fourier_mixer(cmat, smat, cs, x, mod, row_of_group, g, w, n_seqs, seq, n_sub, cast_jobs=()):
    n_groups = n_seqs // n_sub
    rows = n_sub * seq
    tok = pl.BlockSpec((rows, D_MODEL), lambda b: (b, 0))
    cast_in, cast_out, cast_shapes = _cast_specs(cast_jobs, n_groups, lambda b: b)
    outs = pl.pallas_call(
        functools.partial(_fourier_kernel, n_sub=n_sub, seq=seq, n_cast=len(cast_jobs)),
        grid=(n_groups,),
        in_specs=[
            _const_spec((seq, seq)),
            _const_spec((seq, seq)),
            _const_spec((FOURIER_GROUP, 2 * FOURIER_GROUP)),
            tok,
            _mod_spec(row_of_group),
            _const_spec((4, D_MODEL)),
            _const_spec((D_MODEL, D_MODEL)),
        ] + cast_in,
        out_specs=[tok] + cast_out,
        out_shape=[jax.ShapeDtypeStruct((n_seqs * seq, D_MODEL), F32)] + cast_shapes,
        scratch_shapes=[pltpu.VMEM((rows, D_MODEL), BF16), pltpu.VMEM((rows, D_MODEL), BF16),
                        pltpu.VMEM((rows, D_MODEL), BF16),
                        pltpu.VMEM((D_MODEL, D_MODEL), BF16)],
        compiler_params=_params(1),
        name="fourier_mixer",
    )(cmat, smat, cs, x, mod, g, w, *[src for src, _ in cast_jobs])
    return outs[0], outs[1:]


HERM_TILE = 512
HERM_ROWS = HERM_TILE + BF16_SUBLANES


def _half_spectrum_blocks(mat):
    n_half = mat.shape[0] // 2 // HERM_TILE
    return np.stack([mat[HERM_TILE * t:HERM_TILE * t + HERM_ROWS] for t in range(n_half)])


def _reversal_matrix():
    rev = np.zeros((HERM_TILE, HERM_ROWS), np.float32)
    rev[np.arange(HERM_TILE), HERM_TILE - np.arange(HERM_TILE)] = 1.0
    return rev


def _fourier_long_kernel(ch_ref, sh_ref, rev_ref, cs_ref, x_ref, mod_ref, g_ref, w_ref, *rest,
                         seq, n_cast):
    o_ref = rest[n_cast]
    xc_scr, xs_scr, pq_scr, f_scr, w_scr = rest[2 * n_cast + 1:]
    _run_cast_jobs(rest[:n_cast], rest[n_cast + 1:2 * n_cast + 1])
    b, t, u = pl.program_id(0), pl.program_id(1), pl.program_id(2)
    m = mod_ref[...]
    sh = m[:, 0:D_MODEL]
    sc = m[:, D_MODEL:2 * D_MODEL]
    gt = m[:, 2 * D_MODEL:3 * D_MODEL]

    @pl.when((b == 0) & (t == 0) & (u == 0))
    def _():
        w_scr[...] = w_ref[...].astype(BF16)

    @pl.when((t == 0) & (u == 0))
    def _():
        _premix_channel_dft(x_ref, seq, sc, sh, g_ref, cs_ref, xc_scr, xs_scr)

    @pl.when(u == 0)
    def _():
        p = _dot(ch_ref[...], xc_scr[...])
        q = _dot(sh_ref[...], xs_scr[...])
        f_scr[...] = (p - q)[0:HERM_TILE].astype(BF16)
        pq_scr[...] = (p + q).astype(BF16)

    @pl.when(u == 1)
    def _():
        f_scr[...] = _dot(rev_ref[...], pq_scr[...]).astype(BF16)

    n_tiles = seq // HERM_TILE
    tile = t + u * (n_tiles - 1 - 2 * t)
    mix = _dot(f_scr[...], w_scr[...])
    x = x_ref[pl.ds(pl.multiple_of(tile * HERM_TILE, HERM_TILE), HERM_TILE), :]
    o_ref[...] = x + gt * _rms(mix, g_ref[1:2, :])


def _fourier_mixer_long(cos_h, sin_h, rev, cs, x, mod, row_of_seq, g, w, n_seqs, seq, cast_jobs=()):
    n_tiles = seq // HERM_TILE
    n_half = n_tiles // 2
    tile_of = lambda t, u: t + u * (n_tiles - 1 - 2 * t)
    half = pl.BlockSpec((None, HERM_ROWS, seq), lambda b, t, u: (t, 0, 0))
    cast_in, cast_out, cast_shapes = _cast_specs(
        cast_jobs, n_seqs * n_tiles, lambda b, t, u: (b * n_half + t) * 2 + u)
    outs = pl.pallas_call(
        functools.partial(_fourier_long_kernel, seq=seq, n_cast=len(cast_jobs)),
        grid=(n_seqs, n_half, 2),
        in_specs=[
            half, half,
            _const_spec((HERM_TILE, HERM_ROWS)),
            _const_spec((FOURIER_GROUP, 2 * FOURIER_GROUP)),
            pl.BlockSpec((seq, D_MODEL), lambda b, t, u: (b, 0)),
            _mod_spec(lambda b, t, u: row_of_seq(b)),
            _const_spec((4, D_MODEL)),
            _const_spec((D_MODEL, D_MODEL)),
        ] + cast_in,
        out_specs=[pl.BlockSpec((HERM_TILE, D_MODEL),
                                lambda b, t, u: (b * n_tiles + tile_of(t, u), 0))] + cast_out,
        out_shape=[jax.ShapeDtypeStruct((n_seqs * seq, D_MODEL), F32)] + cast_shapes,
        scratch_shapes=[pltpu.VMEM((seq, D_MODEL), BF16), pltpu.VMEM((seq, D_MODEL), BF16),
                        pltpu.VMEM((HERM_ROWS, D_MODEL), BF16),
                        pltpu.VMEM((HERM_TILE, D_MODEL), BF16),
                        pltpu.VMEM((D_MODEL, D_MODEL), BF16)],
        compiler_params=_params(3),
        name="fourier_mixer_long",
    )(cos_h, sin_h, rev, cs, x, mod, g, w, *[src for src, _ in cast_jobs])
    return outs[0], outs[1:]


def _split_pair(q, lo):
    zero = jnp.zeros_like(q)
    return jnp.concatenate([jnp.where(lo, q, zero), jnp.where(lo, zero, q)], axis=0)


def _prompt_attn_kernel(x_ref, mod_ref, g_ref, wq_ref, wk_ref, wv_ref, wout_ref,
                        o_ref, kt_ref, vt_ref, wkt_ref, wvt_ref, h_scr, q_scr, kt_scr, vt_scr,
                        att_scr, *, n_seq):
    @pl.when(pl.program_id(0) == 0)
    def _():
        wkt_ref[...] = wk_ref[...].T
        wvt_ref[...] = wv_ref[...].T

    x = x_ref[...]
    m = mod_ref[...]
    sh = m[:, 0:D_MODEL]
    sc = m[:, D_MODEL:2 * D_MODEL]
    gt = m[:, 2 * D_MODEL:3 * D_MODEL]
    h_scr[...] = (_rms(x, g_ref[0:1, :]) * (1.0 + sc) + sh).astype(BF16)
    q_scr[...] = (_dot(h_scr[...], wq_ref[...]) * Q_SCALE).astype(BF16)

    lo = lax.broadcasted_iota(jnp.int32, (SEQ, PAIR_W), 1) < HEAD_DIM
    ones = jnp.ones((PAIR_W, SEQ), BF16)
    for b in range(n_seq):
        rows = slice(SEQ * b, SEQ * (b + 1))
        kt = _dot_nt(wkt_ref[...], h_scr[rows, :])
        vt = _dot_nt(wvt_ref[...], h_scr[rows, :])
        kt_ref[b] = kt.reshape(N_HEADS, HEAD_DIM, SEQ)
        vt_ref[b] = vt.reshape(N_HEADS, HEAD_DIM, SEQ)
        kt_scr[b] = kt.astype(BF16)
        vt_scr[b] = vt.astype(BF16)
        for j in range(N_PAIRS):
            lanes = slice(PAIR_W * j, PAIR_W * (j + 1))
            qs = _split_pair(q_scr[rows, lanes], lo)
            s = _dot(qs, kt_scr[b, lanes, :])
            p = jnp.exp2((s - jnp.max(s, axis=-1, keepdims=True)).astype(BF16))
            o2 = _dot_nt(p, jnp.concatenate([vt_scr[b, lanes, :], ones], axis=0))
            o = o2[:, 0:PAIR_W] / o2[:, PAIR_W:2 * PAIR_W]
            att_scr[rows, lanes] = jnp.where(lo, o[:SEQ], o[SEQ:]).astype(BF16)

    mix = _dot(att_scr[...], wout_ref[...])
    o_ref[...] = x + gt * _rms(mix, g_ref[1:2, :])


def _prompt_attn(x, mod, row, g, wqkv, wout, n_seq=4):
    t = x.shape[0]
    tm = n_seq * SEQ
    tok = pl.BlockSpec((tm, D_MODEL), lambda i: (i, 0))
    out = jax.ShapeDtypeStruct((t, D_MODEL), F32)
    cache = pl.BlockSpec((n_seq, None, N_HEADS, HEAD_DIM, SEQ), lambda i: (i, 0, 0, 0, 0))
    cache_out = jax.ShapeDtypeStruct((t // SEQ, 1, N_HEADS, HEAD_DIM, SEQ), F32)

    def qkv_part(n):
        return pl.BlockSpec((D_MODEL, D_MODEL), lambda i: (0, n), pipeline_mode=pl.Buffered(1))

    return pl.pallas_call(
        functools.partial(_prompt_attn_kernel, n_seq=n_seq),
        grid=(t // tm,),
        in_specs=[tok, _mod_spec(lambda i: row), _const_spec((4, D_MODEL)),
                  qkv_part(0), qkv_part(1), qkv_part(2), _const_spec((D_MODEL, D_MODEL))],
        out_specs=[tok, cache, cache],
        out_shape=[out, cache_out, cache_out],
        scratch_shapes=[pltpu.VMEM((D_MODEL, D_MODEL), BF16), pltpu.VMEM((D_MODEL, D_MODEL), BF16),
                        pltpu.VMEM((tm, D_MODEL), BF16), pltpu.VMEM((tm, D_MODEL), BF16),
                        pltpu.VMEM((n_seq, D_MODEL, SEQ), BF16),
                        pltpu.VMEM((n_seq, D_MODEL, SEQ), BF16),
                        pltpu.VMEM((tm, D_MODEL), BF16)],
        compiler_params=_params(1),
        name="prompt_attn",
    )(x, mod, g, wqkv, wqkv, wqkv, wout)


def _premix_qkv_kernel(x_ref, mod_ref, g_ref, wqkv_ref, ckt_ref, cvt_ref, l_ref,
                       q_ref, k_ref, v_ref, ck_ref, cv_ref, bias_ref, *, heads_per_step):
    _build_bias_tiles(l_ref, pl.program_id(0) * heads_per_step, heads_per_step, bias_ref)

    ck_ref[...] = ckt_ref[...].astype(BF16)
    _store_values_with_ones(cv_ref, cvt_ref[...].T.astype(BF16))

    x = x_ref[...]
    m = mod_ref[...]
    sh = m[:, 0:D_MODEL]
    sc = m[:, D_MODEL:2 * D_MODEL]
    h = (_rms(x, g_ref[0:1, :]) * (1.0 + sc) + sh).astype(BF16)
    qkv = _dot(h, wqkv_ref[...])
    q_ref[...] = (qkv[:, 0:D_MODEL] * Q_SCALE).astype(BF16)
    k_ref[...] = qkv[:, D_MODEL:2 * D_MODEL].astype(BF16)
    _store_values_with_ones(v_ref, qkv[:, 2 * D_MODEL:3 * D_MODEL].astype(BF16))


def _store_values_with_ones(v_ref, v):
    ones = jnp.ones((v.shape[0], PAIR_W), BF16)
    for j in range(v.shape[1] // PAIR_W):
        v_ref[:, 2 * PAIR_W * j:2 * PAIR_W * j + PAIR_W] = v[:, PAIR_W * j:PAIR_W * (j + 1)]
        v_ref[:, 2 * PAIR_W * j + PAIR_W:2 * PAIR_W * (j + 1)] = ones


def _premix_qkv(x, mod, row_of_step, g, wqkv, cache_kt, cache_vt, layer_j, rpb, tm=512):
    t = x.shape[0]
    heads_per_step = N_HEADS // (t // tm)
    left = (GRID_W - 1) - (WIN_COLS - 1)
    right = PAIR_W - (2 * WIN_COLS - 1) - left
    rpb_rows = jnp.pad(rpb, ((0, 0), (0, 0), (left, right)), mode="edge")
    steps_per_seq = DEC_SEQ // tm
    chunk = D_MODEL // steps_per_seq
    tok = pl.BlockSpec((tm, D_MODEL), lambda i: (i, 0))
    tok2 = pl.BlockSpec((tm, 2 * D_MODEL), lambda i: (i, 0))
    cache = pl.BlockSpec((None, None, chunk, PAST_LEN),
                         lambda i: (i // steps_per_seq, layer_j, i % steps_per_seq, 0))
    out = jax.ShapeDtypeStruct((t, D_MODEL), BF16)
    out2 = jax.ShapeDtypeStruct((t, 2 * D_MODEL), BF16)
    return pl.pallas_call(
        functools.partial(_premix_qkv_kernel, heads_per_step=heads_per_step),
        grid=(t // tm,),
        in_specs=[tok, _mod_spec(row_of_step), _const_spec((4, D_MODEL)),
                  _const_spec((D_MODEL, 3 * D_MODEL)), cache, cache,
                  _const_spec((N_HEADS, N_DR, PAIR_W))],
        out_specs=[tok, tok, tok2,
                   pl.BlockSpec((chunk, PAST_LEN), lambda i: (i, 0)),
                   pl.BlockSpec((PAST_LEN, 2 * chunk),
                                lambda i: (i // steps_per_seq, i % steps_per_seq)),
                   pl.BlockSpec((heads_per_step * N_DR_PAIRS, GRID_W, PAIR_W),
                                lambda i: (i, 0, 0))],
        out_shape=[out, out, out2,
                   jax.ShapeDtypeStruct((DEC_BATCH * D_MODEL, PAST_LEN), BF16),
                   jax.ShapeDtypeStruct((DEC_BATCH * PAST_LEN, 2 * D_MODEL), BF16),
                   jax.ShapeDtypeStruct((N_HEADS * N_DR_PAIRS, GRID_W, PAIR_W), F32)],
        compiler_params=_params(1),
        name="premix_qkv",
    )(x, mod, g, wqkv, cache_kt, cache_vt, rpb_rows)


def _build_bias_tiles(l_ref, first_head, n_heads, bias_ref):
    lane = lax.broadcasted_iota(jnp.int32, (GRID_W, PAIR_W), 1)
    qcol = lax.broadcasted_iota(jnp.int32, (GRID_W, PAIR_W), 0)
    kcol = lane & (GRID_W - 1)
    start = jnp.clip(qcol - WIN_COLS // 2, 0, GRID_W - WIN_COLS)
    in_window = (kcol >= start) & (kcol < start + WIN_COLS)
    lo = lane < GRID_W

    for hh in range(n_heads):
        def toeplitz(d, shift):
            row = jnp.broadcast_to(l_ref[first_head + hh, d:d + 1, :], (GRID_W, PAIR_W))
            return pltpu.roll(row, shift, 1, stride=1, stride_axis=0)

        for d in range(N_DR_PAIRS):
            both = jnp.where(lo, toeplitz(d, GRID_W + 1), toeplitz(d + 1, 1))
            bias_ref[hh * N_DR_PAIRS + d] = jnp.where(in_window, both * LOG2E, NEG_INF)


def _kv_window_start(blk, rows_per_step):
    return jnp.clip(blk * rows_per_step - WIN_ROWS // 2, 0,
                    GRID_ROWS - (rows_per_step + WIN_ROWS))


def _na_attn_kernel(q_ref, k_ref, v_ref, ckt_ref, cv_ref, bias_ref, x_ref, mod_ref, g_ref,
                    wout_ref, o_ref, s_scr, m_scr, p_scr, att_scr, *, rows_per_step):
    blk = pl.program_id(1)
    lo = lax.broadcasted_iota(jnp.int32, (GRID_W, PAIR_W), 1) < HEAD_DIM

    win_start = _kv_window_start(blk, rows_per_step)

    def row_geometry(i):
        r = blk * rows_per_step + i
        rs = jnp.clip(r - WIN_ROWS // 2, 0, GRID_ROWS - WIN_ROWS)
        d0 = rs - r + (WIN_ROWS - 1)
        q0 = pl.multiple_of(i * GRID_W, GRID_W)
        k0 = pl.multiple_of((rs - win_start) * GRID_W, GRID_W)
        return d0, q0, k0

    def scores(i, slot):
        d0, q0, k0 = row_geometry(i)
        for j in range(N_PAIRS):
            lanes = slice(PAIR_W * j, PAIR_W * (j + 1))
            qs = _split_pair(q_ref[pl.ds(q0, GRID_W), lanes], lo)
            bias = jnp.concatenate(
                [jnp.concatenate(
                    [bias_ref[(2 * j) * N_DR_PAIRS + d0 + 2 * jj],
                     bias_ref[(2 * j + 1) * N_DR_PAIRS + d0 + 2 * jj]], axis=0)
                 for jj in range(WIN_ROWS // 2)], axis=1)
            s_loc = _dot_nt(qs, k_ref[pl.ds(k0, N_LOCAL), lanes]) + bias
            s_ctx = _dot(qs, ckt_ref[lanes, :])
            mx = jnp.maximum(jnp.max(s_loc, axis=-1, keepdims=True),
                             jnp.max(s_ctx, axis=-1, keepdims=True))
            s_scr[slot, j, :, 0:N_LOCAL] = s_loc
            s_scr[slot, j, :, N_LOCAL:N_KEYS] = s_ctx
            m_scr[slot, j] = jnp.broadcast_to(mx, (2 * GRID_W, PAIR_W))

    def probs(slot):
        for j in range(N_PAIRS):
            mx = m_scr[slot, j][:, 0:1]
            p_scr[slot, j] = jnp.exp2((s_scr[slot, j] - mx).astype(BF16))

    def values(i, slot):
        _, q0, k0 = row_geometry(i)
        for j in range(N_PAIRS):
            lanes2 = slice(2 * PAIR_W * j, 2 * PAIR_W * (j + 1))
            p = p_scr[slot, j]
            o2 = (_dot(p[:, 0:N_LOCAL], v_ref[pl.ds(k0, N_LOCAL), lanes2])
                  + _dot(p[:, N_LOCAL:N_KEYS], cv_ref[:, lanes2]))
            o = o2[:, 0:PAIR_W] / o2[:, PAIR_W:2 * PAIR_W]
            att_scr[pl.ds(q0, GRID_W), PAIR_W * j:PAIR_W * (j + 1)] = (
                jnp.where(lo, o[:GRID_W], o[GRID_W:]).astype(BF16))

    scores(0, 0)
    probs(0)
    scores(1, 1)

    def two_rows(t, carry):
        i = 2 * t
        values(i - 2, 0)
        probs(1)
        scores(i, 0)
        values(i - 1, 1)
        probs(0)
        scores(i + 1, 1)
        return carry

    lax.fori_loop(1, rows_per_step // 2, two_rows, 0)
    values(rows_per_step - 2, 0)
    probs(1)
    values(rows_per_step - 1, 1)

    gt = mod_ref[...][:, 2 * D_MODEL:3 * D_MODEL]
    mix = _dot(att_scr[...], wout_ref[...])
    o_ref[...] = x_ref[...] + gt * _rms(mix, g_ref[1:2, :])


def _na_attn(q, k, v, ck, cv, bias, x, mod, row_of_batch, g, wout, rows_per_step=8):
    tm = rows_per_step * GRID_W
    n_t = DEC_SEQ // tm
    tok = pl.BlockSpec((tm, D_MODEL), lambda b, t: (b * n_t + t, 0))

    def per_batch(rows, width):
        return pl.BlockSpec((rows, width), lambda b, t: (b, 0), pipeline_mode=pl.Buffered(1))

    def kv_window(width):
        def start(b, t):
            row = _kv_window_start(t, rows_per_step)
            return (pl.multiple_of(b * DEC_SEQ + row * GRID_W, GRID_W), 0)
        return pl.BlockSpec(
            (pl.Element((rows_per_step + WIN_ROWS) * GRID_W), pl.Element(width)), start)

    return pl.pallas_call(
        functools.partial(_na_attn_kernel, rows_per_step=rows_per_step),
        grid=(DEC_BATCH, n_t),
        in_specs=[
            tok,
            kv_window(D_MODEL), kv_window(2 * D_MODEL),
            per_batch(D_MODEL, PAST_LEN), per_batch(PAST_LEN, 2 * D_MODEL),
            _const_spec((N_HEADS * N_DR_PAIRS, GRID_W, PAIR_W)),
            tok,
            _mod_spec(lambda b, t: row_of_batch(b)),
            _const_spec((4, D_MODEL)),
            _const_spec((D_MODEL, D_MODEL)),
        ],
        out_specs=tok,
        out_shape=jax.ShapeDtypeStruct((DEC_BATCH * DEC_SEQ, D_MODEL), F32),
        scratch_shapes=[
            pltpu.VMEM((2, N_PAIRS, 2 * GRID_W, N_KEYS), F32),
            pltpu.VMEM((2, N_PAIRS, 2 * GRID_W, PAIR_W), F32),
            pltpu.VMEM((2, N_PAIRS, 2 * GRID_W, N_KEYS), BF16),
            pltpu.VMEM((tm, D_MODEL), BF16),
        ],
        compiler_params=_params(2),
        name="na_attn",
    )(q, k, v, ck, cv, bias, x, mod, g, wout)


def kernel(x_prompt, x_sample, c, cache_k, cache_v, c_ctx, ada_w, ada_b, norm_g, fourier_w_out,
           na_w_qkv, na_rpb, na_w_out, ffn_w_gate, ffn_w_up, ffn_w_down):
    n_p = BATCH * SEQ
    n_s = DEC_BATCH * DEC_SEQ
    xp = x_prompt.reshape(n_p, D_MODEL)
    xs = x_sample.reshape(n_s, D_MODEL)

    cond = jnp.concatenate(
        [c_ctx[None, :], c, jnp.zeros((COND_ROWS - 1 - DEC_BATCH, D_MODEL), F32)], axis=0)
    mods = {0: _modulation(cond, ada_w, ada_b, 0)}

    cos_g, sin_g = _dft_cos_sin(FOURIER_GROUP)
    cs_chan = jnp.asarray(np.concatenate([cos_g, sin_g], axis=1), F32).astype(BF16)
    cos_p, sin_p = (jnp.asarray(a, F32).astype(BF16) for a in _dft_cos_sin(SEQ))
    cos_s, sin_s = (jnp.asarray(_half_spectrum_blocks(a), F32).astype(BF16)
                    for a in _dft_cos_sin(DEC_SEQ))
    rev = jnp.asarray(_reversal_matrix(), F32).astype(BF16)

    tm, ffn_tm = TOKEN_TILE, FFN_TILE
    ffn_weights = (ffn_w_gate, ffn_w_up, ffn_w_down)
    cache_kt = jnp.transpose(cache_k, (0, 1, 3, 4, 2)).reshape(DEC_BATCH, -1, D_MODEL, PAST_LEN)
    cache_vt = jnp.transpose(cache_v, (0, 1, 3, 4, 2)).reshape(DEC_BATCH, -1, D_MODEL, PAST_LEN)
    new_kt = new_vt = None
    attn_weights = None
    for layer in range(DEPTH):
        g = norm_g[layer]
        if layer not in mods:
            mods[layer] = _modulation(cond, ada_w, ada_b, layer)
        mod = mods[layer]
        prompt_row = lambda *_: 0
        sample_row_of_batch = lambda b: 1 + b
        sample_row_of_tile = lambda i: 1 + i // (DEC_SEQ // tm)
        sample_row_of_ffn_tile = lambda i: 1 + i // (DEC_SEQ // ffn_tm)

        if layer % 2 == 0:
            w_out = fourier_w_out[layer // 2]
            jobs = [(w, layer) for w in ffn_weights] if layer == 0 else []
            xp, cast_p = _fourier_mixer(cos_p, sin_p, cs_chan, xp, mod, prompt_row, g, w_out,
                                        BATCH, SEQ, n_sub=4, cast_jobs=jobs[:2])
            xs, cast_s = _fourier_mixer_long(cos_s, sin_s, rev, cs_chan, xs, mod,
                                             sample_row_of_batch, g, w_out, DEC_BATCH, DEC_SEQ,
                                             cast_jobs=jobs[2:])
            if layer == 0:
                wg, wu, wd = list(cast_p) + list(cast_s)
        else:
            j = layer // 2
            if attn_weights is None:
                attn_weights = (na_w_qkv[j].astype(BF16), na_w_out[j].astype(BF16))
            w_qkv, w_out = attn_weights
            attn_weights = None
            xp, new_kt, new_vt = _prompt_attn(xp, mod, 0, g, w_qkv, w_out)
            q, k, v, ckt, cvt, bias = _premix_qkv(xs, mod, sample_row_of_tile, g, w_qkv,
                                                  cache_kt, cache_vt, j, na_rpb[j], tm)
            xs = _na_attn(q, k, v, ckt, cvt, bias, xs, mod, sample_row_of_batch, g, w_out)

        nxt = layer + 1
        ffn_jobs = [(w, nxt) for w in ffn_weights] if nxt < DEPTH else []
        attn_jobs = ([(na_w_qkv, nxt // 2), (na_w_out, nxt // 2)]
                     if nxt < DEPTH and nxt % 2 == 1 else [])
        mod_job = (cond, ada_w, ada_b, nxt) if nxt < DEPTH else None
        busy = bool(attn_jobs) or mod_job is not None
        xp, cast_ffn, _ = _ffn(xp, mod, prompt_row, g, wg, wu, wd, tm if ffn_jobs else ffn_tm,
                               cast_jobs=ffn_jobs)
        xs, cast_attn, next_mod = _ffn(
            xs, mod, sample_row_of_tile if busy else sample_row_of_ffn_tile, g, wg, wu, wd,
            tm if busy else ffn_tm, cast_jobs=attn_jobs, mod_job=mod_job)
        if ffn_jobs:
            wg, wu, wd = cast_ffn
        if attn_jobs:
            attn_weights = tuple(cast_attn)
        if mod_job is not None:
            mods[nxt] = next_mod

    new_k = jnp.transpose(new_kt, (0, 1, 4, 2, 3))
    new_v = jnp.transpose(new_vt, (0, 1, 4, 2, 3))
    return (xp.reshape(BATCH, SEQ, D_MODEL), xs.reshape(DEC_BATCH, DEC_SEQ, D_MODEL), new_k, new_v)
```

```python
import functools

import numpy as np
import jax
import jax.numpy as jnp
from jax import lax
from jax.experimental import pallas as pl
from jax.experimental.pallas import tpu as pltpu

D_MODEL = 1024
BATCH = 32
SEQ = 256
DEPTH = 2
DEC_BATCH = 2
DEC_SEQ = 2048
PAST_LEN = 512
GRID_W = 64
GRID_ROWS = DEC_SEQ // GRID_W
N_HEADS = 16
HEAD_DIM = D_MODEL // N_HEADS
N_FOURIER_GROUPS = 4
FOURIER_GROUP = D_MODEL // N_FOURIER_GROUPS
WIN_ROWS = 8
WIN_COLS = 16
D_FF = 2816
EPS = 1e-6
NEG_INF = -1e30

N_PAIRS = N_HEADS // 2
PAIR_W = 2 * HEAD_DIM
N_DR = 2 * WIN_ROWS - 1
N_DR_PAIRS = N_DR - 1
N_LOCAL = WIN_ROWS * GRID_W
N_KEYS = N_LOCAL + PAST_LEN
COND_ROWS = 8
LOG2E = 1.4426950408889634
Q_SCALE = HEAD_DIM ** -0.5 * LOG2E

VMEM_LIMIT = 56 * 1024 * 1024
BF16_SUBLANES = 16
TOKEN_TILE = 512
FFN_TILE = 1024
MOD_K_STEPS = 4

F32 = jnp.float32
BF16 = jnp.bfloat16


def _dot(a, b):
    return jnp.dot(a, b, preferred_element_type=F32)


def _dot_nt(a, b):
    return lax.dot_general(a, b, (((1,), (1,)), ((), ())), preferred_element_type=F32)


def _rms(x, g):
    ms = jnp.mean(x * x, axis=-1, keepdims=True)
    return x * lax.rsqrt(ms + EPS) * g


def _silu(x):
    return x / (1.0 + jnp.exp(-x))


def _const_spec(shape):
    return pl.BlockSpec(shape, lambda *_: (0,) * len(shape), pipeline_mode=pl.Buffered(1))


def _mod_spec(row_of_step):
    return pl.BlockSpec((None, 1, 6 * D_MODEL), lambda *idx: (row_of_step(*idx), 0, 0))


def _params(n_axes):
    return pltpu.CompilerParams(
        dimension_semantics=("arbitrary",) * n_axes, vmem_limit_bytes=VMEM_LIMIT)


def _mod_accumulate(cond_ref, w_ref, b_ref, o_ref, acc_ref, layer, step, n_steps):
    @pl.when(step == 0)
    def _():
        acc_ref[...] = jnp.broadcast_to(b_ref[layer:layer + 1, :], acc_ref.shape)

    a = _silu(cond_ref[...])
    a_hi = a.astype(BF16)
    a_lo = (a - a_hi.astype(F32)).astype(BF16)
    r = _dot(jnp.concatenate([a_hi, a_lo], axis=0), w_ref[...].astype(BF16))
    acc_ref[...] += r[:COND_ROWS] + r[COND_ROWS:]

    @pl.when(step == n_steps - 1)
    def _():
        for row in range(COND_ROWS):
            o_ref[row] = acc_ref[row:row + 1, :]


def _mod_specs(layer, n_steps, step_of):
    tk = D_MODEL // n_steps
    n_out = 6 * D_MODEL
    in_specs = [
        pl.BlockSpec((COND_ROWS, tk), lambda *idx: (0, step_of(*idx))),
        pl.BlockSpec((None, tk, n_out), lambda *idx: (layer, step_of(*idx), 0)),
        pl.BlockSpec((DEPTH, n_out), lambda *idx: (0, 0)),
    ]
    out_spec = pl.BlockSpec((COND_ROWS, 1, n_out), lambda *idx: (0, 0, 0))
    out_shape = jax.ShapeDtypeStruct((COND_ROWS, 1, n_out), F32)
    scratch = pltpu.VMEM((COND_ROWS, n_out), F32)
    return in_specs, out_spec, out_shape, scratch


def _mod_kernel(cond_ref, w_ref, b_ref, o_ref, acc_ref, *, layer):
    _mod_accumulate(cond_ref, w_ref, b_ref, o_ref, acc_ref, layer,
                    pl.program_id(0), pl.num_programs(0))


def _modulation(cond, ada_w, ada_b, layer):
    n_steps = MOD_K_STEPS
    in_specs, out_spec, out_shape, scratch = _mod_specs(layer, n_steps, lambda k: k)
    return pl.pallas_call(
        functools.partial(_mod_kernel, layer=layer),
        grid=(n_steps,),
        in_specs=in_specs,
        out_specs=out_spec,
        out_shape=out_shape,
        scratch_shapes=[scratch],
        compiler_params=_params(1),
        name="modulation",
    )(cond, ada_w, ada_b)


FFN_SUB_ROWS = 256


def _cast_specs(jobs, n_steps, step_of):
    in_specs, out_specs, out_shapes = [], [], []
    for src, layer in jobs:
        rows, cols = src.shape[1:]
        chunk = rows // n_steps
        assert chunk * n_steps == rows and chunk % BF16_SUBLANES == 0
        in_specs.append(pl.BlockSpec(
            (None, chunk, cols), lambda *idx, layer=layer: (layer, step_of(*idx), 0)))
        out_specs.append(pl.BlockSpec((chunk, cols), lambda *idx: (step_of(*idx), 0)))
        out_shapes.append(jax.ShapeDtypeStruct((rows, cols), BF16))
    return in_specs, out_specs, out_shapes


def _run_cast_jobs(src_refs, dst_refs):
    for src, dst in zip(src_refs, dst_refs):
        dst[...] = src[...].astype(BF16)


def _ffn_kernel(x_ref, mod_ref, g_ref, wg_ref, wu_ref, wd_ref, *rest, n_cast, mod_layer):
    n_mod = 0 if mod_layer is None else 1
    n_in = n_cast + 3 * n_mod
    o_ref = rest[n_in]
    _run_cast_jobs(rest[:n_cast], rest[n_in + 1:n_in + 1 + n_cast])
    if n_mod:
        _mod_accumulate(*rest[n_cast:n_in], rest[n_in + 1 + n_cast], rest[n_in + 2 + n_cast],
                        mod_layer, pl.program_id(0), pl.num_programs(0))
    m = mod_ref[...]
    sh = m[:, 3 * D_MODEL:4 * D_MODEL]
    sc = m[:, 4 * D_MODEL:5 * D_MODEL]
    gt = m[:, 5 * D_MODEL:6 * D_MODEL]
    for r0 in range(0, x_ref.shape[0], FFN_SUB_ROWS):
        rows = slice(r0, r0 + FFN_SUB_ROWS)
        x = x_ref[rows, :]
        f = (_rms(x, g_ref[2:3, :]) * (1.0 + sc) + sh).astype(BF16)
        gate = _dot(f, wg_ref[...])
        up = _dot(f, wu_ref[...])
        act = (_silu(gate) * up).astype(BF16)
        y = _dot(act, wd_ref[...])
        o_ref[rows, :] = x + gt * _rms(y, g_ref[3:4, :])


def _ffn(x, mod, row_of_step, g, wg, wu, wd, tm, cast_jobs=(), mod_job=None):
    t = x.shape[0]
    n_steps = t // tm
    tok = pl.BlockSpec((tm, D_MODEL), lambda i: (i, 0))
    cast_in, cast_out, cast_shapes = _cast_specs(cast_jobs, n_steps, lambda i: i)
    mod_in, mod_out, mod_shapes, mod_scratch, mod_args, mod_layer = [], [], [], [], (), None
    if mod_job is not None:
        mod_layer = mod_job[3]
        mod_in, out_spec, out_shape, scratch = _mod_specs(mod_layer, n_steps, lambda i: i)
        mod_out, mod_shapes, mod_scratch, mod_args = [out_spec], [out_shape], [scratch], mod_job[:3]
    outs = pl.pallas_call(
        functools.partial(_ffn_kernel, n_cast=len(cast_jobs), mod_layer=mod_layer),
        grid=(n_steps,),
        in_specs=[
            tok,
            _mod_spec(row_of_step),
            _const_spec((4, D_MODEL)),
            _const_spec((D_MODEL, D_FF)),
            _const_spec((D_MODEL, D_FF)),
            _const_spec((D_FF, D_MODEL)),
        ] + cast_in + mod_in,
        out_specs=[tok] + cast_out + mod_out,
        out_shape=[jax.ShapeDtypeStruct((t, D_MODEL), F32)] + cast_shapes + mod_shapes,
        scratch_shapes=mod_scratch,
        compiler_params=_params(1),
        name="ffn",
    )(x, mod, g, wg, wu, wd, *[src for src, _ in cast_jobs], *mod_args)
    n_cast = len(cast_jobs)
    return outs[0], outs[1:1 + n_cast], (outs[1 + n_cast] if mod_job is not None else None)


def _dft_cos_sin(n):
    j = np.arange(n)
    ang = 2.0 * np.pi * ((j[:, None] * j[None, :]) % n) / n
    return np.cos(ang), np.sin(ang)


PREMIX_CHUNK = 512


def _premix_channel_dft(x_ref, n_rows, sc, sh, g_ref, cs_ref, xc_scr, xs_scr):
    cs = cs_ref[...]
    for r0 in range(0, n_rows, PREMIX_CHUNK):
        rows = slice(r0, r0 + PREMIX_CHUNK)
        h = (_rms(x_ref[rows, :], g_ref[0:1, :]) * (1.0 + sc) + sh).astype(BF16)
        for j in range(N_FOURIER_GROUPS):
            lanes = slice(FOURIER_GROUP * j, FOURIER_GROUP * (j + 1))
            r = _dot(h[:, lanes], cs)
            xc_scr[rows, lanes] = r[:, :FOURIER_GROUP].astype(BF16)
            xs_scr[rows, lanes] = r[:, FOURIER_GROUP:].astype(BF16)


def _fourier_kernel(c_ref, s_ref, cs_ref, x_ref, mod_ref, g_ref, w_ref, *rest,
                    n_sub, seq, n_cast):
    o_ref = rest[n_cast]
    xc_scr, xs_scr, f_scr, w_scr = rest[2 * n_cast + 1:]
    _run_cast_jobs(rest[:n_cast], rest[n_cast + 1:2 * n_cast + 1])
    m = mod_ref[...]
    sh = m[:, 0:D_MODEL]
    sc = m[:, D_MODEL:2 * D_MODEL]
    gt = m[:, 2 * D_MODEL:3 * D_MODEL]

    @pl.when(pl.program_id(0) == 0)
    def _():
        w_scr[...] = w_ref[...].astype(BF16)

    _premix_channel_dft(x_ref, n_sub * seq, sc, sh, g_ref, cs_ref, xc_scr, xs_scr)
    for b in range(n_sub):
        rows = slice(seq * b, seq * (b + 1))
        f = _dot(c_ref[...], xc_scr[rows, :]) - _dot(s_ref[...], xs_scr[rows, :])
        f_scr[rows, :] = f.astype(BF16)
    mix = _dot(f_scr[...], w_scr[...])
    o_ref[...] = x_ref[...] + gt * _rms(mix, g_ref[1:2, :])


def _fourier_mixer(cmat, smat, cs, x, mod, row_of_group, g, w, n_seqs, seq, n_sub, cast_jobs=()):
    n_groups = n_seqs // n_sub
    rows = n_sub * seq
    tok = pl.BlockSpec((rows, D_MODEL), lambda b: (b, 0))
    cast_in, cast_out, cast_shapes = _cast_specs(cast_jobs, n_groups, lambda b: b)
    outs = pl.pallas_call(
        functools.partial(_fourier_kernel, n_sub=n_sub, seq=seq, n_cast=len(cast_jobs)),
        grid=(n_groups,),
        in_specs=[
            _const_spec((seq, seq)),
            _const_spec((seq, seq)),
            _const_spec((FOURIER_GROUP, 2 * FOURIER_GROUP)),
            tok,
            _mod_spec(row_of_group),
            _const_spec((4, D_MODEL)),
            _const_spec((D_MODEL, D_MODEL)),
        ] + cast_in,
        out_specs=[tok] + cast_out,
        out_shape=[jax.ShapeDtypeStruct((n_seqs * seq, D_MODEL), F32)] + cast_shapes,
        scratch_shapes=[pltpu.VMEM((rows, D_MODEL), BF16), pltpu.VMEM((rows, D_MODEL), BF16),
                        pltpu.VMEM((rows, D_MODEL), BF16),
                        pltpu.VMEM((D_MODEL, D_MODEL), BF16)],
        compiler_params=_params(1),
        name="fourier_mixer",
    )(cmat, smat, cs, x, mod, g, w, *[src for src, _ in cast_jobs])
    return outs[0], outs[1:]


HERM_TILE = 512
HERM_ROWS = HERM_TILE + BF16_SUBLANES


def _half_spectrum_blocks(mat):
    n_half = mat.shape[0] // 2 // HERM_TILE
    return np.stack([mat[HERM_TILE * t:HERM_TILE * t + HERM_ROWS] for t in range(n_half)])


def _reversal_matrix():
    rev = np.zeros((HERM_TILE, HERM_ROWS), np.float32)
    rev[np.arange(HERM_TILE), HERM_TILE - np.arange(HERM_TILE)] = 1.0
    return rev


def _fourier_long_kernel(ch_ref, sh_ref, rev_ref, cs_ref, x_ref, mod_ref, g_ref, w_ref, *rest,
                         seq, n_cast):
    o_ref = rest[n_cast]
    xc_scr, xs_scr, pq_scr, f_scr, w_scr = rest[2 * n_cast + 1:]
    _run_cast_jobs(rest[:n_cast], rest[n_cast + 1:2 * n_cast + 1])
    b, t, u = pl.program_id(0), pl.program_id(1), pl.program_id(2)
    m = mod_ref[...]
    sh = m[:, 0:D_MODEL]
    sc = m[:, D_MODEL:2 * D_MODEL]
    gt = m[:, 2 * D_MODEL:3 * D_MODEL]

    @pl.when((b == 0) & (t == 0) & (u == 0))
    def _():
        w_scr[...] = w_ref[...].astype(BF16)

    @pl.when((t == 0) & (u == 0))
    def _():
        _premix_channel_dft(x_ref, seq, sc, sh, g_ref, cs_ref, xc_scr, xs_scr)

    @pl.when(u == 0)
    def _():
        p = _dot(ch_ref[...], xc_scr[...])
        q = _dot(sh_ref[...], xs_scr[...])
        f_scr[...] = (p - q)[0:HERM_TILE].astype(BF16)
        pq_scr[...] = (p + q).astype(BF16)

    @pl.when(u == 1)
    def _():
        f_scr[...] = _dot(rev_ref[...], pq_scr[...]).astype(BF16)

    n_tiles = seq // HERM_TILE
    tile = t + u * (n_tiles - 1 - 2 * t)
    mix = _dot(f_scr[...], w_scr[...])
    x = x_ref[pl.ds(pl.multiple_of(tile * HERM_TILE, HERM_TILE), HERM_TILE), :]
    o_ref[...] = x + gt * _rms(mix, g_ref[1:2, :])


def _fourier_mixer_long(cos_h, sin_h, rev, cs, x, mod, row_of_seq, g, w, n_seqs, seq, cast_jobs=()):
    n_tiles = seq // HERM_TILE
    n_half = n_tiles // 2
    tile_of = lambda t, u: t + u * (n_tiles - 1 - 2 * t)
    half = pl.BlockSpec((None, HERM_ROWS, seq), lambda b, t, u: (t, 0, 0))
    cast_in, cast_out, cast_shapes = _cast_specs(
        cast_jobs, n_seqs * n_tiles, lambda b, t, u: (b * n_half + t) * 2 + u)
    outs = pl.pallas_call(
        functools.partial(_fourier_long_kernel, seq=seq, n_cast=len(cast_jobs)),
        grid=(n_seqs, n_half, 2),
        in_specs=[
            half, half,
            _const_spec((HERM_TILE, HERM_ROWS)),
            _const_spec((FOURIER_GROUP, 2 * FOURIER_GROUP)),
            pl.BlockSpec((seq, D_MODEL), lambda b, t, u: (b, 0)),
            _mod_spec(lambda b, t, u: row_of_seq(b)),
            _const_spec((4, D_MODEL)),
            _const_spec((D_MODEL, D_MODEL)),
        ] + cast_in,
        out_specs=[pl.BlockSpec((HERM_TILE, D_MODEL),
                                lambda b, t, u: (b * n_tiles + tile_of(t, u), 0))] + cast_out,
        out_shape=[jax.ShapeDtypeStruct((n_seqs * seq, D_MODEL), F32)] + cast_shapes,
        scratch_shapes=[pltpu.VMEM((seq, D_MODEL), BF16), pltpu.VMEM((seq, D_MODEL), BF16),
                        pltpu.VMEM((HERM_ROWS, D_MODEL), BF16),
                        pltpu.VMEM((HERM_TILE, D_MODEL), BF16),
                        pltpu.VMEM((D_MODEL, D_MODEL), BF16)],
        compiler_params=_params(3),
        name="fourier_mixer_long",
    )(cos_h, sin_h, rev, cs, x, mod, g, w, *[src for src, _ in cast_jobs])
    return outs[0], outs[1:]


def _split_pair(q, lo):
    zero = jnp.zeros_like(q)
    return jnp.concatenate([jnp.where(lo, q, zero), jnp.where(lo, zero, q)], axis=0)


def _prompt_attn_kernel(x_ref, mod_ref, g_ref, wq_ref, wk_ref, wv_ref, wout_ref,
                        o_ref, kt_ref, vt_ref, wkt_ref, wvt_ref, h_scr, q_scr, kt_scr, vt_scr,
                        att_scr, *, n_seq):
    @pl.when(pl.program_id(0) == 0)
    def _():
        wkt_ref[...] = wk_ref[...].T
        wvt_ref[...] = wv_ref[...].T

    x = x_ref[...]
    m = mod_ref[...]
    sh = m[:, 0:D_MODEL]
    sc = m[:, D_MODEL:2 * D_MODEL]
    gt = m[:, 2 * D_MODEL:3 * D_MODEL]
    h_scr[...] = (_rms(x, g_ref[0:1, :]) * (1.0 + sc) + sh).astype(BF16)
    q_scr[...] = (_dot(h_scr[...], wq_ref[...]) * Q_SCALE).astype(BF16)

    lo = lax.broadcasted_iota(jnp.int32, (SEQ, PAIR_W), 1) < HEAD_DIM
    ones = jnp.ones((PAIR_W, SEQ), BF16)
    for b in range(n_seq):
        rows = slice(SEQ * b, SEQ * (b + 1))
        kt = _dot_nt(wkt_ref[...], h_scr[rows, :])
        vt = _dot_nt(wvt_ref[...], h_scr[rows, :])
        kt_ref[b] = kt.reshape(N_HEADS, HEAD_DIM, SEQ)
        vt_ref[b] = vt.reshape(N_HEADS, HEAD_DIM, SEQ)
        kt_scr[b] = kt.astype(BF16)
        vt_scr[b] = vt.astype(BF16)
        for j in range(N_PAIRS):
            lanes = slice(PAIR_W * j, PAIR_W * (j + 1))
            qs = _split_pair(q_scr[rows, lanes], lo)
            s = _dot(qs, kt_scr[b, lanes, :])
            p = jnp.exp2((s - jnp.max(s, axis=-1, keepdims=True)).astype(BF16))
            o2 = _dot_nt(p, jnp.concatenate([vt_scr[b, lanes, :], ones], axis=0))
            o = o2[:, 0:PAIR_W] / o2[:, PAIR_W:2 * PAIR_W]
            att_scr[rows, lanes] = jnp.where(lo, o[:SEQ], o[SEQ:]).astype(BF16)

    mix = _dot(att_scr[...], wout_ref[...])
    o_ref[...] = x + gt * _rms(mix, g_ref[1:2, :])


def _prompt_attn(x, mod, row, g, wqkv, wout, n_seq=4):
    t = x.shape[0]
    tm = n_seq * SEQ
    tok = pl.BlockSpec((tm, D_MODEL), lambda i: (i, 0))
    out = jax.ShapeDtypeStruct((t, D_MODEL), F32)
    cache = pl.BlockSpec((n_seq, None, N_HEADS, HEAD_DIM, SEQ), lambda i: (i, 0, 0, 0, 0))
    cache_out = jax.ShapeDtypeStruct((t // SEQ, 1, N_HEADS, HEAD_DIM, SEQ), F32)

    def qkv_part(n):
        return pl.BlockSpec((D_MODEL, D_MODEL), lambda i: (0, n), pipeline_mode=pl.Buffered(1))

    return pl.pallas_call(
        functools.partial(_prompt_attn_kernel, n_seq=n_seq),
        grid=(t // tm,),
        in_specs=[tok, _mod_spec(lambda i: row), _const_spec((4, D_MODEL)),
                  qkv_part(0), qkv_part(1), qkv_part(2), _const_spec((D_MODEL, D_MODEL))],
        out_specs=[tok, cache, cache],
        out_shape=[out, cache_out, cache_out],
        scratch_shapes=[pltpu.VMEM((D_MODEL, D_MODEL), BF16), pltpu.VMEM((D_MODEL, D_MODEL), BF16),
                        pltpu.VMEM((tm, D_MODEL), BF16), pltpu.VMEM((tm, D_MODEL), BF16),
                        pltpu.VMEM((n_seq, D_MODEL, SEQ), BF16),
                        pltpu.VMEM((n_seq, D_MODEL, SEQ), BF16),
                        pltpu.VMEM((tm, D_MODEL), BF16)],
        compiler_params=_params(1),
        name="prompt_attn",
    )(x, mod, g, wqkv, wqkv, wqkv, wout)


def _premix_qkv_kernel(x_ref, mod_ref, g_ref, wqkv_ref, ckt_ref, cvt_ref, l_ref,
                       q_ref, k_ref, v_ref, ck_ref, cv_ref, bias_ref, *, heads_per_step):
    _build_bias_tiles(l_ref, pl.program_id(0) * heads_per_step, heads_per_step, bias_ref)

    ck_ref[...] = ckt_ref[...].astype(BF16)
    _store_values_with_ones(cv_ref, cvt_ref[...].T.astype(BF16))

    x = x_ref[...]
    m = mod_ref[...]
    sh = m[:, 0:D_MODEL]
    sc = m[:, D_MODEL:2 * D_MODEL]
    h = (_rms(x, g_ref[0:1, :]) * (1.0 + sc) + sh).astype(BF16)
    qkv = _dot(h, wqkv_ref[...])
    q_ref[...] = (qkv[:, 0:D_MODEL] * Q_SCALE).astype(BF16)
    k_ref[...] = qkv[:, D_MODEL:2 * D_MODEL].astype(BF16)
    _store_values_with_ones(v_ref, qkv[:, 2 * D_MODEL:3 * D_MODEL].astype(BF16))


def _store_values_with_ones(v_ref, v):
    ones = jnp.ones((v.shape[0], PAIR_W), BF16)
    for j in range(v.shape[1] // PAIR_W):
        v_ref[:, 2 * PAIR_W * j:2 * PAIR_W * j + PAIR_W] = v[:, PAIR_W * j:PAIR_W * (j + 1)]
        v_ref[:, 2 * PAIR_W * j + PAIR_W:2 * PAIR_W * (j + 1)] = ones


def _premix_qkv(x, mod, row_of_step, g, wqkv, cache_kt, cache_vt, layer_j, rpb, tm=512):
    t = x.shape[0]
    heads_per_step = N_HEADS // (t // tm)
    left = (GRID_W - 1) - (WIN_COLS - 1)
    right = PAIR_W - (2 * WIN_COLS - 1) - left
    rpb_rows = jnp.pad(rpb, ((0, 0), (0, 0), (left, right)), mode="edge")
    steps_per_seq = DEC_SEQ // tm
    chunk = D_MODEL // steps_per_seq
    tok = pl.BlockSpec((tm, D_MODEL), lambda i: (i, 0))
    tok2 = pl.BlockSpec((tm, 2 * D_MODEL), lambda i: (i, 0))
    cache = pl.BlockSpec((None, None, chunk, PAST_LEN),
                         lambda i: (i // steps_per_seq, layer_j, i % steps_per_seq, 0))
    out = jax.ShapeDtypeStruct((t, D_MODEL), BF16)
    out2 = jax.ShapeDtypeStruct((t, 2 * D_MODEL), BF16)
    return pl.pallas_call(
        functools.partial(_premix_qkv_kernel, heads_per_step=heads_per_step),
        grid=(t // tm,),
        in_specs=[tok, _mod_spec(row_of_step), _const_spec((4, D_MODEL)),
                  _const_spec((D_MODEL, 3 * D_MODEL)), cache, cache,
                  _const_spec((N_HEADS, N_DR, PAIR_W))],
        out_specs=[tok, tok, tok2,
                   pl.BlockSpec((chunk, PAST_LEN), lambda i: (i, 0)),
                   pl.BlockSpec((PAST_LEN, 2 * chunk),
                                lambda i: (i // steps_per_seq, i % steps_per_seq)),
                   pl.BlockSpec((heads_per_step * N_DR_PAIRS, GRID_W, PAIR_W),
                                lambda i: (i, 0, 0))],
        out_shape=[out, out, out2,
                   jax.ShapeDtypeStruct((DEC_BATCH * D_MODEL, PAST_LEN), BF16),
                   jax.ShapeDtypeStruct((DEC_BATCH * PAST_LEN, 2 * D_MODEL), BF16),
                   jax.ShapeDtypeStruct((N_HEADS * N_DR_PAIRS, GRID_W, PAIR_W), F32)],
        compiler_params=_params(1),
        name="premix_qkv",
    )(x, mod, g, wqkv, cache_kt, cache_vt, rpb_rows)


def _build_bias_tiles(l_ref, first_head, n_heads, bias_ref):
    lane = lax.broadcasted_iota(jnp.int32, (GRID_W, PAIR_W), 1)
    qcol = lax.broadcasted_iota(jnp.int32, (GRID_W, PAIR_W), 0)
    kcol = lane & (GRID_W - 1)
    start = jnp.clip(qcol - WIN_COLS // 2, 0, GRID_W - WIN_COLS)
    in_window = (kcol >= start) & (kcol < start + WIN_COLS)
    lo = lane < GRID_W

    for hh in range(n_heads):
        def toeplitz(d, shift):
            row = jnp.broadcast_to(l_ref[first_head + hh, d:d + 1, :], (GRID_W, PAIR_W))
            return pltpu.roll(row, shift, 1, stride=1, stride_axis=0)

        for d in range(N_DR_PAIRS):
            both = jnp.where(lo, toeplitz(d, GRID_W + 1), toeplitz(d + 1, 1))
            bias_ref[hh * N_DR_PAIRS + d] = jnp.where(in_window, both * LOG2E, NEG_INF)


def _kv_window_start(blk, rows_per_step):
    return jnp.clip(blk * rows_per_step - WIN_ROWS // 2, 0,
                    GRID_ROWS - (rows_per_step + WIN_ROWS))


def _na_attn_kernel(q_ref, k_ref, v_ref, ckt_ref, cv_ref, bias_ref, x_ref, mod_ref, g_ref,
                    wout_ref, o_ref, s_scr, m_scr, p_scr, att_scr, *, rows_per_step):
    blk = pl.program_id(1)
    lo = lax.broadcasted_iota(jnp.int32, (GRID_W, PAIR_W), 1) < HEAD_DIM

    win_start = _kv_window_start(blk, rows_per_step)

    def row_geometry(i):
        r = blk * rows_per_step + i
        rs = jnp.clip(r - WIN_ROWS // 2, 0, GRID_ROWS - WIN_ROWS)
        d0 = rs - r + (WIN_ROWS - 1)
        q0 = pl.multiple_of(i * GRID_W, GRID_W)
        k0 = pl.multiple_of((rs - win_start) * GRID_W, GRID_W)
        return d0, q0, k0

    def scores(i, slot):
        d0, q0, k0 = row_geometry(i)
        for j in range(N_PAIRS):
            lanes = slice(PAIR_W * j, PAIR_W * (j + 1))
            qs = _split_pair(q_ref[pl.ds(q0, GRID_W), lanes], lo)
            bias = jnp.concatenate(
                [jnp.concatenate(
                    [bias_ref[(2 * j) * N_DR_PAIRS + d0 + 2 * jj],
                     bias_ref[(2 * j + 1) * N_DR_PAIRS + d0 + 2 * jj]], axis=0)
                 for jj in range(WIN_ROWS // 2)], axis=1)
            s_loc = _dot_nt(qs, k_ref[pl.ds(k0, N_LOCAL), lanes]) + bias
            s_ctx = _dot(qs, ckt_ref[lanes, :])
            mx = jnp.maximum(jnp.max(s_loc, axis=-1, keepdims=True),
                             jnp.max(s_ctx, axis=-1, keepdims=True))
            s_scr[slot, j, :, 0:N_LOCAL] = s_loc
            s_scr[slot, j, :, N_LOCAL:N_KEYS] = s_ctx
            m_scr[slot, j] = jnp.broadcast_to(mx, (2 * GRID_W, PAIR_W))

    def probs(slot):
        for j in range(N_PAIRS):
            mx = m_scr[slot, j][:, 0:1]
            p_scr[slot, j] = jnp.exp2((s_scr[slot, j] - mx).astype(BF16))

    def values(i, slot):
        _, q0, k0 = row_geometry(i)
        for j in range(N_PAIRS):
            lanes2 = slice(2 * PAIR_W * j, 2 * PAIR_W * (j + 1))
            p = p_scr[slot, j]
            o2 = (_dot(p[:, 0:N_LOCAL], v_ref[pl.ds(k0, N_LOCAL), lanes2])
                  + _dot(p[:, N_LOCAL:N_KEYS], cv_ref[:, lanes2]))
            o = o2[:, 0:PAIR_W] / o2[:, PAIR_W:2 * PAIR_W]
            att_scr[pl.ds(q0, GRID_W), PAIR_W * j:PAIR_W * (j + 1)] = (
                jnp.where(lo, o[:GRID_W], o[GRID_W:]).astype(BF16))

    scores(0, 0)
    probs(0)
    scores(1, 1)

    def two_rows(t, carry):
        i = 2 * t
        scores(i, 0)
        probs(1)
        values(i - 2, 0)
        scores(i + 1, 1)
        values(i - 1, 1)
        probs(0)
        return carry

    lax.fori_loop(1, rows_per_step // 2, two_rows, 0)
    values(rows_per_step - 2, 0)
    probs(1)
    values(rows_per_step - 1, 1)

    gt = mod_ref[...][:, 2 * D_MODEL:3 * D_MODEL]
    mix = _dot(att_scr[...], wout_ref[...])
    o_ref[...] = x_ref[...] + gt * _rms(mix, g_ref[1:2, :])


def _na_attn(q, k, v, ck, cv, bias, x, mod, row_of_batch, g, wout, rows_per_step=8):
    tm = rows_per_step * GRID_W
    n_t = DEC_SEQ // tm
    tok = pl.BlockSpec((tm, D_MODEL), lambda b, t: (b * n_t + t, 0))

    def per_batch(rows, width):
        return pl.BlockSpec((rows, width), lambda b, t: (b, 0), pipeline_mode=pl.Buffered(1))

    def kv_window(width):
        def start(b, t):
            row = _kv_window_start(t, rows_per_step)
            return (pl.multiple_of(b * DEC_SEQ + row * GRID_W, GRID_W), 0)
        return pl.BlockSpec(
            (pl.Element((rows_per_step + WIN_ROWS) * GRID_W), pl.Element(width)), start)

    return pl.pallas_call(
        functools.partial(_na_attn_kernel, rows_per_step=rows_per_step),
        grid=(DEC_BATCH, n_t),
        in_specs=[
            tok,
            kv_window(D_MODEL), kv_window(2 * D_MODEL),
            per_batch(D_MODEL, PAST_LEN), per_batch(PAST_LEN, 2 * D_MODEL),
            _const_spec((N_HEADS * N_DR_PAIRS, GRID_W, PAIR_W)),
            tok,
            _mod_spec(lambda b, t: row_of_batch(b)),
            _const_spec((4, D_MODEL)),
            _const_spec((D_MODEL, D_MODEL)),
        ],
        out_specs=tok,
        out_shape=jax.ShapeDtypeStruct((DEC_BATCH * DEC_SEQ, D_MODEL), F32),
        scratch_shapes=[
            pltpu.VMEM((2, N_PAIRS, 2 * GRID_W, N_KEYS), F32),
            pltpu.VMEM((2, N_PAIRS, 2 * GRID_W, PAIR_W), F32),
            pltpu.VMEM((2, N_PAIRS, 2 * GRID_W, N_KEYS), BF16),
            pltpu.VMEM((tm, D_MODEL), BF16),
        ],
        compiler_params=_params(2),
        name="na_attn",
    )(q, k, v, ck, cv, bias, x, mod, g, wout)


def kernel(x_prompt, x_sample, c, cache_k, cache_v, c_ctx, ada_w, ada_b, norm_g, fourier_w_out,
           na_w_qkv, na_rpb, na_w_out, ffn_w_gate, ffn_w_up, ffn_w_down):
    n_p = BATCH * SEQ
    n_s = DEC_BATCH * DEC_SEQ
    xp = x_prompt.reshape(n_p, D_MODEL)
    xs = x_sample.reshape(n_s, D_MODEL)

    cond = jnp.concatenate(
        [c_ctx[None, :], c, jnp.zeros((COND_ROWS - 1 - DEC_BATCH, D_MODEL), F32)], axis=0)
    mods = {0: _modulation(cond, ada_w, ada_b, 0)}

    cos_g, sin_g = _dft_cos_sin(FOURIER_GROUP)
    cs_chan = jnp.asarray(np.concatenate([cos_g, sin_g], axis=1), F32).astype(BF16)
    cos_p, sin_p = (jnp.asarray(a, F32).astype(BF16) for a in _dft_cos_sin(SEQ))
    cos_s, sin_s = (jnp.asarray(_half_spectrum_blocks(a), F32).astype(BF16)
                    for a in _dft_cos_sin(DEC_SEQ))
    rev = jnp.asarray(_reversal_matrix(), F32).astype(BF16)

    tm, ffn_tm = TOKEN_TILE, FFN_TILE
    ffn_weights = (ffn_w_gate, ffn_w_up, ffn_w_down)
    cache_kt = jnp.transpose(cache_k, (0, 1, 3, 4, 2)).reshape(DEC_BATCH, -1, D_MODEL, PAST_LEN)
    cache_vt = jnp.transpose(cache_v, (0, 1, 3, 4, 2)).reshape(DEC_BATCH, -1, D_MODEL, PAST_LEN)
    new_kt = new_vt = None
    attn_weights = None
    for layer in range(DEPTH):
        g = norm_g[layer]
        if layer not in mods:
            mods[layer] = _modulation(cond, ada_w, ada_b, layer)
        mod = mods[layer]
        prompt_row = lambda *_: 0
        sample_row_of_batch = lambda b: 1 + b
        sample_row_of_tile = lambda i: 1 + i // (DEC_SEQ // tm)
        sample_row_of_ffn_tile = lambda i: 1 + i // (DEC_SEQ // ffn_tm)

        if layer % 2 == 0:
            w_out = fourier_w_out[layer // 2]
            jobs = [(w, layer) for w in ffn_weights] if layer == 0 else []
            xp, cast_p = _fourier_mixer(cos_p, sin_p, cs_chan, xp, mod, prompt_row, g, w_out,
                                        BATCH, SEQ, n_sub=4, cast_jobs=jobs[:2])
            xs, cast_s = _fourier_mixer_long(cos_s, sin_s, rev, cs_chan, xs, mod,
                                             sample_row_of_batch, g, w_out, DEC_BATCH, DEC_SEQ,
                                             cast_jobs=jobs[2:])
            if layer == 0:
                wg, wu, wd = list(cast_p) + list(cast_s)
        else:
            j = layer // 2
            if attn_weights is None:
                attn_weights = (na_w_qkv[j].astype(BF16), na_w_out[j].astype(BF16))
            w_qkv, w_out = attn_weights
            attn_weights = None
            xp, new_kt, new_vt = _prompt_attn(xp, mod, 0, g, w_qkv, w_out)
            q, k, v, ckt, cvt, bias = _premix_qkv(xs, mod, sample_row_of_tile, g, w_qkv,
                                                  cache_kt, cache_vt, j, na_rpb[j], tm)
            xs = _na_attn(q, k, v, ckt, cvt, bias, xs, mod, sample_row_of_batch, g, w_out)

        nxt = layer + 1
        ffn_jobs = [(w, nxt) for w in ffn_weights] if nxt < DEPTH else []
        attn_jobs = ([(na_w_qkv, nxt // 2), (na_w_out, nxt // 2)]
                     if nxt < DEPTH and nxt % 2 == 1 else [])
        mod_job = (cond, ada_w, ada_b, nxt) if nxt < DEPTH else None
        busy = bool(attn_jobs) or mod_job is not None
        xp, cast_ffn, _ = _ffn(xp, mod, prompt_row, g, wg, wu, wd, tm if ffn_jobs else ffn_tm,
                               cast_jobs=ffn_jobs)
        xs, cast_attn, next_mod = _ffn(
            xs, mod, sample_row_of_tile if busy else sample_row_of_ffn_tile, g, wg, wu, wd,
            tm if busy else ffn_tm, cast_jobs=attn_jobs, mod_job=mod_job)
        if ffn_jobs:
            wg, wu, wd = cast_ffn
        if attn_jobs:
            attn_weights = tuple(cast_attn)
        if mod_job is not None:
            mods[nxt] = next_mod

    new_k = jnp.transpose(new_kt, (0, 1, 4, 2, 3))
    new_v = jnp.transpose(new_vt, (0, 1, 4, 2, 3))
    return (xp.reshape(BATCH, SEQ, D_MODEL), xs.reshape(DEC_BATCH, DEC_SEQ, D_MODEL), new_k, new_v)
```

```python
import functools

import numpy as np
import jax
import jax.numpy as jnp
from jax import lax
from jax.experimental import pallas as pl
from jax.experimental.pallas import tpu as pltpu

D_MODEL = 1024
BATCH = 32
SEQ = 256
DEPTH = 2
DEC_BATCH = 2
DEC_SEQ = 2048
PAST_LEN = 512
GRID_W = 64
GRID_ROWS = DEC_SEQ // GRID_W
N_HEADS = 16
HEAD_DIM = D_MODEL // N_HEADS
N_FOURIER_GROUPS = 4
FOURIER_GROUP = D_MODEL // N_FOURIER_GROUPS
WIN_ROWS = 8
WIN_COLS = 16
D_FF = 2816
EPS = 1e-6
NEG_INF = -1e30

N_PAIRS = N_HEADS // 2
PAIR_W = 2 * HEAD_DIM
N_DR = 2 * WIN_ROWS - 1
N_DR_PAIRS = N_DR - 1
N_LOCAL = WIN_ROWS * GRID_W
N_KEYS = N_LOCAL + PAST_LEN
COND_ROWS = 8
LOG2E = 1.4426950408889634
Q_SCALE = HEAD_DIM ** -0.5 * LOG2E

VMEM_LIMIT = 56 * 1024 * 1024
BF16_SUBLANES = 16
TOKEN_TILE = 512
FFN_TILE = 1024
MOD_K_STEPS = 4

F32 = jnp.float32
BF16 = jnp.bfloat16


def _dot(a, b):
    return jnp.dot(a, b, preferred_element_type=F32)


def _dot_nt(a, b):
    return lax.dot_general(a, b, (((1,), (1,)), ((), ())), preferred_element_type=F32)


def _rms(x, g):
    ms = jnp.mean(x * x, axis=-1, keepdims=True)
    return x * lax.rsqrt(ms + EPS) * g


def _silu(x):
    return x / (1.0 + jnp.exp(-x))


def _const_spec(shape):
    return pl.BlockSpec(shape, lambda *_: (0,) * len(shape), pipeline_mode=pl.Buffered(1))


def _mod_spec(row_of_step):
    return pl.BlockSpec((None, 1, 6 * D_MODEL), lambda *idx: (row_of_step(*idx), 0, 0))


def _params(n_axes):
    return pltpu.CompilerParams(
        dimension_semantics=("arbitrary",) * n_axes, vmem_limit_bytes=VMEM_LIMIT)


def _mod_accumulate(cond_ref, w_ref, b_ref, o_ref, acc_ref, layer, step, n_steps):
    @pl.when(step == 0)
    def _():
        acc_ref[...] = jnp.broadcast_to(b_ref[layer:layer + 1, :], acc_ref.shape)

    a = _silu(cond_ref[...])
    a_hi = a.astype(BF16)
    a_lo = (a - a_hi.astype(F32)).astype(BF16)
    r = _dot(jnp.concatenate([a_hi, a_lo], axis=0), w_ref[...].astype(BF16))
    acc_ref[...] += r[:COND_ROWS] + r[COND_ROWS:]

    @pl.when(step == n_steps - 1)
    def _():
        for row in range(COND_ROWS):
            o_ref[row] = acc_ref[row:row + 1, :]


def _mod_specs(layer, n_steps, step_of):
    tk = D_MODEL // n_steps
    n_out = 6 * D_MODEL
    in_specs = [
        pl.BlockSpec((COND_ROWS, tk), lambda *idx: (0, step_of(*idx))),
        pl.BlockSpec((None, tk, n_out), lambda *idx: (layer, step_of(*idx), 0)),
        pl.BlockSpec((DEPTH, n_out), lambda *idx: (0, 0)),
    ]
    out_spec = pl.BlockSpec((COND_ROWS, 1, n_out), lambda *idx: (0, 0, 0))
    out_shape = jax.ShapeDtypeStruct((COND_ROWS, 1, n_out), F32)
    scratch = pltpu.VMEM((COND_ROWS, n_out), F32)
    return in_specs, out_spec, out_shape, scratch


def _mod_kernel(cond_ref, w_ref, b_ref, o_ref, acc_ref, *, layer):
    _mod_accumulate(cond_ref, w_ref, b_ref, o_ref, acc_ref, layer,
                    pl.program_id(0), pl.num_programs(0))


def _modulation(cond, ada_w, ada_b, layer):
    n_steps = MOD_K_STEPS
    in_specs, out_spec, out_shape, scratch = _mod_specs(layer, n_steps, lambda k: k)
    return pl.pallas_call(
        functools.partial(_mod_kernel, layer=layer),
        grid=(n_steps,),
        in_specs=in_specs,
        out_specs=out_spec,
        out_shape=out_shape,
        scratch_shapes=[scratch],
        compiler_params=_params(1),
        name="modulation",
    )(cond, ada_w, ada_b)


FFN_SUB_ROWS = 256


def _cast_specs(jobs, n_steps, step_of):
    in_specs, out_specs, out_shapes = [], [], []
    for src, layer in jobs:
        rows, cols = src.shape[1:]
        chunk = rows // n_steps
        assert chunk * n_steps == rows and chunk % BF16_SUBLANES == 0
        in_specs.append(pl.BlockSpec(
            (None, chunk, cols), lambda *idx, layer=layer: (layer, step_of(*idx), 0)))
        out_specs.append(pl.BlockSpec((chunk, cols), lambda *idx: (step_of(*idx), 0)))
        out_shapes.append(jax.ShapeDtypeStruct((rows, cols), BF16))
    return in_specs, out_specs, out_shapes


def _run_cast_jobs(src_refs, dst_refs):
    for src, dst in zip(src_refs, dst_refs):
        dst[...] = src[...].astype(BF16)


def _ffn_kernel(x_ref, mod_ref, g_ref, wg_ref, wu_ref, wd_ref, *rest, n_cast, mod_layer):
    n_mod = 0 if mod_layer is None else 1
    n_in = n_cast + 3 * n_mod
    o_ref = rest[n_in]
    _run_cast_jobs(rest[:n_cast], rest[n_in + 1:n_in + 1 + n_cast])
    if n_mod:
        _mod_accumulate(*rest[n_cast:n_in], rest[n_in + 1 + n_cast], rest[n_in + 2 + n_cast],
                        mod_layer, pl.program_id(0), pl.num_programs(0))
    m = mod_ref[...]
    sh = m[:, 3 * D_MODEL:4 * D_MODEL]
    sc = m[:, 4 * D_MODEL:5 * D_MODEL]
    gt = m[:, 5 * D_MODEL:6 * D_MODEL]
    for r0 in range(0, x_ref.shape[0], FFN_SUB_ROWS):
        rows = slice(r0, r0 + FFN_SUB_ROWS)
        x = x_ref[rows, :]
        f = (_rms(x, g_ref[2:3, :]) * (1.0 + sc) + sh).astype(BF16)
        gate = _dot(f, wg_ref[...])
        up = _dot(f, wu_ref[...])
        act = (_silu(gate) * up).astype(BF16)
        y = _dot(act, wd_ref[...])
        o_ref[rows, :] = x + gt * _rms(y, g_ref[3:4, :])


def _ffn(x, mod, row_of_step, g, wg, wu, wd, tm, cast_jobs=(), mod_job=None):
    t = x.shape[0]
    n_steps = t // tm
    tok = pl.BlockSpec((tm, D_MODEL), lambda i: (i, 0))
    cast_in, cast_out, cast_shapes = _cast_specs(cast_jobs, n_steps, lambda i: i)
    mod_in, mod_out, mod_shapes, mod_scratch, mod_args, mod_layer = [], [], [], [], (), None
    if mod_job is not None:
        mod_layer = mod_job[3]
        mod_in, out_spec, out_shape, scratch = _mod_specs(mod_layer, n_steps, lambda i: i)
        mod_out, mod_shapes, mod_scratch, mod_args = [out_spec], [out_shape], [scratch], mod_job[:3]
    outs = pl.pallas_call(
        functools.partial(_ffn_kernel, n_cast=len(cast_jobs), mod_layer=mod_layer),
        grid=(n_steps,),
        in_specs=[
            tok,
            _mod_spec(row_of_step),
            _const_spec((4, D_MODEL)),
            _const_spec((D_MODEL, D_FF)),
            _const_spec((D_MODEL, D_FF)),
            _const_spec((D_FF, D_MODEL)),
        ] + cast_in + mod_in,
        out_specs=[tok] + cast_out + mod_out,
        out_shape=[jax.ShapeDtypeStruct((t, D_MODEL), F32)] + cast_shapes + mod_shapes,
        scratch_shapes=mod_scratch,
        compiler_params=_params(1),
        name="ffn",
    )(x, mod, g, wg, wu, wd, *[src for src, _ in cast_jobs], *mod_args)
    n_cast = len(cast_jobs)
    return outs[0], outs[1:1 + n_cast], (outs[1 + n_cast] if mod_job is not None else None)


def _ffn_pair_kernel(xp_hbm, xs_hbm, mod_ref, g_ref, wg_ref, wu_ref, wd_ref, op_hbm, os_hbm,
                     xbuf, obuf, sem_in, sem_out, *, n_p, tm):
    i = pl.program_id(0)
    n = pl.num_programs(0)
    slot = i % 2

    def tile_copy(step, s, inbound, act):
        def build(hbm, t):
            rows = pl.ds(pl.multiple_of(t * tm, tm), tm)
            if inbound:
                return pltpu.make_async_copy(hbm.at[rows, :], xbuf.at[s], sem_in.at[s])
            return pltpu.make_async_copy(obuf.at[s], hbm.at[rows, :], sem_out.at[s])

        @pl.when(step < n_p)
        def _():
            act(build(xp_hbm if inbound else op_hbm, step))

        @pl.when(step >= n_p)
        def _():
            act(build(xs_hbm if inbound else os_hbm, step - n_p))

    start = lambda copy: copy.start()
    wait = lambda copy: copy.wait()

    @pl.when(i == 0)
    def _():
        tile_copy(i, slot, True, start)

    @pl.when(i + 1 < n)
    def _():
        tile_copy(i + 1, 1 - slot, True, start)

    tile_copy(i, slot, True, wait)

    @pl.when(i >= 2)
    def _():
        tile_copy(i - 2, slot, False, wait)

    m = mod_ref[...]
    sh = m[:, 3 * D_MODEL:4 * D_MODEL]
    sc = m[:, 4 * D_MODEL:5 * D_MODEL]
    gt = m[:, 5 * D_MODEL:6 * D_MODEL]
    for r0 in range(0, tm, FFN_SUB_ROWS):
        rows = slice(r0, r0 + FFN_SUB_ROWS)
        x = xbuf[slot, rows, :]
        f = (_rms(x, g_ref[2:3, :]) * (1.0 + sc) + sh).astype(BF16)
        act = (_silu(_dot(f, wg_ref[...])) * _dot(f, wu_ref[...])).astype(BF16)
        y = _dot(act, wd_ref[...])
        obuf[slot, rows, :] = x + gt * _rms(y, g_ref[3:4, :])

    tile_copy(i, slot, False, start)

    @pl.when(i == n - 1)
    def _():
        tile_copy(i - 1, 1 - slot, False, wait)
        tile_copy(i, slot, False, wait)


def _ffn_pair(xp, xs, mod, g, wg, wu, wd, tm):
    n_p, n_s = xp.shape[0] // tm, xs.shape[0] // tm
    assert n_p >= 1 and n_s >= 1 and n_p + n_s >= 2
    any_spec = pl.BlockSpec(memory_space=pl.ANY)
    row_of_step = lambda i: jnp.where(i < n_p, 0, 1 + (i - n_p) // (DEC_SEQ // tm))
    return pl.pallas_call(
        functools.partial(_ffn_pair_kernel, n_p=n_p, tm=tm),
        grid=(n_p + n_s,),
        in_specs=[any_spec, any_spec, _mod_spec(row_of_step), _const_spec((4, D_MODEL)),
                  _const_spec((D_MODEL, D_FF)), _const_spec((D_MODEL, D_FF)),
                  _const_spec((D_FF, D_MODEL))],
        out_specs=[any_spec, any_spec],
        out_shape=[jax.ShapeDtypeStruct(xp.shape, F32), jax.ShapeDtypeStruct(xs.shape, F32)],
        scratch_shapes=[pltpu.VMEM((2, tm, D_MODEL), F32), pltpu.VMEM((2, tm, D_MODEL), F32),
                        pltpu.SemaphoreType.DMA((2,)), pltpu.SemaphoreType.DMA((2,))],
        compiler_params=_params(1),
        name="ffn_pair",
    )(xp, xs, mod, g, wg, wu, wd)


def _dft_cos_sin(n):
    j = np.arange(n)
    ang = 2.0 * np.pi * ((j[:, None] * j[None, :]) % n) / n
    return np.cos(ang), np.sin(ang)


PREMIX_CHUNK = 512


def _premix_channel_dft(x_ref, n_rows, sc, sh, g_ref, cs_ref, xc_scr, xs_scr):
    cs = cs_ref[...]
    for r0 in range(0, n_rows, PREMIX_CHUNK):
        rows = slice(r0, r0 + PREMIX_CHUNK)
        h = (_rms(x_ref[rows, :], g_ref[0:1, :]) * (1.0 + sc) + sh).astype(BF16)
        for j in range(N_FOURIER_GROUPS):
            lanes = slice(FOURIER_GROUP * j, FOURIER_GROUP * (j + 1))
            r = _dot(h[:, lanes], cs)
            xc_scr[rows, lanes] = r[:, :FOURIER_GROUP].astype(BF16)
            xs_scr[rows, lanes] = r[:, FOURIER_GROUP:].astype(BF16)


def _fourier_kernel(c_ref, s_ref, cs_ref, x_ref, mod_ref, g_ref, w_ref, *rest,
                    n_sub, seq, n_cast):
    o_ref = rest[n_cast]
    xc_scr, xs_scr, f_scr, w_scr = rest[2 * n_cast + 1:]
    _run_cast_jobs(rest[:n_cast], rest[n_cast + 1:2 * n_cast + 1])
    m = mod_ref[...]
    sh = m[:, 0:D_MODEL]
    sc = m[:, D_MODEL:2 * D_MODEL]
    gt = m[:, 2 * D_MODEL:3 * D_MODEL]

    @pl.when(pl.program_id(0) == 0)
    def _():
        w_scr[...] = w_ref[...].astype(BF16)

    _premix_channel_dft(x_ref, n_sub * seq, sc, sh, g_ref, cs_ref, xc_scr, xs_scr)
    for b in range(n_sub):
        rows = slice(seq * b, seq * (b + 1))
        f = _dot(c_ref[...], xc_scr[rows, :]) - _dot(s_ref[...], xs_scr[rows, :])
        f_scr[rows, :] = f.astype(BF16)
    mix = _dot(f_scr[...], w_scr[...])
    o_ref[...] = x_ref[...] + gt * _rms(mix, g_ref[1:2, :])


def _fourier_mixer(cmat, smat, cs, x, mod, row_of_group, g, w, n_seqs, seq, n_sub, cast_jobs=()):
    n_groups = n_seqs // n_sub
    rows = n_sub * seq
    tok = pl.BlockSpec((rows, D_MODEL), lambda b: (b, 0))
    cast_in, cast_out, cast_shapes = _cast_specs(cast_jobs, n_groups, lambda b: b)
    outs = pl.pallas_call(
        functools.partial(_fourier_kernel, n_sub=n_sub, seq=seq, n_cast=len(cast_jobs)),
        grid=(n_groups,),
        in_specs=[
            _const_spec((seq, seq)),
            _const_spec((seq, seq)),
            _const_spec((FOURIER_GROUP, 2 * FOURIER_GROUP)),
            tok,
            _mod_spec(row_of_group),
            _const_spec((4, D_MODEL)),
            _const_spec((D_MODEL, D_MODEL)),
        ] + cast_in,
        out_specs=[tok] + cast_out,
        out_shape=[jax.ShapeDtypeStruct((n_seqs * seq, D_MODEL), F32)] + cast_shapes,
        scratch_shapes=[pltpu.VMEM((rows, D_MODEL), BF16), pltpu.VMEM((rows, D_MODEL), BF16),
                        pltpu.VMEM((rows, D_MODEL), BF16),
                        pltpu.VMEM((D_MODEL, D_MODEL), BF16)],
        compiler_params=_params(1),
        name="fourier_mixer",
    )(cmat, smat, cs, x, mod, g, w, *[src for src, _ in cast_jobs])
    return outs[0], outs[1:]


HERM_TILE = 512
HERM_ROWS = HERM_TILE + BF16_SUBLANES


def _half_spectrum_blocks(mat):
    n_half = mat.shape[0] // 2 // HERM_TILE
    return np.stack([mat[HERM_TILE * t:HERM_TILE * t + HERM_ROWS] for t in range(n_half)])


def _reversal_matrix():
    rev = np.zeros((HERM_TILE, HERM_ROWS), np.float32)
    rev[np.arange(HERM_TILE), HERM_TILE - np.arange(HERM_TILE)] = 1.0
    return rev


def _fourier_long_kernel(ch_ref, sh_ref, rev_ref, cs_ref, x_ref, mod_ref, g_ref, w_ref, *rest,
                         seq, n_cast):
    o_ref = rest[n_cast]
    xc_scr, xs_scr, pq_scr, f_scr, w_scr = rest[2 * n_cast + 1:]
    _run_cast_jobs(rest[:n_cast], rest[n_cast + 1:2 * n_cast + 1])
    b, t, u = pl.program_id(0), pl.program_id(1), pl.program_id(2)
    m = mod_ref[...]
    sh = m[:, 0:D_MODEL]
    sc = m[:, D_MODEL:2 * D_MODEL]
    gt = m[:, 2 * D_MODEL:3 * D_MODEL]

    @pl.when((b == 0) & (t == 0) & (u == 0))
    def _():
        w_scr[...] = w_ref[...].astype(BF16)

    @pl.when((t == 0) & (u == 0))
    def _():
        _premix_channel_dft(x_ref, seq, sc, sh, g_ref, cs_ref, xc_scr, xs_scr)

    @pl.when(u == 0)
    def _():
        p = _dot(ch_ref[...], xc_scr[...])
        q = _dot(sh_ref[...], xs_scr[...])
        f_scr[...] = (p - q)[0:HERM_TILE].astype(BF16)
        pq_scr[...] = (p + q).astype(BF16)

    @pl.when(u == 1)
    def _():
        f_scr[...] = _dot(rev_ref[...], pq_scr[...]).astype(BF16)

    n_tiles = seq // HERM_TILE
    tile = t + u * (n_tiles - 1 - 2 * t)
    mix = _dot(f_scr[...], w_scr[...])
    x = x_ref[pl.ds(pl.multiple_of(tile * HERM_TILE, HERM_TILE), HERM_TILE), :]
    o_ref[...] = x + gt * _rms(mix, g_ref[1:2, :])


def _fourier_mixer_long(cos_h, sin_h, rev, cs, x, mod, row_of_seq, g, w, n_seqs, seq, cast_jobs=()):
    n_tiles = seq // HERM_TILE
    n_half = n_tiles // 2
    tile_of = lambda t, u: t + u * (n_tiles - 1 - 2 * t)
    half = pl.BlockSpec((None, HERM_ROWS, seq), lambda b, t, u: (t, 0, 0))
    cast_in, cast_out, cast_shapes = _cast_specs(
        cast_jobs, n_seqs * n_tiles, lambda b, t, u: (b * n_half + t) * 2 + u)
    outs = pl.pallas_call(
        functools.partial(_fourier_long_kernel, seq=seq, n_cast=len(cast_jobs)),
        grid=(n_seqs, n_half, 2),
        in_specs=[
            half, half,
            _const_spec((HERM_TILE, HERM_ROWS)),
            _const_spec((FOURIER_GROUP, 2 * FOURIER_GROUP)),
            pl.BlockSpec((seq, D_MODEL), lambda b, t, u: (b, 0)),
            _mod_spec(lambda b, t, u: row_of_seq(b)),
            _const_spec((4, D_MODEL)),
            _const_spec((D_MODEL, D_MODEL)),
        ] + cast_in,
        out_specs=[pl.BlockSpec((HERM_TILE, D_MODEL),
                                lambda b, t, u: (b * n_tiles + tile_of(t, u), 0))] + cast_out,
        out_shape=[jax.ShapeDtypeStruct((n_seqs * seq, D_MODEL), F32)] + cast_shapes,
        scratch_shapes=[pltpu.VMEM((seq, D_MODEL), BF16), pltpu.VMEM((seq, D_MODEL), BF16),
                        pltpu.VMEM((HERM_ROWS, D_MODEL), BF16),
                        pltpu.VMEM((HERM_TILE, D_MODEL), BF16),
                        pltpu.VMEM((D_MODEL, D_MODEL), BF16)],
        compiler_params=_params(3),
        name="fourier_mixer_long",
    )(cos_h, sin_h, rev, cs, x, mod, g, w, *[src for src, _ in cast_jobs])
    return outs[0], outs[1:]


def _split_pair(q, lo):
    zero = jnp.zeros_like(q)
    return jnp.concatenate([jnp.where(lo, q, zero), jnp.where(lo, zero, q)], axis=0)


def _prompt_attn_kernel(x_ref, mod_ref, g_ref, wq_ref, wk_ref, wv_ref, wout_ref,
                        o_ref, kt_ref, vt_ref, wkt_ref, wvt_ref, h_scr, q_scr, kt_scr, vt_scr,
                        att_scr, *, n_seq):
    @pl.when(pl.program_id(0) == 0)
    def _():
        wkt_ref[...] = wk_ref[...].T
        wvt_ref[...] = wv_ref[...].T

    x = x_ref[...]
    m = mod_ref[...]
    sh = m[:, 0:D_MODEL]
    sc = m[:, D_MODEL:2 * D_MODEL]
    gt = m[:, 2 * D_MODEL:3 * D_MODEL]
    h_scr[...] = (_rms(x, g_ref[0:1, :]) * (1.0 + sc) + sh).astype(BF16)
    q_scr[...] = (_dot(h_scr[...], wq_ref[...]) * Q_SCALE).astype(BF16)

    lo = lax.broadcasted_iota(jnp.int32, (SEQ, PAIR_W), 1) < HEAD_DIM
    ones = jnp.ones((PAIR_W, SEQ), BF16)
    for b in range(n_seq):
        rows = slice(SEQ * b, SEQ * (b + 1))
        kt = _dot_nt(wkt_ref[...], h_scr[rows, :])
        vt = _dot_nt(wvt_ref[...], h_scr[rows, :])
        kt_ref[b] = kt.reshape(N_HEADS, HEAD_DIM, SEQ)
        vt_ref[b] = vt.reshape(N_HEADS, HEAD_DIM, SEQ)
        kt_scr[b] = kt.astype(BF16)
        vt_scr[b] = vt.astype(BF16)
        for j in range(N_PAIRS):
            lanes = slice(PAIR_W * j, PAIR_W * (j + 1))
            qs = _split_pair(q_scr[rows, lanes], lo)
            s = _dot(qs, kt_scr[b, lanes, :])
            p = jnp.exp2((s - jnp.max(s, axis=-1, keepdims=True)).astype(BF16))
            o2 = _dot_nt(p, jnp.concatenate([vt_scr[b, lanes, :], ones], axis=0))
            o = o2[:, 0:PAIR_W] / o2[:, PAIR_W:2 * PAIR_W]
            att_scr[rows, lanes] = jnp.where(lo, o[:SEQ], o[SEQ:]).astype(BF16)

    mix = _dot(att_scr[...], wout_ref[...])
    o_ref[...] = x + gt * _rms(mix, g_ref[1:2, :])


def _prompt_attn(x, mod, row, g, wqkv, wout, n_seq=4):
    t = x.shape[0]
    tm = n_seq * SEQ
    tok = pl.BlockSpec((tm, D_MODEL), lambda i: (i, 0))
    out = jax.ShapeDtypeStruct((t, D_MODEL), F32)
    cache = pl.BlockSpec((n_seq, None, N_HEADS, HEAD_DIM, SEQ), lambda i: (i, 0, 0, 0, 0))
    cache_out = jax.ShapeDtypeStruct((t // SEQ, 1, N_HEADS, HEAD_DIM, SEQ), F32)

    def qkv_part(n):
        return pl.BlockSpec((D_MODEL, D_MODEL), lambda i: (0, n), pipeline_mode=pl.Buffered(1))

    return pl.pallas_call(
        functools.partial(_prompt_attn_kernel, n_seq=n_seq),
        grid=(t // tm,),
        in_specs=[tok, _mod_spec(lambda i: row), _const_spec((4, D_MODEL)),
                  qkv_part(0), qkv_part(1), qkv_part(2), _const_spec((D_MODEL, D_MODEL))],
        out_specs=[tok, cache, cache],
        out_shape=[out, cache_out, cache_out],
        scratch_shapes=[pltpu.VMEM((D_MODEL, D_MODEL), BF16), pltpu.VMEM((D_MODEL, D_MODEL), BF16),
                        pltpu.VMEM((tm, D_MODEL), BF16), pltpu.VMEM((tm, D_MODEL), BF16),
                        pltpu.VMEM((n_seq, D_MODEL, SEQ), BF16),
                        pltpu.VMEM((n_seq, D_MODEL, SEQ), BF16),
                        pltpu.VMEM((tm, D_MODEL), BF16)],
        compiler_params=_params(1),
        name="prompt_attn",
    )(x, mod, g, wqkv, wqkv, wqkv, wout)


def _premix_qkv_kernel(x_ref, mod_ref, g_ref, wqkv_ref, ckt_ref, cvt_ref, l_ref,
                       q_ref, k_ref, v_ref, ck_ref, cv_ref, bias_ref, *, heads_per_step):
    _build_bias_tiles(l_ref, pl.program_id(0) * heads_per_step, heads_per_step, bias_ref)

    ck_ref[...] = ckt_ref[...].astype(BF16)
    _store_values_with_ones(cv_ref, cvt_ref[...].T.astype(BF16))

    x = x_ref[...]
    m = mod_ref[...]
    sh = m[:, 0:D_MODEL]
    sc = m[:, D_MODEL:2 * D_MODEL]
    h = (_rms(x, g_ref[0:1, :]) * (1.0 + sc) + sh).astype(BF16)
    qkv = _dot(h, wqkv_ref[...])
    q_ref[...] = (qkv[:, 0:D_MODEL] * Q_SCALE).astype(BF16)
    k_ref[...] = qkv[:, D_MODEL:2 * D_MODEL].astype(BF16)
    _store_values_with_ones(v_ref, qkv[:, 2 * D_MODEL:3 * D_MODEL].astype(BF16))


def _store_values_with_ones(v_ref, v):
    ones = jnp.ones((v.shape[0], PAIR_W), BF16)
    for j in range(v.shape[1] // PAIR_W):
        v_ref[:, 2 * PAIR_W * j:2 * PAIR_W * j + PAIR_W] = v[:, PAIR_W * j:PAIR_W * (j + 1)]
        v_ref[:, 2 * PAIR_W * j + PAIR_W:2 * PAIR_W * (j + 1)] = ones


def _premix_qkv(x, mod, row_of_step, g, wqkv, cache_kt, cache_vt, layer_j, rpb, tm=512):
    t = x.shape[0]
    heads_per_step = N_HEADS // (t // tm)
    left = (GRID_W - 1) - (WIN_COLS - 1)
    right = PAIR_W - (2 * WIN_COLS - 1) - left
    rpb_rows = jnp.pad(rpb, ((0, 0), (0, 0), (left, right)), mode="edge")
    steps_per_seq = DEC_SEQ // tm
    chunk = D_MODEL // steps_per_seq
    tok = pl.BlockSpec((tm, D_MODEL), lambda i: (i, 0))
    tok2 = pl.BlockSpec((tm, 2 * D_MODEL), lambda i: (i, 0))
    cache = pl.BlockSpec((None, None, chunk, PAST_LEN),
                         lambda i: (i // steps_per_seq, layer_j, i % steps_per_seq, 0))
    out = jax.ShapeDtypeStruct((t, D_MODEL), BF16)
    out2 = jax.ShapeDtypeStruct((t, 2 * D_MODEL), BF16)
    return pl.pallas_call(
        functools.partial(_premix_qkv_kernel, heads_per_step=heads_per_step),
        grid=(t // tm,),
        in_specs=[tok, _mod_spec(row_of_step), _const_spec((4, D_MODEL)),
                  _const_spec((D_MODEL, 3 * D_MODEL)), cache, cache,
                  _const_spec((N_HEADS, N_DR, PAIR_W))],
        out_specs=[tok, tok, tok2,
                   pl.BlockSpec((chunk, PAST_LEN), lambda i: (i, 0)),
                   pl.BlockSpec((PAST_LEN, 2 * chunk),
                                lambda i: (i // steps_per_seq, i % steps_per_seq)),
                   pl.BlockSpec((heads_per_step * N_DR_PAIRS, GRID_W, PAIR_W),
                                lambda i: (i, 0, 0))],
        out_shape=[out, out, out2,
                   jax.ShapeDtypeStruct((DEC_BATCH * D_MODEL, PAST_LEN), BF16),
                   jax.ShapeDtypeStruct((DEC_BATCH * PAST_LEN, 2 * D_MODEL), BF16),
                   jax.ShapeDtypeStruct((N_HEADS * N_DR_PAIRS, GRID_W, PAIR_W), F32)],
        compiler_params=_params(1),
        name="premix_qkv",
    )(x, mod, g, wqkv, cache_kt, cache_vt, rpb_rows)


def _build_bias_tiles(l_ref, first_head, n_heads, bias_ref):
    lane = lax.broadcasted_iota(jnp.int32, (GRID_W, PAIR_W), 1)
    qcol = lax.broadcasted_iota(jnp.int32, (GRID_W, PAIR_W), 0)
    kcol = lane & (GRID_W - 1)
    start = jnp.clip(qcol - WIN_COLS // 2, 0, GRID_W - WIN_COLS)
    in_window = (kcol >= start) & (kcol < start + WIN_COLS)
    lo = lane < GRID_W

    for hh in range(n_heads):
        def toeplitz(d, shift):
            row = jnp.broadcast_to(l_ref[first_head + hh, d:d + 1, :], (GRID_W, PAIR_W))
            return pltpu.roll(row, shift, 1, stride=1, stride_axis=0)

        for d in range(N_DR_PAIRS):
            both = jnp.where(lo, toeplitz(d, GRID_W + 1), toeplitz(d + 1, 1))
            bias_ref[hh * N_DR_PAIRS + d] = jnp.where(in_window, both * LOG2E, NEG_INF)


def _kv_window_start(blk, rows_per_step):
    return jnp.clip(blk * rows_per_step - WIN_ROWS // 2, 0,
                    GRID_ROWS - (rows_per_step + WIN_ROWS))


def _na_attn_kernel(q_ref, k_ref, v_ref, ckt_ref, cv_ref, bias_ref, x_ref, mod_ref, g_ref,
                    wout_ref, o_ref, s_scr, m_scr, p_scr, att_scr, *, rows_per_step):
    blk = pl.program_id(1)
    lo = lax.broadcasted_iota(jnp.int32, (GRID_W, PAIR_W), 1) < HEAD_DIM

    win_start = _kv_window_start(blk, rows_per_step)

    def row_geometry(i):
        r = blk * rows_per_step + i
        rs = jnp.clip(r - WIN_ROWS // 2, 0, GRID_ROWS - WIN_ROWS)
        d0 = rs - r + (WIN_ROWS - 1)
        q0 = pl.multiple_of(i * GRID_W, GRID_W)
        k0 = pl.multiple_of((rs - win_start) * GRID_W, GRID_W)
        return d0, q0, k0

    def scores(i, slot):
        d0, q0, k0 = row_geometry(i)
        for j in range(N_PAIRS):
            lanes = slice(PAIR_W * j, PAIR_W * (j + 1))
            qs = _split_pair(q_ref[pl.ds(q0, GRID_W), lanes], lo)
            bias = jnp.concatenate(
                [jnp.concatenate(
                    [bias_ref[(2 * j) * N_DR_PAIRS + d0 + 2 * jj],
                     bias_ref[(2 * j + 1) * N_DR_PAIRS + d0 + 2 * jj]], axis=0)
                 for jj in range(WIN_ROWS // 2)], axis=1)
            s_loc = _dot_nt(qs, k_ref[pl.ds(k0, N_LOCAL), lanes]) + bias
            s_ctx = _dot(qs, ckt_ref[lanes, :])
            mx = jnp.maximum(jnp.max(s_loc, axis=-1, keepdims=True),
                             jnp.max(s_ctx, axis=-1, keepdims=True))
            s_scr[slot, j, :, 0:N_LOCAL] = s_loc
            s_scr[slot, j, :, N_LOCAL:N_KEYS] = s_ctx
            m_scr[slot, j] = jnp.broadcast_to(mx, (2 * GRID_W, PAIR_W))

    def probs(slot):
        for j in range(N_PAIRS):
            mx = m_scr[slot, j][:, 0:1]
            p_scr[slot, j] = jnp.exp2((s_scr[slot, j] - mx).astype(BF16))

    def values(i, slot):
        _, q0, k0 = row_geometry(i)
        for j in range(N_PAIRS):
            lanes2 = slice(2 * PAIR_W * j, 2 * PAIR_W * (j + 1))
            p = p_scr[slot, j]
            o2 = (_dot(p[:, 0:N_LOCAL], v_ref[pl.ds(k0, N_LOCAL), lanes2])
                  + _dot(p[:, N_LOCAL:N_KEYS], cv_ref[:, lanes2]))
            o = o2[:, 0:PAIR_W] / o2[:, PAIR_W:2 * PAIR_W]
            att_scr[pl.ds(q0, GRID_W), PAIR_W * j:PAIR_W * (j + 1)] = (
                jnp.where(lo, o[:GRID_W], o[GRID_W:]).astype(BF16))

    scores(0, 0)
    probs(0)
    scores(1, 1)

    def two_rows(t, carry):
        i = 2 * t
        scores(i, 0)
        probs(1)
        values(i - 2, 0)
        scores(i + 1, 1)
        values(i - 1, 1)
        probs(0)
        return carry

    lax.fori_loop(1, rows_per_step // 2, two_rows, 0)
    values(rows_per_step - 2, 0)
    probs(1)
    values(rows_per_step - 1, 1)

    gt = mod_ref[...][:, 2 * D_MODEL:3 * D_MODEL]
    mix = _dot(att_scr[...], wout_ref[...])
    o_ref[...] = x_ref[...] + gt * _rms(mix, g_ref[1:2, :])


def _na_attn(q, k, v, ck, cv, bias, x, mod, row_of_batch, g, wout, rows_per_step=8):
    tm = rows_per_step * GRID_W
    n_t = DEC_SEQ // tm
    tok = pl.BlockSpec((tm, D_MODEL), lambda b, t: (b * n_t + t, 0))

    def per_batch(rows, width):
        return pl.BlockSpec((rows, width), lambda b, t: (b, 0), pipeline_mode=pl.Buffered(1))

    def kv_window(width):
        def start(b, t):
            row = _kv_window_start(t, rows_per_step)
            return (pl.multiple_of(b * DEC_SEQ + row * GRID_W, GRID_W), 0)
        return pl.BlockSpec(
            (pl.Element((rows_per_step + WIN_ROWS) * GRID_W), pl.Element(width)), start)

    return pl.pallas_call(
        functools.partial(_na_attn_kernel, rows_per_step=rows_per_step),
        grid=(DEC_BATCH, n_t),
        in_specs=[
            tok,
            kv_window(D_MODEL), kv_window(2 * D_MODEL),
            per_batch(D_MODEL, PAST_LEN), per_batch(PAST_LEN, 2 * D_MODEL),
            _const_spec((N_HEADS * N_DR_PAIRS, GRID_W, PAIR_W)),
            tok,
            _mod_spec(lambda b, t: row_of_batch(b)),
            _const_spec((4, D_MODEL)),
            _const_spec((D_MODEL, D_MODEL)),
        ],
        out_specs=tok,
        out_shape=jax.ShapeDtypeStruct((DEC_BATCH * DEC_SEQ, D_MODEL), F32),
        scratch_shapes=[
            pltpu.VMEM((2, N_PAIRS, 2 * GRID_W, N_KEYS), F32),
            pltpu.VMEM((2, N_PAIRS, 2 * GRID_W, PAIR_W), F32),
            pltpu.VMEM((2, N_PAIRS, 2 * GRID_W, N_KEYS), BF16),
            pltpu.VMEM((tm, D_MODEL), BF16),
        ],
        compiler_params=_params(2),
        name="na_attn",
    )(q, k, v, ck, cv, bias, x, mod, g, wout)


def kernel(x_prompt, x_sample, c, cache_k, cache_v, c_ctx, ada_w, ada_b, norm_g, fourier_w_out,
           na_w_qkv, na_rpb, na_w_out, ffn_w_gate, ffn_w_up, ffn_w_down):
    n_p = BATCH * SEQ
    n_s = DEC_BATCH * DEC_SEQ
    xp = x_prompt.reshape(n_p, D_MODEL)
    xs = x_sample.reshape(n_s, D_MODEL)

    cond = jnp.concatenate(
        [c_ctx[None, :], c, jnp.zeros((COND_ROWS - 1 - DEC_BATCH, D_MODEL), F32)], axis=0)
    mods = {0: _modulation(cond, ada_w, ada_b, 0)}

    cos_g, sin_g = _dft_cos_sin(FOURIER_GROUP)
    cs_chan = jnp.asarray(np.concatenate([cos_g, sin_g], axis=1), F32).astype(BF16)
    cos_p, sin_p = (jnp.asarray(a, F32).astype(BF16) for a in _dft_cos_sin(SEQ))
    cos_s, sin_s = (jnp.asarray(_half_spectrum_blocks(a), F32).astype(BF16)
                    for a in _dft_cos_sin(DEC_SEQ))
    rev = jnp.asarray(_reversal_matrix(), F32).astype(BF16)

    tm, ffn_tm = TOKEN_TILE, FFN_TILE
    ffn_weights = (ffn_w_gate, ffn_w_up, ffn_w_down)
    cache_kt = jnp.transpose(cache_k, (0, 1, 3, 4, 2)).reshape(DEC_BATCH, -1, D_MODEL, PAST_LEN)
    cache_vt = jnp.transpose(cache_v, (0, 1, 3, 4, 2)).reshape(DEC_BATCH, -1, D_MODEL, PAST_LEN)
    new_kt = new_vt = None
    attn_weights = None
    for layer in range(DEPTH):
        g = norm_g[layer]
        if layer not in mods:
            mods[layer] = _modulation(cond, ada_w, ada_b, layer)
        mod = mods[layer]
        prompt_row = lambda *_: 0
        sample_row_of_batch = lambda b: 1 + b
        sample_row_of_tile = lambda i: 1 + i // (DEC_SEQ // tm)
        sample_row_of_ffn_tile = lambda i: 1 + i // (DEC_SEQ // ffn_tm)

        if layer % 2 == 0:
            w_out = fourier_w_out[layer // 2]
            jobs = [(w, layer) for w in ffn_weights] if layer == 0 else []
            xp, cast_p = _fourier_mixer(cos_p, sin_p, cs_chan, xp, mod, prompt_row, g, w_out,
                                        BATCH, SEQ, n_sub=4, cast_jobs=jobs[:2])
            xs, cast_s = _fourier_mixer_long(cos_s, sin_s, rev, cs_chan, xs, mod,
                                             sample_row_of_batch, g, w_out, DEC_BATCH, DEC_SEQ,
                                             cast_jobs=jobs[2:])
            if layer == 0:
                wg, wu, wd = list(cast_p) + list(cast_s)
        else:
            j = layer // 2
            if attn_weights is None:
                attn_weights = (na_w_qkv[j].astype(BF16), na_w_out[j].astype(BF16))
            w_qkv, w_out = attn_weights
            attn_weights = None
            xp, new_kt, new_vt = _prompt_attn(xp, mod, 0, g, w_qkv, w_out)
            q, k, v, ckt, cvt, bias = _premix_qkv(xs, mod, sample_row_of_tile, g, w_qkv,
                                                  cache_kt, cache_vt, j, na_rpb[j], tm)
            xs = _na_attn(q, k, v, ckt, cvt, bias, xs, mod, sample_row_of_batch, g, w_out)

        nxt = layer + 1
        ffn_jobs = [(w, nxt) for w in ffn_weights] if nxt < DEPTH else []
        attn_jobs = ([(na_w_qkv, nxt // 2), (na_w_out, nxt // 2)]
                     if nxt < DEPTH and nxt % 2 == 1 else [])
        mod_job = (cond, ada_w, ada_b, nxt) if nxt < DEPTH else None
        busy = bool(attn_jobs) or mod_job is not None
        if not ffn_jobs and not busy:
            xp, xs = _ffn_pair(xp, xs, mod, g, wg, wu, wd, ffn_tm)
            continue
        xp, cast_ffn, _ = _ffn(xp, mod, prompt_row, g, wg, wu, wd, tm if ffn_jobs else ffn_tm,
                               cast_jobs=ffn_jobs)
        xs, cast_attn, next_mod = _ffn(
            xs, mod, sample_row_of_tile if busy else sample_row_of_ffn_tile, g, wg, wu, wd,
            tm if busy else ffn_tm, cast_jobs=attn_jobs, mod_job=mod_job)
        if ffn_jobs:
            wg, wu, wd = cast_ffn
        if attn_jobs:
            attn_weights = tuple(cast_attn)
        if mod_job is not None:
            mods[nxt] = next_mod

    new_k = jnp.transpose(new_kt, (0, 1, 4, 2, 3))
    new_v = jnp.transpose(new_vt, (0, 1, 4, 2, 3))
    return (xp.reshape(BATCH, SEQ, D_MODEL), xs.reshape(DEC_BATCH, DEC_SEQ, D_MODEL), new_k, new_v)
```

```python
import functools

import numpy as np
import jax
import jax.numpy as jnp
from jax import lax
from jax.experimental import pallas as pl
from jax.experimental.pallas import tpu as pltpu

D_MODEL = 1024
BATCH = 32
SEQ = 256
DEPTH = 2
DEC_BATCH = 2
DEC_SEQ = 2048
PAST_LEN = 512
GRID_W = 64
GRID_ROWS = DEC_SEQ // GRID_W
N_HEADS = 16
HEAD_DIM = D_MODEL // N_HEADS
N_FOURIER_GROUPS = 4
FOURIER_GROUP = D_MODEL // N_FOURIER_GROUPS
WIN_ROWS = 8
WIN_COLS = 16
D_FF = 2816
EPS = 1e-6
NEG_INF = -1e30

N_PAIRS = N_HEADS // 2
PAIR_W = 2 * HEAD_DIM
N_DR = 2 * WIN_ROWS - 1
N_DR_PAIRS = N_DR - 1
N_LOCAL = WIN_ROWS * GRID_W
N_KEYS = N_LOCAL + PAST_LEN
COND_ROWS = 8
LOG2E = 1.4426950408889634
Q_SCALE = HEAD_DIM ** -0.5 * LOG2E

VMEM_LIMIT = 56 * 1024 * 1024
BF16_SUBLANES = 16
TOKEN_TILE = 512
FFN_TILE = 1024
MOD_K_STEPS = 4

F32 = jnp.float32
BF16 = jnp.bfloat16


def _dot(a, b):
    return jnp.dot(a, b, preferred_element_type=F32)


def _dot_nt(a, b):
    return lax.dot_general(a, b, (((1,), (1,)), ((), ())), preferred_element_type=F32)


def _rms(x, g):
    ms = jnp.mean(x * x, axis=-1, keepdims=True)
    return x * lax.rsqrt(ms + EPS) * g


def _silu(x):
    return x / (1.0 + jnp.exp(-x))


def _const_spec(shape):
    return pl.BlockSpec(shape, lambda *_: (0,) * len(shape), pipeline_mode=pl.Buffered(1))


def _mod_spec(row_of_step):
    return pl.BlockSpec((None, 1, 6 * D_MODEL), lambda *idx: (row_of_step(*idx), 0, 0))


def _params(n_axes):
    return pltpu.CompilerParams(
        dimension_semantics=("arbitrary",) * n_axes, vmem_limit_bytes=VMEM_LIMIT)


def _mod_accumulate(cond_ref, w_ref, b_ref, o_ref, acc_ref, layer, step, n_steps):
    @pl.when(step == 0)
    def _():
        acc_ref[...] = jnp.broadcast_to(b_ref[layer:layer + 1, :], acc_ref.shape)

    a = _silu(cond_ref[...])
    a_hi = a.astype(BF16)
    a_lo = (a - a_hi.astype(F32)).astype(BF16)
    r = _dot(jnp.concatenate([a_hi, a_lo], axis=0), w_ref[...].astype(BF16))
    acc_ref[...] += r[:COND_ROWS] + r[COND_ROWS:]

    @pl.when(step == n_steps - 1)
    def _():
        for row in range(COND_ROWS):
            o_ref[row] = acc_ref[row:row + 1, :]


def _mod_specs(layer, n_steps, step_of):
    tk = D_MODEL // n_steps
    n_out = 6 * D_MODEL
    in_specs = [
        pl.BlockSpec((COND_ROWS, tk), lambda *idx: (0, step_of(*idx))),
        pl.BlockSpec((None, tk, n_out), lambda *idx: (layer, step_of(*idx), 0)),
        pl.BlockSpec((DEPTH, n_out), lambda *idx: (0, 0)),
    ]
    out_spec = pl.BlockSpec((COND_ROWS, 1, n_out), lambda *idx: (0, 0, 0))
    out_shape = jax.ShapeDtypeStruct((COND_ROWS, 1, n_out), F32)
    scratch = pltpu.VMEM((COND_ROWS, n_out), F32)
    return in_specs, out_spec, out_shape, scratch


def _mod_kernel(cond_ref, w_ref, b_ref, o_ref, acc_ref, *, layer):
    _mod_accumulate(cond_ref, w_ref, b_ref, o_ref, acc_ref, layer,
                    pl.program_id(0), pl.num_programs(0))


def _modulation(cond, ada_w, ada_b, layer):
    n_steps = MOD_K_STEPS
    in_specs, out_spec, out_shape, scratch = _mod_specs(layer, n_steps, lambda k: k)
    return pl.pallas_call(
        functools.partial(_mod_kernel, layer=layer),
        grid=(n_steps,),
        in_specs=in_specs,
        out_specs=out_spec,
        out_shape=out_shape,
        scratch_shapes=[scratch],
        compiler_params=_params(1),
        name="modulation",
    )(cond, ada_w, ada_b)


FFN_SUB_ROWS = 256


def _cast_specs(jobs, n_steps, step_of):
    in_specs, out_specs, out_shapes = [], [], []
    for src, layer in jobs:
        rows, cols = src.shape[1:]
        chunk = rows // n_steps
        assert chunk * n_steps == rows and chunk % BF16_SUBLANES == 0
        in_specs.append(pl.BlockSpec(
            (None, chunk, cols), lambda *idx, layer=layer: (layer, step_of(*idx), 0)))
        out_specs.append(pl.BlockSpec((chunk, cols), lambda *idx: (step_of(*idx), 0)))
        out_shapes.append(jax.ShapeDtypeStruct((rows, cols), BF16))
    return in_specs, out_specs, out_shapes


def _run_cast_jobs(src_refs, dst_refs):
    for src, dst in zip(src_refs, dst_refs):
        dst[...] = src[...].astype(BF16)


def _ffn_pair_kernel(xp_hbm, xs_hbm, mod_ref, g_ref, wg_ref, wu_ref, wd_ref, *rest,
                     n_p, tm, n_cast, mod_layer):
    n_mod = 0 if mod_layer is None else 1
    n_in = n_cast + 3 * n_mod
    op_hbm, os_hbm = rest[n_in:n_in + 2]
    n_out_end = n_in + 2 + n_cast + n_mod
    xbuf, obuf, sem_in, sem_out = rest[n_out_end:n_out_end + 4]
    i = pl.program_id(0)
    n = pl.num_programs(0)
    slot = i % 2
    _run_cast_jobs(rest[:n_cast], rest[n_in + 2:n_in + 2 + n_cast])
    if n_mod:
        @pl.when(i >= n_p)
        def _():
            _mod_accumulate(*rest[n_cast:n_in], rest[n_out_end - 1], rest[n_out_end + 4],
                            mod_layer, i - n_p, n - n_p)

    def tile_copy(step, s, inbound, act):
        def build(hbm, t):
            rows = pl.ds(pl.multiple_of(t * tm, tm), tm)
            if inbound:
                return pltpu.make_async_copy(hbm.at[rows, :], xbuf.at[s], sem_in.at[s])
            return pltpu.make_async_copy(obuf.at[s], hbm.at[rows, :], sem_out.at[s])

        @pl.when(step < n_p)
        def _():
            act(build(xp_hbm if inbound else op_hbm, step))

        @pl.when(step >= n_p)
        def _():
            act(build(xs_hbm if inbound else os_hbm, step - n_p))

    start = lambda copy: copy.start()
    wait = lambda copy: copy.wait()

    @pl.when(i == 0)
    def _():
        tile_copy(i, slot, True, start)

    @pl.when(i + 1 < n)
    def _():
        tile_copy(i + 1, 1 - slot, True, start)

    tile_copy(i, slot, True, wait)

    @pl.when(i >= 2)
    def _():
        tile_copy(i - 2, slot, False, wait)

    m = mod_ref[...]
    sh = m[:, 3 * D_MODEL:4 * D_MODEL]
    sc = m[:, 4 * D_MODEL:5 * D_MODEL]
    gt = m[:, 5 * D_MODEL:6 * D_MODEL]
    for r0 in range(0, tm, FFN_SUB_ROWS):
        rows = slice(r0, r0 + FFN_SUB_ROWS)
        x = xbuf[slot, rows, :]
        f = (_rms(x, g_ref[2:3, :]) * (1.0 + sc) + sh).astype(BF16)
        act = (_silu(_dot(f, wg_ref[...])) * _dot(f, wu_ref[...])).astype(BF16)
        y = _dot(act, wd_ref[...])
        obuf[slot, rows, :] = x + gt * _rms(y, g_ref[3:4, :])

    tile_copy(i, slot, False, start)

    @pl.when(i == n - 1)
    def _():
        tile_copy(i - 1, 1 - slot, False, wait)
        tile_copy(i, slot, False, wait)


def _ffn_pair(xp, xs, mod, g, wg, wu, wd, tm, prompt_jobs=(), latent_jobs=(), mod_job=None):
    n_p, n_s = xp.shape[0] // tm, xs.shape[0] // tm
    assert n_p >= 1 and n_s >= 1 and n_p + n_s >= 2
    any_spec = pl.BlockSpec(memory_space=pl.ANY)
    row_of_step = lambda i: jnp.where(i < n_p, 0, 1 + (i - n_p) // (DEC_SEQ // tm))
    prompt_step = lambda i: jnp.minimum(i, n_p - 1)
    latent_step = lambda i: jnp.maximum(i - n_p, 0)
    specs_p = _cast_specs(prompt_jobs, n_p, prompt_step)
    specs_s = _cast_specs(latent_jobs, n_s, latent_step)
    cast_in, cast_out, cast_shapes = (a + b for a, b in zip(specs_p, specs_s))
    jobs = list(prompt_jobs) + list(latent_jobs)
    mod_in, mod_out, mod_shapes, mod_scratch, mod_args, mod_layer = [], [], [], [], (), None
    if mod_job is not None:
        mod_layer = mod_job[3]
        mod_in, out_spec, out_shape, scratch = _mod_specs(mod_layer, n_s, latent_step)
        mod_out, mod_shapes, mod_scratch, mod_args = [out_spec], [out_shape], [scratch], mod_job[:3]
    outs = pl.pallas_call(
        functools.partial(_ffn_pair_kernel, n_p=n_p, tm=tm, n_cast=len(jobs), mod_layer=mod_layer),
        grid=(n_p + n_s,),
        in_specs=[any_spec, any_spec, _mod_spec(row_of_step), _const_spec((4, D_MODEL)),
                  _const_spec((D_MODEL, D_FF)), _const_spec((D_MODEL, D_FF)),
                  _const_spec((D_FF, D_MODEL))] + cast_in + mod_in,
        out_specs=[any_spec, any_spec] + cast_out + mod_out,
        out_shape=([jax.ShapeDtypeStruct(xp.shape, F32), jax.ShapeDtypeStruct(xs.shape, F32)]
                   + cast_shapes + mod_shapes),
        scratch_shapes=[pltpu.VMEM((2, tm, D_MODEL), F32), pltpu.VMEM((2, tm, D_MODEL), F32),
                        pltpu.SemaphoreType.DMA((2,)), pltpu.SemaphoreType.DMA((2,))] + mod_scratch,
        compiler_params=_params(1),
        name="ffn_pair",
    )(xp, xs, mod, g, wg, wu, wd, *[src for src, _ in jobs], *mod_args)
    n_cast = len(jobs)
    return outs[0], outs[1], outs[2:2 + n_cast], (outs[2 + n_cast] if mod_job is not None else None)


def _dft_cos_sin(n):
    j = np.arange(n)
    ang = 2.0 * np.pi * ((j[:, None] * j[None, :]) % n) / n
    return np.cos(ang), np.sin(ang)


PREMIX_CHUNK = 512


def _premix_channel_dft(x_ref, n_rows, sc, sh, g_ref, cs_ref, xc_scr, xs_scr):
    cs = cs_ref[...]
    for r0 in range(0, n_rows, PREMIX_CHUNK):
        rows = slice(r0, r0 + PREMIX_CHUNK)
        h = (_rms(x_ref[rows, :], g_ref[0:1, :]) * (1.0 + sc) + sh).astype(BF16)
        for j in range(N_FOURIER_GROUPS):
            lanes = slice(FOURIER_GROUP * j, FOURIER_GROUP * (j + 1))
            r = _dot(h[:, lanes], cs)
            xc_scr[rows, lanes] = r[:, :FOURIER_GROUP].astype(BF16)
            xs_scr[rows, lanes] = r[:, FOURIER_GROUP:].astype(BF16)


def _fourier_kernel(c_ref, s_ref, cs_ref, x_ref, mod_ref, g_ref, w_ref, *rest,
                    n_sub, seq, n_cast):
    o_ref = rest[n_cast]
    xc_scr, xs_scr, f_scr, w_scr = rest[2 * n_cast + 1:]
    _run_cast_jobs(rest[:n_cast], rest[n_cast + 1:2 * n_cast + 1])
    m = mod_ref[...]
    sh = m[:, 0:D_MODEL]
    sc = m[:, D_MODEL:2 * D_MODEL]
    gt = m[:, 2 * D_MODEL:3 * D_MODEL]

    @pl.when(pl.program_id(0) == 0)
    def _():
        w_scr[...] = w_ref[...].astype(BF16)

    _premix_channel_dft(x_ref, n_sub * seq, sc, sh, g_ref, cs_ref, xc_scr, xs_scr)
    for b in range(n_sub):
        rows = slice(seq * b, seq * (b + 1))
        f = _dot(c_ref[...], xc_scr[rows, :]) - _dot(s_ref[...], xs_scr[rows, :])
        f_scr[rows, :] = f.astype(BF16)
    mix = _dot(f_scr[...], w_scr[...])
    o_ref[...] = x_ref[...] + gt * _rms(mix, g_ref[1:2, :])


def _fourier_mixer(cmat, smat, cs, x, mod, row_of_group, g, w, n_seqs, seq, n_sub, cast_jobs=()):
    n_groups = n_seqs // n_sub
    rows = n_sub * seq
    tok = pl.BlockSpec((rows, D_MODEL), lambda b: (b, 0))
    cast_in, cast_out, cast_shapes = _cast_specs(cast_jobs, n_groups, lambda b: b)
    outs = pl.pallas_call(
        functools.partial(_fourier_kernel, n_sub=n_sub, seq=seq, n_cast=len(cast_jobs)),
        grid=(n_groups,),
        in_specs=[
            _const_spec((seq, seq)),
            _const_spec((seq, seq)),
            _const_spec((FOURIER_GROUP, 2 * FOURIER_GROUP)),
            tok,
            _mod_spec(row_of_group),
            _const_spec((4, D_MODEL)),
            _const_spec((D_MODEL, D_MODEL)),
        ] + cast_in,
        out_specs=[tok] + cast_out,
        out_shape=[jax.ShapeDtypeStruct((n_seqs * seq, D_MODEL), F32)] + cast_shapes,
        scratch_shapes=[pltpu.VMEM((rows, D_MODEL), BF16), pltpu.VMEM((rows, D_MODEL), BF16),
                        pltpu.VMEM((rows, D_MODEL), BF16),
                        pltpu.VMEM((D_MODEL, D_MODEL), BF16)],
        compiler_params=_params(1),
        name="fourier_mixer",
    )(cmat, smat, cs, x, mod, g, w, *[src for src, _ in cast_jobs])
    return outs[0], outs[1:]


HERM_TILE = 512
HERM_ROWS = HERM_TILE + BF16_SUBLANES


def _half_spectrum_blocks(mat):
    n_half = mat.shape[0] // 2 // HERM_TILE
    return np.stack([mat[HERM_TILE * t:HERM_TILE * t + HERM_ROWS] for t in range(n_half)])


def _reversal_matrix():
    rev = np.zeros((HERM_TILE, HERM_ROWS), np.float32)
    rev[np.arange(HERM_TILE), HERM_TILE - np.arange(HERM_TILE)] = 1.0
    return rev


def _fourier_long_kernel(ch_ref, sh_ref, rev_ref, cs_ref, x_ref, mod_ref, g_ref, w_ref, *rest,
                         seq, n_cast):
    o_ref = rest[n_cast]
    xc_scr, xs_scr, pq_scr, f_scr, w_scr = rest[2 * n_cast + 1:]
    _run_cast_jobs(rest[:n_cast], rest[n_cast + 1:2 * n_cast + 1])
    b, t, u = pl.program_id(0), pl.program_id(1), pl.program_id(2)
    m = mod_ref[...]
    sh = m[:, 0:D_MODEL]
    sc = m[:, D_MODEL:2 * D_MODEL]
    gt = m[:, 2 * D_MODEL:3 * D_MODEL]

    @pl.when((b == 0) & (t == 0) & (u == 0))
    def _():
        w_scr[...] = w_ref[...].astype(BF16)

    @pl.when((t == 0) & (u == 0))
    def _():
        _premix_channel_dft(x_ref, seq, sc, sh, g_ref, cs_ref, xc_scr, xs_scr)

    @pl.when(u == 0)
    def _():
        p = _dot(ch_ref[...], xc_scr[...])
        q = _dot(sh_ref[...], xs_scr[...])
        f_scr[...] = (p - q)[0:HERM_TILE].astype(BF16)
        pq_scr[...] = (p + q).astype(BF16)

    @pl.when(u == 1)
    def _():
        f_scr[...] = _dot(rev_ref[...], pq_scr[...]).astype(BF16)

    n_tiles = seq // HERM_TILE
    tile = t + u * (n_tiles - 1 - 2 * t)
    mix = _dot(f_scr[...], w_scr[...])
    x = x_ref[pl.ds(pl.multiple_of(tile * HERM_TILE, HERM_TILE), HERM_TILE), :]
    o_ref[...] = x + gt * _rms(mix, g_ref[1:2, :])


def _fourier_mixer_long(cos_h, sin_h, rev, cs, x, mod, row_of_seq, g, w, n_seqs, seq, cast_jobs=()):
    n_tiles = seq // HERM_TILE
    n_half = n_tiles // 2
    tile_of = lambda t, u: t + u * (n_tiles - 1 - 2 * t)
    half = pl.BlockSpec((None, HERM_ROWS, seq), lambda b, t, u: (t, 0, 0))
    cast_in, cast_out, cast_shapes = _cast_specs(
        cast_jobs, n_seqs * n_tiles, lambda b, t, u: (b * n_half + t) * 2 + u)
    outs = pl.pallas_call(
        functools.partial(_fourier_long_kernel, seq=seq, n_cast=len(cast_jobs)),
        grid=(n_seqs, n_half, 2),
        in_specs=[
            half, half,
            _const_spec((HERM_TILE, HERM_ROWS)),
            _const_spec((FOURIER_GROUP, 2 * FOURIER_GROUP)),
            pl.BlockSpec((seq, D_MODEL), lambda b, t, u: (b, 0)),
            _mod_spec(lambda b, t, u: row_of_seq(b)),
            _const_spec((4, D_MODEL)),
            _const_spec((D_MODEL, D_MODEL)),
        ] + cast_in,
        out_specs=[pl.BlockSpec((HERM_TILE, D_MODEL),
                                lambda b, t, u: (b * n_tiles + tile_of(t, u), 0))] + cast_out,
        out_shape=[jax.ShapeDtypeStruct((n_seqs * seq, D_MODEL), F32)] + cast_shapes,
        scratch_shapes=[pltpu.VMEM((seq, D_MODEL), BF16), pltpu.VMEM((seq, D_MODEL), BF16),
                        pltpu.VMEM((HERM_ROWS, D_MODEL), BF16),
                        pltpu.VMEM((HERM_TILE, D_MODEL), BF16),
                        pltpu.VMEM((D_MODEL, D_MODEL), BF16)],
        compiler_params=_params(3),
        name="fourier_mixer_long",
    )(cos_h, sin_h, rev, cs, x, mod, g, w, *[src for src, _ in cast_jobs])
    return outs[0], outs[1:]


def _split_pair(q, lo):
    zero = jnp.zeros_like(q)
    return jnp.concatenate([jnp.where(lo, q, zero), jnp.where(lo, zero, q)], axis=0)


def _prompt_attn_kernel(x_ref, mod_ref, g_ref, wq_ref, wk_ref, wv_ref, wout_ref,
                        o_ref, kt_ref, vt_ref, wkt_ref, wvt_ref, h_scr, q_scr, kt_scr, vt_scr,
                        att_scr, *, n_seq):
    @pl.when(pl.program_id(0) == 0)
    def _():
        wkt_ref[...] = wk_ref[...].T
        wvt_ref[...] = wv_ref[...].T

    x = x_ref[...]
    m = mod_ref[...]
    sh = m[:, 0:D_MODEL]
    sc = m[:, D_MODEL:2 * D_MODEL]
    gt = m[:, 2 * D_MODEL:3 * D_MODEL]
    h_scr[...] = (_rms(x, g_ref[0:1, :]) * (1.0 + sc) + sh).astype(BF16)
    q_scr[...] = (_dot(h_scr[...], wq_ref[...]) * Q_SCALE).astype(BF16)

    lo = lax.broadcasted_iota(jnp.int32, (SEQ, PAIR_W), 1) < HEAD_DIM
    ones = jnp.ones((PAIR_W, SEQ), BF16)
    for b in range(n_seq):
        rows = slice(SEQ * b, SEQ * (b + 1))
        kt = _dot_nt(wkt_ref[...], h_scr[rows, :])
        vt = _dot_nt(wvt_ref[...], h_scr[rows, :])
        kt_ref[b] = kt.reshape(N_HEADS, HEAD_DIM, SEQ)
        vt_ref[b] = vt.reshape(N_HEADS, HEAD_DIM, SEQ)
        kt_scr[b] = kt.astype(BF16)
        vt_scr[b] = vt.astype(BF16)
        for j in range(N_PAIRS):
            lanes = slice(PAIR_W * j, PAIR_W * (j + 1))
            qs = _split_pair(q_scr[rows, lanes], lo)
            s = _dot(qs, kt_scr[b, lanes, :])
            p = jnp.exp2((s - jnp.max(s, axis=-1, keepdims=True)).astype(BF16))
            o2 = _dot_nt(p, jnp.concatenate([vt_scr[b, lanes, :], ones], axis=0))
            o = o2[:, 0:PAIR_W] / o2[:, PAIR_W:2 * PAIR_W]
            att_scr[rows, lanes] = jnp.where(lo, o[:SEQ], o[SEQ:]).astype(BF16)

    mix = _dot(att_scr[...], wout_ref[...])
    o_ref[...] = x + gt * _rms(mix, g_ref[1:2, :])


def _prompt_attn(x, mod, row, g, wqkv, wout, n_seq=4):
    t = x.shape[0]
    tm = n_seq * SEQ
    tok = pl.BlockSpec((tm, D_MODEL), lambda i: (i, 0))
    out = jax.ShapeDtypeStruct((t, D_MODEL), F32)
    cache = pl.BlockSpec((n_seq, None, N_HEADS, HEAD_DIM, SEQ), lambda i: (i, 0, 0, 0, 0))
    cache_out = jax.ShapeDtypeStruct((t // SEQ, 1, N_HEADS, HEAD_DIM, SEQ), F32)

    def qkv_part(n):
        return pl.BlockSpec((D_MODEL, D_MODEL), lambda i: (0, n), pipeline_mode=pl.Buffered(1))

    return pl.pallas_call(
        functools.partial(_prompt_attn_kernel, n_seq=n_seq),
        grid=(t // tm,),
        in_specs=[tok, _mod_spec(lambda i: row), _const_spec((4, D_MODEL)),
                  qkv_part(0), qkv_part(1), qkv_part(2), _const_spec((D_MODEL, D_MODEL))],
        out_specs=[tok, cache, cache],
        out_shape=[out, cache_out, cache_out],
        scratch_shapes=[pltpu.VMEM((D_MODEL, D_MODEL), BF16), pltpu.VMEM((D_MODEL, D_MODEL), BF16),
                        pltpu.VMEM((tm, D_MODEL), BF16), pltpu.VMEM((tm, D_MODEL), BF16),
                        pltpu.VMEM((n_seq, D_MODEL, SEQ), BF16),
                        pltpu.VMEM((n_seq, D_MODEL, SEQ), BF16),
                        pltpu.VMEM((tm, D_MODEL), BF16)],
        compiler_params=_params(1),
        name="prompt_attn",
    )(x, mod, g, wqkv, wqkv, wqkv, wout)


def _premix_qkv_kernel(x_ref, mod_ref, g_ref, wqkv_ref, ckt_ref, cvt_ref, l_ref,
                       q_ref, k_ref, v_ref, ck_ref, cv_ref, bias_ref, *, heads_per_step):
    _build_bias_tiles(l_ref, pl.program_id(0) * heads_per_step, heads_per_step, bias_ref)

    ck_ref[...] = ckt_ref[...].astype(BF16)
    _store_values_with_ones(cv_ref, cvt_ref[...].T.astype(BF16))

    x = x_ref[...]
    m = mod_ref[...]
    sh = m[:, 0:D_MODEL]
    sc = m[:, D_MODEL:2 * D_MODEL]
    h = (_rms(x, g_ref[0:1, :]) * (1.0 + sc) + sh).astype(BF16)
    qkv = _dot(h, wqkv_ref[...])
    q_ref[...] = (qkv[:, 0:D_MODEL] * Q_SCALE).astype(BF16)
    k_ref[...] = qkv[:, D_MODEL:2 * D_MODEL].astype(BF16)
    _store_values_with_ones(v_ref, qkv[:, 2 * D_MODEL:3 * D_MODEL].astype(BF16))


def _store_values_with_ones(v_ref, v):
    ones = jnp.ones((v.shape[0], PAIR_W), BF16)
    for j in range(v.shape[1] // PAIR_W):
        v_ref[:, 2 * PAIR_W * j:2 * PAIR_W * j + PAIR_W] = v[:, PAIR_W * j:PAIR_W * (j + 1)]
        v_ref[:, 2 * PAIR_W * j + PAIR_W:2 * PAIR_W * (j + 1)] = ones


def _premix_qkv(x, mod, row_of_step, g, wqkv, cache_kt, cache_vt, layer_j, rpb, tm=512):
    t = x.shape[0]
    heads_per_step = N_HEADS // (t // tm)
    left = (GRID_W - 1) - (WIN_COLS - 1)
    right = PAIR_W - (2 * WIN_COLS - 1) - left
    rpb_rows = jnp.pad(rpb, ((0, 0), (0, 0), (left, right)), mode="edge")
    steps_per_seq = DEC_SEQ // tm
    chunk = D_MODEL // steps_per_seq
    tok = pl.BlockSpec((tm, D_MODEL), lambda i: (i, 0))
    tok2 = pl.BlockSpec((tm, 2 * D_MODEL), lambda i: (i, 0))
    cache = pl.BlockSpec((None, None, chunk, PAST_LEN),
                         lambda i: (i // steps_per_seq, layer_j, i % steps_per_seq, 0))
    out = jax.ShapeDtypeStruct((t, D_MODEL), BF16)
    out2 = jax.ShapeDtypeStruct((t, 2 * D_MODEL), BF16)
    return pl.pallas_call(
        functools.partial(_premix_qkv_kernel, heads_per_step=heads_per_step),
        grid=(t // tm,),
        in_specs=[tok, _mod_spec(row_of_step), _const_spec((4, D_MODEL)),
                  _const_spec((D_MODEL, 3 * D_MODEL)), cache, cache,
                  _const_spec((N_HEADS, N_DR, PAIR_W))],
        out_specs=[tok, tok, tok2,
                   pl.BlockSpec((chunk, PAST_LEN), lambda i: (i, 0)),
                   pl.BlockSpec((PAST_LEN, 2 * chunk),
                                lambda i: (i // steps_per_seq, i % steps_per_seq)),
                   pl.BlockSpec((heads_per_step * N_DR_PAIRS, GRID_W, PAIR_W),
                                lambda i: (i, 0, 0))],
        out_shape=[out, out, out2,
                   jax.ShapeDtypeStruct((DEC_BATCH * D_MODEL, PAST_LEN), BF16),
                   jax.ShapeDtypeStruct((DEC_BATCH * PAST_LEN, 2 * D_MODEL), BF16),
                   jax.ShapeDtypeStruct((N_HEADS * N_DR_PAIRS, GRID_W, PAIR_W), F32)],
        compiler_params=_params(1),
        name="premix_qkv",
    )(x, mod, g, wqkv, cache_kt, cache_vt, rpb_rows)


def _build_bias_tiles(l_ref, first_head, n_heads, bias_ref):
    lane = lax.broadcasted_iota(jnp.int32, (GRID_W, PAIR_W), 1)
    qcol = lax.broadcasted_iota(jnp.int32, (GRID_W, PAIR_W), 0)
    kcol = lane & (GRID_W - 1)
    start = jnp.clip(qcol - WIN_COLS // 2, 0, GRID_W - WIN_COLS)
    in_window = (kcol >= start) & (kcol < start + WIN_COLS)
    lo = lane < GRID_W

    for hh in range(n_heads):
        def toeplitz(d, shift):
            row = jnp.broadcast_to(l_ref[first_head + hh, d:d + 1, :], (GRID_W, PAIR_W))
            return pltpu.roll(row, shift, 1, stride=1, stride_axis=0)

        for d in range(N_DR_PAIRS):
            both = jnp.where(lo, toeplitz(d, GRID_W + 1), toeplitz(d + 1, 1))
            bias_ref[hh * N_DR_PAIRS + d] = jnp.where(in_window, both * LOG2E, NEG_INF)


def _kv_window_start(blk, rows_per_step):
    return jnp.clip(blk * rows_per_step - WIN_ROWS // 2, 0,
                    GRID_ROWS - (rows_per_step + WIN_ROWS))


def _na_attn_kernel(q_ref, k_ref, v_ref, ckt_ref, cv_ref, bias_ref, x_ref, mod_ref, g_ref,
                    wout_ref, o_ref, s_scr, m_scr, p_scr, att_scr, *, rows_per_step):
    blk = pl.program_id(1)
    lo = lax.broadcasted_iota(jnp.int32, (GRID_W, PAIR_W), 1) < HEAD_DIM

    win_start = _kv_window_start(blk, rows_per_step)

    def row_geometry(i):
        r = blk * rows_per_step + i
        rs = jnp.clip(r - WIN_ROWS // 2, 0, GRID_ROWS - WIN_ROWS)
        d0 = rs - r + (WIN_ROWS - 1)
        q0 = pl.multiple_of(i * GRID_W, GRID_W)
        k0 = pl.multiple_of((rs - win_start) * GRID_W, GRID_W)
        return d0, q0, k0

    def scores(i, slot):
        d0, q0, k0 = row_geometry(i)
        for j in range(N_PAIRS):
            lanes = slice(PAIR_W * j, PAIR_W * (j + 1))
            qs = _split_pair(q_ref[pl.ds(q0, GRID_W), lanes], lo)
            bias = jnp.concatenate(
                [jnp.concatenate(
                    [bias_ref[(2 * j) * N_DR_PAIRS + d0 + 2 * jj],
                     bias_ref[(2 * j + 1) * N_DR_PAIRS + d0 + 2 * jj]], axis=0)
                 for jj in range(WIN_ROWS // 2)], axis=1)
            s_loc = _dot_nt(qs, k_ref[pl.ds(k0, N_LOCAL), lanes]) + bias
            s_ctx = _dot(qs, ckt_ref[lanes, :])
            mx = jnp.maximum(jnp.max(s_loc, axis=-1, keepdims=True),
                             jnp.max(s_ctx, axis=-1, keepdims=True))
            s_scr[slot, j, :, 0:N_LOCAL] = s_loc
            s_scr[slot, j, :, N_LOCAL:N_KEYS] = s_ctx
            m_scr[slot, j] = jnp.broadcast_to(mx, (2 * GRID_W, PAIR_W))

    def probs(slot):
        for j in range(N_PAIRS):
            mx = m_scr[slot, j][:, 0:1]
            p_scr[slot, j] = jnp.exp2((s_scr[slot, j] - mx).astype(BF16))

    def values(i, slot):
        _, q0, k0 = row_geometry(i)
        for j in range(N_PAIRS):
            lanes2 = slice(2 * PAIR_W * j, 2 * PAIR_W * (j + 1))
            p = p_scr[slot, j]
            o2 = (_dot(p[:, 0:N_LOCAL], v_ref[pl.ds(k0, N_LOCAL), lanes2])
                  + _dot(p[:, N_LOCAL:N_KEYS], cv_ref[:, lanes2]))
            o = o2[:, 0:PAIR_W] / o2[:, PAIR_W:2 * PAIR_W]
            att_scr[pl.ds(q0, GRID_W), PAIR_W * j:PAIR_W * (j + 1)] = (
                jnp.where(lo, o[:GRID_W], o[GRID_W:]).astype(BF16))

    scores(0, 0)
    probs(0)
    scores(1, 1)

    def two_rows(t, carry):
        i = 2 * t
        scores(i, 0)
        probs(1)
        values(i - 2, 0)
        scores(i + 1, 1)
        values(i - 1, 1)
        probs(0)
        return carry

    lax.fori_loop(1, rows_per_step // 2, two_rows, 0)
    values(rows_per_step - 2, 0)
    probs(1)
    values(rows_per_step - 1, 1)

    gt = mod_ref[...][:, 2 * D_MODEL:3 * D_MODEL]
    mix = _dot(att_scr[...], wout_ref[...])
    o_ref[...] = x_ref[...] + gt * _rms(mix, g_ref[1:2, :])


def _na_attn(q, k, v, ck, cv, bias, x, mod, row_of_batch, g, wout, rows_per_step=8):
    tm = rows_per_step * GRID_W
    n_t = DEC_SEQ // tm
    tok = pl.BlockSpec((tm, D_MODEL), lambda b, t: (b * n_t + t, 0))

    def per_batch(rows, width):
        return pl.BlockSpec((rows, width), lambda b, t: (b, 0), pipeline_mode=pl.Buffered(1))

    def kv_window(width):
        def start(b, t):
            row = _kv_window_start(t, rows_per_step)
            return (pl.multiple_of(b * DEC_SEQ + row * GRID_W, GRID_W), 0)
        return pl.BlockSpec(
            (pl.Element((rows_per_step + WIN_ROWS) * GRID_W), pl.Element(width)), start)

    return pl.pallas_call(
        functools.partial(_na_attn_kernel, rows_per_step=rows_per_step),
        grid=(DEC_BATCH, n_t),
        in_specs=[
            tok,
            kv_window(D_MODEL), kv_window(2 * D_MODEL),
            per_batch(D_MODEL, PAST_LEN), per_batch(PAST_LEN, 2 * D_MODEL),
            _const_spec((N_HEADS * N_DR_PAIRS, GRID_W, PAIR_W)),
            tok,
            _mod_spec(lambda b, t: row_of_batch(b)),
            _const_spec((4, D_MODEL)),
            _const_spec((D_MODEL, D_MODEL)),
        ],
        out_specs=tok,
        out_shape=jax.ShapeDtypeStruct((DEC_BATCH * DEC_SEQ, D_MODEL), F32),
        scratch_shapes=[
            pltpu.VMEM((2, N_PAIRS, 2 * GRID_W, N_KEYS), F32),
            pltpu.VMEM((2, N_PAIRS, 2 * GRID_W, PAIR_W), F32),
            pltpu.VMEM((2, N_PAIRS, 2 * GRID_W, N_KEYS), BF16),
            pltpu.VMEM((tm, D_MODEL), BF16),
        ],
        compiler_params=_params(2),
        name="na_attn",
    )(q, k, v, ck, cv, bias, x, mod, g, wout)


def kernel(x_prompt, x_sample, c, cache_k, cache_v, c_ctx, ada_w, ada_b, norm_g, fourier_w_out,
           na_w_qkv, na_rpb, na_w_out, ffn_w_gate, ffn_w_up, ffn_w_down):
    n_p = BATCH * SEQ
    n_s = DEC_BATCH * DEC_SEQ
    xp = x_prompt.reshape(n_p, D_MODEL)
    xs = x_sample.reshape(n_s, D_MODEL)

    cond = jnp.concatenate(
        [c_ctx[None, :], c, jnp.zeros((COND_ROWS - 1 - DEC_BATCH, D_MODEL), F32)], axis=0)
    mods = {0: _modulation(cond, ada_w, ada_b, 0)}

    cos_g, sin_g = _dft_cos_sin(FOURIER_GROUP)
    cs_chan = jnp.asarray(np.concatenate([cos_g, sin_g], axis=1), F32).astype(BF16)
    cos_p, sin_p = (jnp.asarray(a, F32).astype(BF16) for a in _dft_cos_sin(SEQ))
    cos_s, sin_s = (jnp.asarray(_half_spectrum_blocks(a), F32).astype(BF16)
                    for a in _dft_cos_sin(DEC_SEQ))
    rev = jnp.asarray(_reversal_matrix(), F32).astype(BF16)

    tm, ffn_tm = TOKEN_TILE, FFN_TILE
    ffn_weights = (ffn_w_gate, ffn_w_up, ffn_w_down)
    cache_kt = jnp.transpose(cache_k, (0, 1, 3, 4, 2)).reshape(DEC_BATCH, -1, D_MODEL, PAST_LEN)
    cache_vt = jnp.transpose(cache_v, (0, 1, 3, 4, 2)).reshape(DEC_BATCH, -1, D_MODEL, PAST_LEN)
    new_kt = new_vt = None
    attn_weights = None
    for layer in range(DEPTH):
        g = norm_g[layer]
        if layer not in mods:
            mods[layer] = _modulation(cond, ada_w, ada_b, layer)
        mod = mods[layer]
        prompt_row = lambda *_: 0
        sample_row_of_batch = lambda b: 1 + b
        sample_row_of_tile = lambda i: 1 + i // (DEC_SEQ // tm)

        if layer % 2 == 0:
            w_out = fourier_w_out[layer // 2]
            jobs = [(w, layer) for w in ffn_weights] if layer == 0 else []
            xp, cast_p = _fourier_mixer(cos_p, sin_p, cs_chan, xp, mod, prompt_row, g, w_out,
                                        BATCH, SEQ, n_sub=4, cast_jobs=jobs[:2])
            xs, cast_s = _fourier_mixer_long(cos_s, sin_s, rev, cs_chan, xs, mod,
                                             sample_row_of_batch, g, w_out, DEC_BATCH, DEC_SEQ,
                                             cast_jobs=jobs[2:])
            if layer == 0:
                wg, wu, wd = list(cast_p) + list(cast_s)
        else:
            j = layer // 2
            if attn_weights is None:
                attn_weights = (na_w_qkv[j].astype(BF16), na_w_out[j].astype(BF16))
            w_qkv, w_out = attn_weights
            attn_weights = None
            xp, new_kt, new_vt = _prompt_attn(xp, mod, 0, g, w_qkv, w_out)
            q, k, v, ckt, cvt, bias = _premix_qkv(xs, mod, sample_row_of_tile, g, w_qkv,
                                                  cache_kt, cache_vt, j, na_rpb[j], tm)
            xs = _na_attn(q, k, v, ckt, cvt, bias, xs, mod, sample_row_of_batch, g, w_out)

        nxt = layer + 1
        ffn_jobs = [(w, nxt) for w in ffn_weights] if nxt < DEPTH else []
        attn_jobs = ([(na_w_qkv, nxt // 2), (na_w_out, nxt // 2)]
                     if nxt < DEPTH and nxt % 2 == 1 else [])
        mod_job = (cond, ada_w, ada_b, nxt) if nxt < DEPTH else None
        busy = bool(ffn_jobs) or bool(attn_jobs) or mod_job is not None
        xp, xs, casts, next_mod = _ffn_pair(
            xp, xs, mod, g, wg, wu, wd, tm if busy else ffn_tm,
            prompt_jobs=ffn_jobs, latent_jobs=attn_jobs, mod_job=mod_job)
        if ffn_jobs:
            wg, wu, wd = casts[:len(ffn_jobs)]
        if attn_jobs:
            attn_weights = tuple(casts[len(ffn_jobs):])
        if mod_job is not None:
            mods[nxt] = next_mod

    new_k = jnp.transpose(new_kt, (0, 1, 4, 2, 3))
    new_v = jnp.transpose(new_vt, (0, 1, 4, 2, 3))
    return (xp.reshape(BATCH, SEQ, D_MODEL), xs.reshape(DEC_BATCH, DEC_SEQ, D_MODEL), new_k, new_v)
```

```python
import functools

import numpy as np
import jax
import jax.numpy as jnp
from jax import lax
from jax.experimental import pallas as pl
from jax.experimental.pallas import tpu as pltpu

D_MODEL = 1024
BATCH = 32
SEQ = 256
DEPTH = 2
DEC_BATCH = 2
DEC_SEQ = 2048
PAST_LEN = 512
GRID_W = 64
GRID_ROWS = DEC_SEQ // GRID_W
N_HEADS = 16
HEAD_DIM = D_MODEL // N_HEADS
N_FOURIER_GROUPS = 4
FOURIER_GROUP = D_MODEL // N_FOURIER_GROUPS
WIN_ROWS = 8
WIN_COLS = 16
D_FF = 2816
EPS = 1e-6
NEG_INF = -1e30

N_PAIRS = N_HEADS // 2
PAIR_W = 2 * HEAD_DIM
N_DR = 2 * WIN_ROWS - 1
N_DR_PAIRS = N_DR - 1
N_LOCAL = WIN_ROWS * GRID_W
N_KEYS = N_LOCAL + PAST_LEN
COND_ROWS = 8
LOG2E = 1.4426950408889634
Q_SCALE = HEAD_DIM ** -0.5 * LOG2E

VMEM_LIMIT = 56 * 1024 * 1024
BF16_SUBLANES = 16
TOKEN_TILE = 512
FFN_TILE = 1024
MOD_K_STEPS = 4

F32 = jnp.float32
BF16 = jnp.bfloat16


def _dot(a, b):
    return jnp.dot(a, b, preferred_element_type=F32)


def _dot_nt(a, b):
    return lax.dot_general(a, b, (((1,), (1,)), ((), ())), preferred_element_type=F32)


def _rms(x, g):
    ms = jnp.mean(x * x, axis=-1, keepdims=True)
    return x * lax.rsqrt(ms + EPS) * g


def _silu(x):
    return x / (1.0 + jnp.exp(-x))


def _const_spec(shape):
    return pl.BlockSpec(shape, lambda *_: (0,) * len(shape), pipeline_mode=pl.Buffered(1))


def _mod_spec(row_of_step):
    return pl.BlockSpec((None, 1, 6 * D_MODEL), lambda *idx: (row_of_step(*idx), 0, 0))


def _params(n_axes):
    return pltpu.CompilerParams(
        dimension_semantics=("arbitrary",) * n_axes, vmem_limit_bytes=VMEM_LIMIT)


def _mod_accumulate(cond_ref, w_ref, b_ref, o_ref, acc_ref, layer, step, n_steps):
    @pl.when(step == 0)
    def _():
        acc_ref[...] = jnp.broadcast_to(b_ref[layer:layer + 1, :], acc_ref.shape)

    a = _silu(cond_ref[...])
    a_hi = a.astype(BF16)
    a_lo = (a - a_hi.astype(F32)).astype(BF16)
    r = _dot(jnp.concatenate([a_hi, a_lo], axis=0), w_ref[...].astype(BF16))
    acc_ref[...] += r[:COND_ROWS] + r[COND_ROWS:]

    @pl.when(step == n_steps - 1)
    def _():
        for row in range(COND_ROWS):
            o_ref[row] = acc_ref[row:row + 1, :]


def _mod_specs(layer, n_steps, step_of):
    tk = D_MODEL // n_steps
    n_out = 6 * D_MODEL
    in_specs = [
        pl.BlockSpec((COND_ROWS, tk), lambda *idx: (0, step_of(*idx))),
        pl.BlockSpec((None, tk, n_out), lambda *idx: (layer, step_of(*idx), 0)),
        pl.BlockSpec((DEPTH, n_out), lambda *idx: (0, 0)),
    ]
    out_spec = pl.BlockSpec((COND_ROWS, 1, n_out), lambda *idx: (0, 0, 0))
    out_shape = jax.ShapeDtypeStruct((COND_ROWS, 1, n_out), F32)
    scratch = pltpu.VMEM((COND_ROWS, n_out), F32)
    return in_specs, out_spec, out_shape, scratch


def _mod_kernel(cond_ref, w_ref, b_ref, o_ref, acc_ref, *, layer):
    _mod_accumulate(cond_ref, w_ref, b_ref, o_ref, acc_ref, layer,
                    pl.program_id(0), pl.num_programs(0))


def _modulation(cond, ada_w, ada_b, layer):
    n_steps = MOD_K_STEPS
    in_specs, out_spec, out_shape, scratch = _mod_specs(layer, n_steps, lambda k: k)
    return pl.pallas_call(
        functools.partial(_mod_kernel, layer=layer),
        grid=(n_steps,),
        in_specs=in_specs,
        out_specs=out_spec,
        out_shape=out_shape,
        scratch_shapes=[scratch],
        compiler_params=_params(1),
        name="modulation",
    )(cond, ada_w, ada_b)


FFN_SUB_ROWS = 256


def _cast_specs(jobs, n_steps, step_of):
    in_specs, out_specs, out_shapes = [], [], []
    for src, layer in jobs:
        rows, cols = src.shape[1:]
        chunk = rows // n_steps
        assert chunk * n_steps == rows and chunk % BF16_SUBLANES == 0
        in_specs.append(pl.BlockSpec(
            (None, chunk, cols), lambda *idx, layer=layer: (layer, step_of(*idx), 0)))
        out_specs.append(pl.BlockSpec((chunk, cols), lambda *idx: (step_of(*idx), 0)))
        out_shapes.append(jax.ShapeDtypeStruct((rows, cols), BF16))
    return in_specs, out_specs, out_shapes


def _run_cast_jobs(src_refs, dst_refs):
    for src, dst in zip(src_refs, dst_refs):
        dst[...] = src[...].astype(BF16)


def _ffn_pair_kernel(xp_hbm, xs_hbm, mod_ref, g_ref, wg_ref, wu_ref, wd_ref, *rest,
                     n_p, tm, n_cast, mod_layer):
    n_mod = 0 if mod_layer is None else 1
    n_in = n_cast + 3 * n_mod
    op_hbm, os_hbm = rest[n_in:n_in + 2]
    n_out_end = n_in + 2 + n_cast + n_mod
    xbuf, obuf, sem_in, sem_out = rest[n_out_end:n_out_end + 4]
    i = pl.program_id(0)
    n = pl.num_programs(0)
    slot = i % 2
    _run_cast_jobs(rest[:n_cast], rest[n_in + 2:n_in + 2 + n_cast])
    if n_mod:
        @pl.when(i >= n_p)
        def _():
            _mod_accumulate(*rest[n_cast:n_in], rest[n_out_end - 1], rest[n_out_end + 4],
                            mod_layer, i - n_p, n - n_p)

    def tile_copy(step, s, inbound, act):
        def build(hbm, t):
            rows = pl.ds(pl.multiple_of(t * tm, tm), tm)
            if inbound:
                return pltpu.make_async_copy(hbm.at[rows, :], xbuf.at[s], sem_in.at[s])
            return pltpu.make_async_copy(obuf.at[s], hbm.at[rows, :], sem_out.at[s])

        @pl.when(step < n_p)
        def _():
            act(build(xp_hbm if inbound else op_hbm, step))

        @pl.when(step >= n_p)
        def _():
            act(build(xs_hbm if inbound else os_hbm, step - n_p))

    start = lambda copy: copy.start()
    wait = lambda copy: copy.wait()

    @pl.when(i == 0)
    def _():
        tile_copy(i, slot, True, start)

    @pl.when(i + 1 < n)
    def _():
        tile_copy(i + 1, 1 - slot, True, start)

    tile_copy(i, slot, True, wait)

    @pl.when(i >= 2)
    def _():
        tile_copy(i - 2, slot, False, wait)

    m = mod_ref[...]
    sh = m[:, 3 * D_MODEL:4 * D_MODEL]
    sc = m[:, 4 * D_MODEL:5 * D_MODEL]
    gt = m[:, 5 * D_MODEL:6 * D_MODEL]
    for r0 in range(0, tm, FFN_SUB_ROWS):
        rows = slice(r0, r0 + FFN_SUB_ROWS)
        x = xbuf[slot, rows, :]
        f = (_rms(x, g_ref[2:3, :]) * (1.0 + sc) + sh).astype(BF16)
        act = (_silu(_dot(f, wg_ref[...])) * _dot(f, wu_ref[...])).astype(BF16)
        y = _dot(act, wd_ref[...])
        obuf[slot, rows, :] = x + gt * _rms(y, g_ref[3:4, :])

    tile_copy(i, slot, False, start)

    @pl.when(i == n - 1)
    def _():
        tile_copy(i - 1, 1 - slot, False, wait)
        tile_copy(i, slot, False, wait)


def _ffn_pair(xp, xs, mod, g, wg, wu, wd, tm, prompt_jobs=(), latent_jobs=(), mod_job=None):
    n_p, n_s = xp.shape[0] // tm, xs.shape[0] // tm
    assert n_p >= 1 and n_s >= 1 and n_p + n_s >= 2
    any_spec = pl.BlockSpec(memory_space=pl.ANY)
    row_of_step = lambda i: jnp.where(i < n_p, 0, 1 + (i - n_p) // (DEC_SEQ // tm))
    prompt_step = lambda i: jnp.minimum(i, n_p - 1)
    latent_step = lambda i: jnp.maximum(i - n_p, 0)
    specs_p = _cast_specs(prompt_jobs, n_p, prompt_step)
    specs_s = _cast_specs(latent_jobs, n_s, latent_step)
    cast_in, cast_out, cast_shapes = (a + b for a, b in zip(specs_p, specs_s))
    jobs = list(prompt_jobs) + list(latent_jobs)
    mod_in, mod_out, mod_shapes, mod_scratch, mod_args, mod_layer = [], [], [], [], (), None
    if mod_job is not None:
        mod_layer = mod_job[3]
        mod_in, out_spec, out_shape, scratch = _mod_specs(mod_layer, n_s, latent_step)
        mod_out, mod_shapes, mod_scratch, mod_args = [out_spec], [out_shape], [scratch], mod_job[:3]
    outs = pl.pallas_call(
        functools.partial(_ffn_pair_kernel, n_p=n_p, tm=tm, n_cast=len(jobs), mod_layer=mod_layer),
        grid=(n_p + n_s,),
        in_specs=[any_spec, any_spec, _mod_spec(row_of_step), _const_spec((4, D_MODEL)),
                  _const_spec((D_MODEL, D_FF)), _const_spec((D_MODEL, D_FF)),
                  _const_spec((D_FF, D_MODEL))] + cast_in + mod_in,
        out_specs=[any_spec, any_spec] + cast_out + mod_out,
        out_shape=([jax.ShapeDtypeStruct(xp.shape, F32), jax.ShapeDtypeStruct(xs.shape, F32)]
                   + cast_shapes + mod_shapes),
        scratch_shapes=[pltpu.VMEM((2, tm, D_MODEL), F32), pltpu.VMEM((2, tm, D_MODEL), F32),
                        pltpu.SemaphoreType.DMA((2,)), pltpu.SemaphoreType.DMA((2,))] + mod_scratch,
        compiler_params=_params(1),
        name="ffn_pair",
    )(xp, xs, mod, g, wg, wu, wd, *[src for src, _ in jobs], *mod_args)
    n_cast = len(jobs)
    return outs[0], outs[1], outs[2:2 + n_cast], (outs[2 + n_cast] if mod_job is not None else None)


def _dft_cos_sin(n):
    j = np.arange(n)
    ang = 2.0 * np.pi * ((j[:, None] * j[None, :]) % n) / n
    return np.cos(ang), np.sin(ang)


PREMIX_CHUNK = 512


def _premix_channel_dft(x_ref, n_rows, sc, sh, g_ref, cs_ref, xc_scr, xs_scr):
    cs = cs_ref[...]
    for r0 in range(0, n_rows, PREMIX_CHUNK):
        rows = slice(r0, r0 + PREMIX_CHUNK)
        h = (_rms(x_ref[rows, :], g_ref[0:1, :]) * (1.0 + sc) + sh).astype(BF16)
        for j in range(N_FOURIER_GROUPS):
            lanes = slice(FOURIER_GROUP * j, FOURIER_GROUP * (j + 1))
            r = _dot(h[:, lanes], cs)
            xc_scr[rows, lanes] = r[:, :FOURIER_GROUP].astype(BF16)
            xs_scr[rows, lanes] = r[:, FOURIER_GROUP:].astype(BF16)


def _fourier_kernel(c_ref, s_ref, cs_ref, x_ref, mod_ref, g_ref, w_ref, *rest,
                    n_sub, seq, n_cast):
    o_ref = rest[n_cast]
    xc_scr, xs_scr, w_scr = rest[2 * n_cast + 1:]
    _run_cast_jobs(rest[:n_cast], rest[n_cast + 1:2 * n_cast + 1])
    m = mod_ref[...]
    sh = m[:, 0:D_MODEL]
    sc = m[:, D_MODEL:2 * D_MODEL]
    gt = m[:, 2 * D_MODEL:3 * D_MODEL]

    @pl.when(pl.program_id(0) == 0)
    def _():
        w_scr[...] = w_ref[...].astype(BF16)

    _premix_channel_dft(x_ref, n_sub * seq, sc, sh, g_ref, cs_ref, xc_scr, xs_scr)
    for b in range(n_sub):
        rows = slice(seq * b, seq * (b + 1))
        f = _dot(c_ref[...], xc_scr[rows, :]) - _dot(s_ref[...], xs_scr[rows, :])
        mix = _dot(f.astype(BF16), w_scr[...])
        o_ref[rows, :] = x_ref[rows, :] + gt * _rms(mix, g_ref[1:2, :])


def _fourier_mixer(cmat, smat, cs, x, mod, row_of_group, g, w, n_seqs, seq, n_sub, cast_jobs=()):
    n_groups = n_seqs // n_sub
    rows = n_sub * seq
    tok = pl.BlockSpec((rows, D_MODEL), lambda b: (b, 0))
    cast_in, cast_out, cast_shapes = _cast_specs(cast_jobs, n_groups, lambda b: b)
    outs = pl.pallas_call(
        functools.partial(_fourier_kernel, n_sub=n_sub, seq=seq, n_cast=len(cast_jobs)),
        grid=(n_groups,),
        in_specs=[
            _const_spec((seq, seq)),
            _const_spec((seq, seq)),
            _const_spec((FOURIER_GROUP, 2 * FOURIER_GROUP)),
            tok,
            _mod_spec(row_of_group),
            _const_spec((4, D_MODEL)),
            _const_spec((D_MODEL, D_MODEL)),
        ] + cast_in,
        out_specs=[tok] + cast_out,
        out_shape=[jax.ShapeDtypeStruct((n_seqs * seq, D_MODEL), F32)] + cast_shapes,
        scratch_shapes=[pltpu.VMEM((rows, D_MODEL), BF16), pltpu.VMEM((rows, D_MODEL), BF16),
                        pltpu.VMEM((D_MODEL, D_MODEL), BF16)],
        compiler_params=_params(1),
        name="fourier_mixer",
    )(cmat, smat, cs, x, mod, g, w, *[src for src, _ in cast_jobs])
    return outs[0], outs[1:]


HERM_TILE = 512
HERM_ROWS = HERM_TILE + BF16_SUBLANES


def _half_spectrum_blocks(mat):
    n_half = mat.shape[0] // 2 // HERM_TILE
    return np.stack([mat[HERM_TILE * t:HERM_TILE * t + HERM_ROWS] for t in range(n_half)])


def _reversal_matrix():
    rev = np.zeros((HERM_TILE, HERM_ROWS), np.float32)
    rev[np.arange(HERM_TILE), HERM_TILE - np.arange(HERM_TILE)] = 1.0
    return rev


def _fourier_long_kernel(ch_ref, sh_ref, rev_ref, cs_ref, x_ref, mod_ref, g_ref, w_ref, *rest,
                         seq, n_cast):
    o_ref = rest[n_cast]
    xc_scr, xs_scr, pq_scr, f_scr, w_scr = rest[2 * n_cast + 1:]
    _run_cast_jobs(rest[:n_cast], rest[n_cast + 1:2 * n_cast + 1])
    b, t, u = pl.program_id(0), pl.program_id(1), pl.program_id(2)
    m = mod_ref[...]
    sh = m[:, 0:D_MODEL]
    sc = m[:, D_MODEL:2 * D_MODEL]
    gt = m[:, 2 * D_MODEL:3 * D_MODEL]

    @pl.when((b == 0) & (t == 0) & (u == 0))
    def _():
        w_scr[...] = w_ref[...].astype(BF16)

    @pl.when((t == 0) & (u == 0))
    def _():
        _premix_channel_dft(x_ref, seq, sc, sh, g_ref, cs_ref, xc_scr, xs_scr)

    @pl.when(u == 0)
    def _():
        p = _dot(ch_ref[...], xc_scr[...])
        q = _dot(sh_ref[...], xs_scr[...])
        f_scr[...] = (p - q)[0:HERM_TILE].astype(BF16)
        pq_scr[...] = (p + q).astype(BF16)

    @pl.when(u == 1)
    def _():
        f_scr[...] = _dot(rev_ref[...], pq_scr[...]).astype(BF16)

    n_tiles = seq // HERM_TILE
    tile = t + u * (n_tiles - 1 - 2 * t)
    mix = _dot(f_scr[...], w_scr[...])
    x = x_ref[pl.ds(pl.multiple_of(tile * HERM_TILE, HERM_TILE), HERM_TILE), :]
    o_ref[...] = x + gt * _rms(mix, g_ref[1:2, :])


def _fourier_mixer_long(cos_h, sin_h, rev, cs, x, mod, row_of_seq, g, w, n_seqs, seq, cast_jobs=()):
    n_tiles = seq // HERM_TILE
    n_half = n_tiles // 2
    tile_of = lambda t, u: t + u * (n_tiles - 1 - 2 * t)
    half = pl.BlockSpec((None, HERM_ROWS, seq), lambda b, t, u: (t, 0, 0))
    cast_in, cast_out, cast_shapes = _cast_specs(
        cast_jobs, n_seqs * n_tiles, lambda b, t, u: (b * n_half + t) * 2 + u)
    outs = pl.pallas_call(
        functools.partial(_fourier_long_kernel, seq=seq, n_cast=len(cast_jobs)),
        grid=(n_seqs, n_half, 2),
        in_specs=[
            half, half,
            _const_spec((HERM_TILE, HERM_ROWS)),
            _const_spec((FOURIER_GROUP, 2 * FOURIER_GROUP)),
            pl.BlockSpec((seq, D_MODEL), lambda b, t, u: (b, 0)),
            _mod_spec(lambda b, t, u: row_of_seq(b)),
            _const_spec((4, D_MODEL)),
            _const_spec((D_MODEL, D_MODEL)),
        ] + cast_in,
        out_specs=[pl.BlockSpec((HERM_TILE, D_MODEL),
                                lambda b, t, u: (b * n_tiles + tile_of(t, u), 0))] + cast_out,
        out_shape=[jax.ShapeDtypeStruct((n_seqs * seq, D_MODEL), F32)] + cast_shapes,
        scratch_shapes=[pltpu.VMEM((seq, D_MODEL), BF16), pltpu.VMEM((seq, D_MODEL), BF16),
                        pltpu.VMEM((HERM_ROWS, D_MODEL), BF16),
                        pltpu.VMEM((HERM_TILE, D_MODEL), BF16),
                        pltpu.VMEM((D_MODEL, D_MODEL), BF16)],
        compiler_params=_params(3),
        name="fourier_mixer_long",
    )(cos_h, sin_h, rev, cs, x, mod, g, w, *[src for src, _ in cast_jobs])
    return outs[0], outs[1:]


def _split_pair(q, lo):
    zero = jnp.zeros_like(q)
    return jnp.concatenate([jnp.where(lo, q, zero), jnp.where(lo, zero, q)], axis=0)


def _prompt_attn_kernel(x_ref, mod_ref, g_ref, wq_ref, wk_ref, wv_ref, wout_ref,
                        o_ref, kt_ref, vt_ref, wkt_ref, wvt_ref, h_scr, q_scr, kt_scr, vt_scr,
                        att_scr, *, n_seq):
    @pl.when(pl.program_id(0) == 0)
    def _():
        wkt_ref[...] = wk_ref[...].T
        wvt_ref[...] = wv_ref[...].T

    x = x_ref[...]
    m = mod_ref[...]
    sh = m[:, 0:D_MODEL]
    sc = m[:, D_MODEL:2 * D_MODEL]
    gt = m[:, 2 * D_MODEL:3 * D_MODEL]
    h_scr[...] = (_rms(x, g_ref[0:1, :]) * (1.0 + sc) + sh).astype(BF16)
    q_scr[...] = (_dot(h_scr[...], wq_ref[...]) * Q_SCALE).astype(BF16)

    lo = lax.broadcasted_iota(jnp.int32, (SEQ, PAIR_W), 1) < HEAD_DIM
    ones = jnp.ones((PAIR_W, SEQ), BF16)
    for b in range(n_seq):
        rows = slice(SEQ * b, SEQ * (b + 1))
        kt = _dot_nt(wkt_ref[...], h_scr[rows, :])
        vt = _dot_nt(wvt_ref[...], h_scr[rows, :])
        kt_ref[b] = kt.reshape(N_HEADS, HEAD_DIM, SEQ)
        vt_ref[b] = vt.reshape(N_HEADS, HEAD_DIM, SEQ)
        kt_scr[b] = kt.astype(BF16)
        vt_scr[b] = vt.astype(BF16)
        for j in range(N_PAIRS):
            lanes = slice(PAIR_W * j, PAIR_W * (j + 1))
            qs = _split_pair(q_scr[rows, lanes], lo)
            s = _dot(qs, kt_scr[b, lanes, :])
            p = jnp.exp2((s - jnp.max(s, axis=-1, keepdims=True)).astype(BF16))
            o2 = _dot_nt(p, jnp.concatenate([vt_scr[b, lanes, :], ones], axis=0))
            o = o2[:, 0:PAIR_W] / o2[:, PAIR_W:2 * PAIR_W]
            att_scr[rows, lanes] = jnp.where(lo, o[:SEQ], o[SEQ:]).astype(BF16)

    mix = _dot(att_scr[...], wout_ref[...])
    o_ref[...] = x + gt * _rms(mix, g_ref[1:2, :])


def _prompt_attn(x, mod, row, g, wqkv, wout, n_seq=4):
    t = x.shape[0]
    tm = n_seq * SEQ
    tok = pl.BlockSpec((tm, D_MODEL), lambda i: (i, 0))
    out = jax.ShapeDtypeStruct((t, D_MODEL), F32)
    cache = pl.BlockSpec((n_seq, None, N_HEADS, HEAD_DIM, SEQ), lambda i: (i, 0, 0, 0, 0))
    cache_out = jax.ShapeDtypeStruct((t // SEQ, 1, N_HEADS, HEAD_DIM, SEQ), F32)

    def qkv_part(n):
        return pl.BlockSpec((D_MODEL, D_MODEL), lambda i: (0, n), pipeline_mode=pl.Buffered(1))

    return pl.pallas_call(
        functools.partial(_prompt_attn_kernel, n_seq=n_seq),
        grid=(t // tm,),
        in_specs=[tok, _mod_spec(lambda i: row), _const_spec((4, D_MODEL)),
                  qkv_part(0), qkv_part(1), qkv_part(2), _const_spec((D_MODEL, D_MODEL))],
        out_specs=[tok, cache, cache],
        out_shape=[out, cache_out, cache_out],
        scratch_shapes=[pltpu.VMEM((D_MODEL, D_MODEL), BF16), pltpu.VMEM((D_MODEL, D_MODEL), BF16),
                        pltpu.VMEM((tm, D_MODEL), BF16), pltpu.VMEM((tm, D_MODEL), BF16),
                        pltpu.VMEM((n_seq, D_MODEL, SEQ), BF16),
                        pltpu.VMEM((n_seq, D_MODEL, SEQ), BF16),
                        pltpu.VMEM((tm, D_MODEL), BF16)],
        compiler_params=_params(1),
        name="prompt_attn",
    )(x, mod, g, wqkv, wqkv, wqkv, wout)


def _premix_qkv_kernel(x_ref, mod_ref, g_ref, wqkv_ref, ckt_ref, cvt_ref, l_ref,
                       q_ref, k_ref, v_ref, ck_ref, cv_ref, bias_ref, *, heads_per_step):
    _build_bias_tiles(l_ref, pl.program_id(0) * heads_per_step, heads_per_step, bias_ref)

    ck_ref[...] = ckt_ref[...].astype(BF16)
    _store_values_with_ones(cv_ref, cvt_ref[...].T.astype(BF16))

    x = x_ref[...]
    m = mod_ref[...]
    sh = m[:, 0:D_MODEL]
    sc = m[:, D_MODEL:2 * D_MODEL]
    h = (_rms(x, g_ref[0:1, :]) * (1.0 + sc) + sh).astype(BF16)
    qkv = _dot(h, wqkv_ref[...])
    q_ref[...] = (qkv[:, 0:D_MODEL] * Q_SCALE).astype(BF16)
    k_ref[...] = qkv[:, D_MODEL:2 * D_MODEL].astype(BF16)
    _store_values_with_ones(v_ref, qkv[:, 2 * D_MODEL:3 * D_MODEL].astype(BF16))


def _store_values_with_ones(v_ref, v):
    ones = jnp.ones((v.shape[0], PAIR_W), BF16)
    for j in range(v.shape[1] // PAIR_W):
        v_ref[:, 2 * PAIR_W * j:2 * PAIR_W * j + PAIR_W] = v[:, PAIR_W * j:PAIR_W * (j + 1)]
        v_ref[:, 2 * PAIR_W * j + PAIR_W:2 * PAIR_W * (j + 1)] = ones


def _premix_qkv(x, mod, row_of_step, g, wqkv, cache_kt, cache_vt, layer_j, rpb, tm=512):
    t = x.shape[0]
    heads_per_step = N_HEADS // (t // tm)
    left = (GRID_W - 1) - (WIN_COLS - 1)
    right = PAIR_W - (2 * WIN_COLS - 1) - left
    rpb_rows = jnp.pad(rpb, ((0, 0), (0, 0), (left, right)), mode="edge")
    steps_per_seq = DEC_SEQ // tm
    chunk = D_MODEL // steps_per_seq
    tok = pl.BlockSpec((tm, D_MODEL), lambda i: (i, 0))
    tok2 = pl.BlockSpec((tm, 2 * D_MODEL), lambda i: (i, 0))
    cache = pl.BlockSpec((None, None, chunk, PAST_LEN),
                         lambda i: (i // steps_per_seq, layer_j, i % steps_per_seq, 0))
    out = jax.ShapeDtypeStruct((t, D_MODEL), BF16)
    out2 = jax.ShapeDtypeStruct((t, 2 * D_MODEL), BF16)
    return pl.pallas_call(
        functools.partial(_premix_qkv_kernel, heads_per_step=heads_per_step),
        grid=(t // tm,),
        in_specs=[tok, _mod_spec(row_of_step), _const_spec((4, D_MODEL)),
                  _const_spec((D_MODEL, 3 * D_MODEL)), cache, cache,
                  _const_spec((N_HEADS, N_DR, PAIR_W))],
        out_specs=[tok, tok, tok2,
                   pl.BlockSpec((chunk, PAST_LEN), lambda i: (i, 0)),
                   pl.BlockSpec((PAST_LEN, 2 * chunk),
                                lambda i: (i // steps_per_seq, i % steps_per_seq)),
                   pl.BlockSpec((heads_per_step * N_DR_PAIRS, GRID_W, PAIR_W),
                                lambda i: (i, 0, 0))],
        out_shape=[out, out, out2,
                   jax.ShapeDtypeStruct((DEC_BATCH * D_MODEL, PAST_LEN), BF16),
                   jax.ShapeDtypeStruct((DEC_BATCH * PAST_LEN, 2 * D_MODEL), BF16),
                   jax.ShapeDtypeStruct((N_HEADS * N_DR_PAIRS, GRID_W, PAIR_W), F32)],
        compiler_params=_params(1),
        name="premix_qkv",
    )(x, mod, g, wqkv, cache_kt, cache_vt, rpb_rows)


def _build_bias_tiles(l_ref, first_head, n_heads, bias_ref):
    lane = lax.broadcasted_iota(jnp.int32, (GRID_W, PAIR_W), 1)
    qcol = lax.broadcasted_iota(jnp.int32, (GRID_W, PAIR_W), 0)
    kcol = lane & (GRID_W - 1)
    start = jnp.clip(qcol - WIN_COLS // 2, 0, GRID_W - WIN_COLS)
    in_window = (kcol >= start) & (kcol < start + WIN_COLS)
    lo = lane < GRID_W

    for hh in range(n_heads):
        def toeplitz(d, shift):
            row = jnp.broadcast_to(l_ref[first_head + hh, d:d + 1, :], (GRID_W, PAIR_W))
            return pltpu.roll(row, shift, 1, stride=1, stride_axis=0)

        for d in range(N_DR_PAIRS):
            both = jnp.where(lo, toeplitz(d, GRID_W + 1), toeplitz(d + 1, 1))
            bias_ref[hh * N_DR_PAIRS + d] = jnp.where(in_window, both * LOG2E, NEG_INF)


def _kv_window_start(blk, rows_per_step):
    return jnp.clip(blk * rows_per_step - WIN_ROWS // 2, 0,
                    GRID_ROWS - (rows_per_step + WIN_ROWS))


def _na_attn_kernel(q_ref, k_ref, v_ref, ckt_ref, cv_ref, bias_ref, x_ref, mod_ref, g_ref,
                    wout_ref, o_ref, s_scr, m_scr, p_scr, att_scr, *, rows_per_step):
    blk = pl.program_id(1)
    lo = lax.broadcasted_iota(jnp.int32, (GRID_W, PAIR_W), 1) < HEAD_DIM

    win_start = _kv_window_start(blk, rows_per_step)

    def row_geometry(i):
        r = blk * rows_per_step + i
        rs = jnp.clip(r - WIN_ROWS // 2, 0, GRID_ROWS - WIN_ROWS)
        d0 = rs - r + (WIN_ROWS - 1)
        q0 = pl.multiple_of(i * GRID_W, GRID_W)
        k0 = pl.multiple_of((rs - win_start) * GRID_W, GRID_W)
        return d0, q0, k0

    def scores(i, slot):
        d0, q0, k0 = row_geometry(i)
        for j in range(N_PAIRS):
            lanes = slice(PAIR_W * j, PAIR_W * (j + 1))
            qs = _split_pair(q_ref[pl.ds(q0, GRID_W), lanes], lo)
            bias = jnp.concatenate(
                [jnp.concatenate(
                    [bias_ref[(2 * j) * N_DR_PAIRS + d0 + 2 * jj],
                     bias_ref[(2 * j + 1) * N_DR_PAIRS + d0 + 2 * jj]], axis=0)
                 for jj in range(WIN_ROWS // 2)], axis=1)
            s_loc = _dot_nt(qs, k_ref[pl.ds(k0, N_LOCAL), lanes]) + bias
            s_ctx = _dot(qs, ckt_ref[lanes, :])
            mx = jnp.maximum(jnp.max(s_loc, axis=-1, keepdims=True),
                             jnp.max(s_ctx, axis=-1, keepdims=True))
            s_scr[slot, j, :, 0:N_LOCAL] = s_loc
            s_scr[slot, j, :, N_LOCAL:N_KEYS] = s_ctx
            m_scr[slot, j] = jnp.broadcast_to(mx, (2 * GRID_W, PAIR_W))

    def probs(slot):
        for j in range(N_PAIRS):
            mx = m_scr[slot, j][:, 0:1]
            p_scr[slot, j] = jnp.exp2((s_scr[slot, j] - mx).astype(BF16))

    def values(i, slot):
        _, q0, k0 = row_geometry(i)
        for j in range(N_PAIRS):
            lanes2 = slice(2 * PAIR_W * j, 2 * PAIR_W * (j + 1))
            p = p_scr[slot, j]
            o2 = (_dot(p[:, 0:N_LOCAL], v_ref[pl.ds(k0, N_LOCAL), lanes2])
                  + _dot(p[:, N_LOCAL:N_KEYS], cv_ref[:, lanes2]))
            o = o2[:, 0:PAIR_W] / o2[:, PAIR_W:2 * PAIR_W]
            att_scr[pl.ds(q0, GRID_W), PAIR_W * j:PAIR_W * (j + 1)] = (
                jnp.where(lo, o[:GRID_W], o[GRID_W:]).astype(BF16))

    scores(0, 0)
    probs(0)
    scores(1, 1)

    def two_rows(t, carry):
        i = 2 * t
        scores(i, 0)
        probs(1)
        values(i - 2, 0)
        scores(i + 1, 1)
        values(i - 1, 1)
        probs(0)
        return carry

    lax.fori_loop(1, rows_per_step // 2, two_rows, 0)
    values(rows_per_step - 2, 0)
    probs(1)
    values(rows_per_step - 1, 1)

    gt = mod_ref[...][:, 2 * D_MODEL:3 * D_MODEL]
    mix = _dot(att_scr[...], wout_ref[...])
    o_ref[...] = x_ref[...] + gt * _rms(mix, g_ref[1:2, :])


def _na_attn(q, k, v, ck, cv, bias, x, mod, row_of_batch, g, wout, rows_per_step=8):
    tm = rows_per_step * GRID_W
    n_t = DEC_SEQ // tm
    tok = pl.BlockSpec((tm, D_MODEL), lambda b, t: (b * n_t + t, 0))

    def per_batch(rows, width):
        return pl.BlockSpec((rows, width), lambda b, t: (b, 0), pipeline_mode=pl.Buffered(1))

    def kv_window(width):
        def start(b, t):
            row = _kv_window_start(t, rows_per_step)
            return (pl.multiple_of(b * DEC_SEQ + row * GRID_W, GRID_W), 0)
        return pl.BlockSpec(
            (pl.Element((rows_per_step + WIN_ROWS) * GRID_W), pl.Element(width)), start)

    return pl.pallas_call(
        functools.partial(_na_attn_kernel, rows_per_step=rows_per_step),
        grid=(DEC_BATCH, n_t),
        in_specs=[
            tok,
            kv_window(D_MODEL), kv_window(2 * D_MODEL),
            per_batch(D_MODEL, PAST_LEN), per_batch(PAST_LEN, 2 * D_MODEL),
            _const_spec((N_HEADS * N_DR_PAIRS, GRID_W, PAIR_W)),
            tok,
            _mod_spec(lambda b, t: row_of_batch(b)),
            _const_spec((4, D_MODEL)),
            _const_spec((D_MODEL, D_MODEL)),
        ],
        out_specs=tok,
        out_shape=jax.ShapeDtypeStruct((DEC_BATCH * DEC_SEQ, D_MODEL), F32),
        scratch_shapes=[
            pltpu.VMEM((2, N_PAIRS, 2 * GRID_W, N_KEYS), F32),
            pltpu.VMEM((2, N_PAIRS, 2 * GRID_W, PAIR_W), F32),
            pltpu.VMEM((2, N_PAIRS, 2 * GRID_W, N_KEYS), BF16),
            pltpu.VMEM((tm, D_MODEL), BF16),
        ],
        compiler_params=_params(2),
        name="na_attn",
    )(q, k, v, ck, cv, bias, x, mod, g, wout)


def kernel(x_prompt, x_sample, c, cache_k, cache_v, c_ctx, ada_w, ada_b, norm_g, fourier_w_out,
           na_w_qkv, na_rpb, na_w_out, ffn_w_gate, ffn_w_up, ffn_w_down):
    n_p = BATCH * SEQ
    n_s = DEC_BATCH * DEC_SEQ
    xp = x_prompt.reshape(n_p, D_MODEL)
    xs = x_sample.reshape(n_s, D_MODEL)

    cond = jnp.concatenate(
        [c_ctx[None, :], c, jnp.zeros((COND_ROWS - 1 - DEC_BATCH, D_MODEL), F32)], axis=0)
    mods = {0: _modulation(cond, ada_w, ada_b, 0)}

    cos_g, sin_g = _dft_cos_sin(FOURIER_GROUP)
    cs_chan = jnp.asarray(np.concatenate([cos_g, sin_g], axis=1), F32).astype(BF16)
    cos_p, sin_p = (jnp.asarray(a, F32).astype(BF16) for a in _dft_cos_sin(SEQ))
    cos_s, sin_s = (jnp.asarray(_half_spectrum_blocks(a), F32).astype(BF16)
                    for a in _dft_cos_sin(DEC_SEQ))
    rev = jnp.asarray(_reversal_matrix(), F32).astype(BF16)

    tm, ffn_tm = TOKEN_TILE, FFN_TILE
    ffn_weights = (ffn_w_gate, ffn_w_up, ffn_w_down)
    cache_kt = jnp.transpose(cache_k, (0, 1, 3, 4, 2)).reshape(DEC_BATCH, -1, D_MODEL, PAST_LEN)
    cache_vt = jnp.transpose(cache_v, (0, 1, 3, 4, 2)).reshape(DEC_BATCH, -1, D_MODEL, PAST_LEN)
    new_kt = new_vt = None
    attn_weights = None
    for layer in range(DEPTH):
        g = norm_g[layer]
        if layer not in mods:
            mods[layer] = _modulation(cond, ada_w, ada_b, layer)
        mod = mods[layer]
        prompt_row = lambda *_: 0
        sample_row_of_batch = lambda b: 1 + b
        sample_row_of_tile = lambda i: 1 + i // (DEC_SEQ // tm)

        if layer % 2 == 0:
            w_out = fourier_w_out[layer // 2]
            jobs = [(w, layer) for w in ffn_weights] if layer == 0 else []
            xp, cast_p = _fourier_mixer(cos_p, sin_p, cs_chan, xp, mod, prompt_row, g, w_out,
                                        BATCH, SEQ, n_sub=4, cast_jobs=jobs[:2])
            xs, cast_s = _fourier_mixer_long(cos_s, sin_s, rev, cs_chan, xs, mod,
                                             sample_row_of_batch, g, w_out, DEC_BATCH, DEC_SEQ,
                                             cast_jobs=jobs[2:])
            if layer == 0:
                wg, wu, wd = list(cast_p) + list(cast_s)
        else:
            j = layer // 2
            if attn_weights is None:
                attn_weights = (na_w_qkv[j].astype(BF16), na_w_out[j].astype(BF16))
            w_qkv, w_out = attn_weights
            attn_weights = None
            xp, new_kt, new_vt = _prompt_attn(xp, mod, 0, g, w_qkv, w_out)
            q, k, v, ckt, cvt, bias = _premix_qkv(xs, mod, sample_row_of_tile, g, w_qkv,
                                                  cache_kt, cache_vt, j, na_rpb[j], tm)
            xs = _na_attn(q, k, v, ckt, cvt, bias, xs, mod, sample_row_of_batch, g, w_out)

        nxt = layer + 1
        ffn_jobs = [(w, nxt) for w in ffn_weights] if nxt < DEPTH else []
        attn_jobs = ([(na_w_qkv, nxt // 2), (na_w_out, nxt // 2)]
                     if nxt < DEPTH and nxt % 2 == 1 else [])
        mod_job = (cond, ada_w, ada_b, nxt) if nxt < DEPTH else None
        busy = bool(ffn_jobs) or bool(attn_jobs) or mod_job is not None
        xp, xs, casts, next_mod = _ffn_pair(
            xp, xs, mod, g, wg, wu, wd, tm if busy else ffn_tm,
            prompt_jobs=ffn_jobs, latent_jobs=attn_jobs, mod_job=mod_job)
        if ffn_jobs:
            wg, wu, wd = casts[:len(ffn_jobs)]
        if attn_jobs:
            attn_weights = tuple(casts[len(ffn_jobs):])
        if mod_job is not None:
            mods[nxt] = next_mod

    new_k = jnp.transpose(new_kt, (0, 1, 4, 2, 3))
    new_v = jnp.transpose(new_vt, (0, 1, 4, 2, 3))
    return (xp.reshape(BATCH, SEQ, D_MODEL), xs.reshape(DEC_BATCH, DEC_SEQ, D_MODEL), new_k, new_v)
```
